```python
import math
import jax, jax.numpy as jnp
from jax import lax
import numpy as np

D_MODEL = 1024
BATCH = 4
SEQ = 8192
DEPTH = 1
DEC_BATCH = 16
DEC_SEQ = 64
PAST_LEN = 1024

CHUNK = 64
FOX_HEADS = 8
FOX_HEAD_DIM = 64
FOX_WIDTH = FOX_HEADS * FOX_HEAD_DIM
HGRN_HEADS = 4
HGRN_EXPAND = 128
HGRN_WIDTH = HGRN_HEADS * HGRN_EXPAND
HGRN_HEAD_V = HGRN_WIDTH // HGRN_HEADS
N_EXPERTS = 256
TOP_K = 8
N_GROUPS = 8
TOPK_GROUPS = 4
D_EXPERT = 256
D_SHARED = 256
ROUTED_SCALE = 2.5
Q_BLOCK = 128
NORM_EPS = 1e-6
IN_COLS = 3 * FOX_WIDTH + FOX_HEADS + 4 * HGRN_WIDTH + 2 * D_MODEL

kernel_name = 'fox_hgrn2_moe_adaln_stream_step'


def rmsnorm(x, g):
    xf = x.astype(jnp.float32)
    y = xf * lax.rsqrt(jnp.mean(xf * xf, axis=-1, keepdims=True) + NORM_EPS)
    return (y * g.astype(jnp.float32)).astype(x.dtype)


def split_cols(proj):
    sizes = [FOX_WIDTH, FOX_WIDTH, FOX_WIDTH, FOX_HEADS,
             HGRN_WIDTH, HGRN_WIDTH, HGRN_WIDTH, HGRN_WIDTH, D_MODEL, D_MODEL]
    idx = []
    acc = 0
    for s in sizes[:-1]:
        acc += s
        idx.append(acc)
    return jnp.split(proj, idx, axis=-1)


def fox_attention(q, k, v, fq, fk, q_offset):
    sq = q.shape[1]
    lk = k.shape[1]
    scale = FOX_HEAD_DIM ** -0.5
    outs = []
    for b0 in range(0, sq, Q_BLOCK):
        qn = min(Q_BLOCK, sq - b0)
        nk = min(lk, q_offset + b0 + qn)
        s = jnp.einsum('bqhd,bkhd->bhqk', q[:, b0:b0 + qn], k[:, :nk]).astype(jnp.float32) * scale
        bias = (jnp.transpose(fq[:, b0:b0 + qn], (0, 2, 1))[:, :, :, None]
                - jnp.transpose(fk[:, :nk], (0, 2, 1))[:, :, None, :])
        q_pos = q_offset + b0 + jnp.arange(qn)
        k_pos = jnp.arange(nk)
        mask = k_pos[None, :] <= q_pos[:, None]
        p = jax.nn.softmax(jnp.where(mask, s + bias, -jnp.inf), axis=-1).astype(v.dtype)
        outs.append(jnp.einsum('bhqk,bkhd->bqhd', p, v[:, :nk]))
    return jnp.concatenate(outs, axis=1)


def hgrn2_recurrence(q, k, i, logf, s0):
    bsz, seq, nh, dk = q.shape
    dv = i.shape[-1]
    L = min(CHUNK, seq)
    n = seq // L

    def to_blocks(a):
        return jnp.transpose(a.reshape(bsz, n, L, nh, a.shape[-1]), (1, 0, 3, 2, 4)).astype(jnp.float32)

    tri = jnp.tril(jnp.ones((L, L), dtype=bool))[:, :, None]

    def step(state, xs):
        qc, kc, ic, gc = xs
        b = jnp.cumsum(gc, axis=2)
        inter = jnp.einsum('bhtd,bhde->bhte', qc * jnp.exp(b), state)
        diff = b[:, :, :, None, :] - b[:, :, None, :, :]
        decay = jnp.exp(jnp.where(tri, diff, -jnp.inf))
        a = jnp.einsum('bhtd,bhsd,bhtsd->bhts', qc, kc, decay)
        o = inter + jnp.einsum('bhts,bhse->bhte', a, ic)
        b_last = b[:, :, -1:, :]
        new_state = (jnp.exp(b_last[:, :, 0, :])[..., None] * state
                     + jnp.einsum('bhsd,bhse->bhde', kc * jnp.exp(b_last - b), ic))
        return new_state, o

    s_fin, o = lax.scan(step, s0.astype(jnp.float32),
                        (to_blocks(q), to_blocks(k), to_blocks(i), to_blocks(logf)))
    o = jnp.transpose(o, (1, 0, 3, 2, 4)).reshape(bsz, seq, nh, dv)
    return o, s_fin


def swiglu(x, wg, wu, wd):
    return (jax.nn.silu(x @ wg) * (x @ wu)) @ wd


def routed_experts(x, idx, wts, w_gate, w_up, w_down):
    t, d = x.shape
    a = t * TOP_K
    blk = int(min(128, max(8, 2 ** int(math.log2(max(1, a // N_EXPERTS))))))
    n_blocks = -(-(a + N_EXPERTS * (blk - 1)) // blk)
    rows = n_blocks * blk
    e_flat = idx.reshape(-1)
    tok_flat = jnp.arange(a, dtype=jnp.int32) // TOP_K
    w_flat = wts.reshape(-1)
    order = jnp.argsort(e_flat)
    e_sorted = e_flat[order]
    counts = jnp.bincount(e_flat, length=N_EXPERTS)
    starts = jnp.cumsum(counts) - counts
    padded = (counts + blk - 1) // blk * blk
    pend = jnp.cumsum(padded)
    pstart = pend - padded
    dest = pstart[e_sorted] + jnp.arange(a) - starts[e_sorted]
    row_tok = jnp.full((rows,), t, dtype=jnp.int32).at[dest].set(tok_flat[order])
    row_w = jnp.zeros((rows,), jnp.float32).at[dest].set(w_flat[order])
    blk_expert = jnp.minimum(jnp.searchsorted(pend, jnp.arange(n_blocks) * blk, side='right'), N_EXPERTS - 1)
    x_pad = jnp.concatenate([x, jnp.zeros((1, d), x.dtype)], axis=0)

    def body(y, xs):
        tok, w, e = xs
        xb = x_pad[tok]
        yb = swiglu(xb, w_gate[e], w_up[e], w_down[e]) * w[:, None].astype(x.dtype)
        return y.at[tok].add(yb), None

    y, _ = lax.scan(body, jnp.zeros((t + 1, d), x.dtype),
                    (row_tok.reshape(n_blocks, blk), row_w.reshape(n_blocks, blk), blk_expert))
    return y[:t]


def moe_ffn(h, w_router, b_router, w_exp_gate, w_exp_up, w_exp_down, w_sh_gate, w_sh_up, w_sh_down):
    bsz, seq, d = h.shape
    x = h.reshape(bsz * seq, d)
    t = x.shape[0]
    scores = jax.nn.sigmoid((x @ w_router).astype(jnp.float32))
    biased = scores + b_router.astype(jnp.float32)
    gscore = lax.top_k(biased.reshape(t, N_GROUPS, N_EXPERTS // N_GROUPS), 2)[0].sum(-1)
    _, gidx = lax.top_k(gscore, TOPK_GROUPS)
    gmask = jax.nn.one_hot(gidx, N_GROUPS, dtype=jnp.float32).sum(1)
    emask = jnp.repeat(gmask, N_EXPERTS // N_GROUPS, axis=1) > 0
    _, idx = lax.top_k(jnp.where(emask, biased, -jnp.inf), TOP_K)
    wts = jnp.take_along_axis(scores, idx, axis=1)
    wts = wts / jnp.sum(wts, axis=-1, keepdims=True) * ROUTED_SCALE
    routed = routed_experts(x, idx, wts, w_exp_gate, w_exp_up, w_exp_down)
    shared = swiglu(x, w_sh_gate, w_sh_up, w_sh_down)
    return (routed + shared).reshape(bsz, seq, d)


def trunk_layer(x, c, past, lb, w_ada, b_ada, g_norm1, w_in, b_fox_f, g_q, g_k, g_hgrn_o,
                w_proj_a, w_proj_b, w_out, g_norm2, w_router, b_router,
                w_exp_gate, w_exp_up, w_exp_down, w_sh_gate, w_sh_up, w_sh_down):
    f32 = jnp.float32
    bsz, seq, _ = x.shape
    mod = (jax.nn.silu(c) @ w_ada + b_ada)[:, None, :]
    shift1, scale1, gate1, shift2, scale2, gate2 = jnp.split(mod, 6, axis=-1)
    h = rmsnorm(x, g_norm1) * (1 + scale1) + shift1
    fq, fk, fv, ff, hq, hf, hi, hg, ga, gb = split_cols(h @ w_in)
    q = rmsnorm(fq.reshape(bsz, seq, FOX_HEADS, FOX_HEAD_DIM), g_q)
    k = rmsnorm(fk.reshape(bsz, seq, FOX_HEADS, FOX_HEAD_DIM), g_k)
    v = fv.reshape(bsz, seq, FOX_HEADS, FOX_HEAD_DIM)
    logf = jax.nn.log_sigmoid((ff + b_fox_f).astype(f32))
    if past is None:
        k_all, v_all, logf_all, offset = k, v, logf, 0
        s0 = jnp.zeros((bsz, HGRN_HEADS, HGRN_EXPAND, HGRN_HEAD_V), f32)
    else:
        pk, pv, plogf, s0 = past
        k_all = jnp.concatenate([pk.astype(k.dtype), k], axis=1)
        v_all = jnp.concatenate([pv.astype(v.dtype), v], axis=1)
        logf_all = jnp.concatenate([plogf.astype(f32), logf], axis=1)
        offset = pk.shape[1]
    fcum = jnp.cumsum(logf_all, axis=1)
    y_a = fox_attention(q, k_all, v_all, fcum[:, offset:], fcum, offset).reshape(bsz, seq, FOX_WIDTH)
    fgate = lb + (1.0 - lb) * jax.nn.sigmoid(hf.astype(f32))
    kshape = (bsz, seq, HGRN_HEADS, HGRN_EXPAND)
    vshape = (bsz, seq, HGRN_HEADS, HGRN_HEAD_V)
    o, s_fin = hgrn2_recurrence(jax.nn.silu(hq).reshape(kshape), (1.0 - fgate).reshape(kshape),
                                hi.reshape(vshape), jnp.log(fgate).reshape(kshape), s0)
    o = rmsnorm(o.astype(x.dtype), g_hgrn_o) * jax.nn.sigmoid(hg).reshape(vshape)
    y_b = o.reshape(bsz, seq, HGRN_WIDTH)
    merged = jax.nn.sigmoid(ga) * (y_a @ w_proj_a) + jax.nn.sigmoid(gb) * (y_b @ w_proj_b)
    x = x + gate1 * (merged @ w_out)
    h2 = rmsnorm(x, g_norm2) * (1 + scale2) + shift2
    x = x + gate2 * moe_ffn(h2, w_router, b_router, w_exp_gate, w_exp_up, w_exp_down,
                            w_sh_gate, w_sh_up, w_sh_down)
    return x, k, v, logf.astype(x.dtype), s_fin


def setup_inputs(seed: int = 0) -> dict:
    key = jax.random.key(seed)
    ks = jax.random.split(key, 32)
    D = D_MODEL

    def nrm(k, shape, s):
        return jax.random.normal(k, shape, jnp.float32) * s

    return {
        'x_prompt': nrm(ks[0], (BATCH, SEQ, D), 1.0),
        'x_sample': nrm(ks[1], (DEC_BATCH, DEC_SEQ, D), 1.0),
        'cache_fox_k': nrm(ks[2], (DEPTH, DEC_BATCH, PAST_LEN, FOX_HEADS, FOX_HEAD_DIM), 1.0),
        'cache_fox_v': nrm(ks[3], (DEPTH, DEC_BATCH, PAST_LEN, FOX_HEADS, FOX_HEAD_DIM), 1.0),
        'cache_fox_logf': jax.nn.log_sigmoid(3.0 + nrm(ks[4], (DEPTH, DEC_BATCH, PAST_LEN, FOX_HEADS), 1.0)),
        'state_hgrn': nrm(ks[5], (DEPTH, DEC_BATCH, HGRN_HEADS, HGRN_EXPAND, HGRN_HEAD_V), 0.5),
        'c_prompt': nrm(ks[6], (BATCH, D), 1.0),
        'c_sample': nrm(ks[7], (DEC_BATCH, D), 1.0),
        'w_ada': nrm(ks[8], (DEPTH, D, 6 * D), 0.5 * D ** -0.5),
        'b_ada': nrm(ks[9], (DEPTH, 6 * D), 0.02),
        'g_norm1': 1.0 + nrm(ks[10], (DEPTH, D), 0.05),
        'w_in': nrm(ks[11], (DEPTH, D, IN_COLS), D ** -0.5),
        'b_fox_f': 3.0 + nrm(ks[12], (DEPTH, FOX_HEADS), 0.5),
        'g_q': 1.0 + nrm(ks[13], (DEPTH, FOX_HEAD_DIM), 0.05),
        'g_k': 1.0 + nrm(ks[14], (DEPTH, FOX_HEAD_DIM), 0.05),
        'hgrn_lb': nrm(ks[15], (DEPTH + 1, HGRN_WIDTH), 0.5),
        'g_hgrn_o': 1.0 + nrm(ks[16], (DEPTH, HGRN_HEAD_V), 0.05),
        'w_proj_a': nrm(ks[17], (DEPTH, FOX_WIDTH, D), FOX_WIDTH ** -0.5),
        'w_proj_b': nrm(ks[18], (DEPTH, HGRN_WIDTH, D), HGRN_WIDTH ** -0.5),
        'w_out': nrm(ks[19], (DEPTH, D, D), D ** -0.5),
        'g_norm2': 1.0 + nrm(ks[20], (DEPTH, D), 0.05),
        'w_router': nrm(ks[21], (DEPTH, D, N_EXPERTS), D ** -0.5),
        'b_router': nrm(ks[22], (DEPTH, N_EXPERTS), 0.01),
        'w_exp_gate': nrm(ks[23], (DEPTH, N_EXPERTS, D, D_EXPERT), D ** -0.5),
        'w_exp_up': nrm(ks[24], (DEPTH, N_EXPERTS, D, D_EXPERT), D ** -0.5),
        'w_exp_down': nrm(ks[25], (DEPTH, N_EXPERTS, D_EXPERT, D), D_EXPERT ** -0.5),
        'w_sh_gate': nrm(ks[26], (DEPTH, D, D_SHARED), D ** -0.5),
        'w_sh_up': nrm(ks[27], (DEPTH, D, D_SHARED), D ** -0.5),
        'w_sh_down': nrm(ks[28], (DEPTH, D_SHARED, D), D_SHARED ** -0.5),
    }


def reference(x_prompt, x_sample, cache_fox_k, cache_fox_v, cache_fox_logf, state_hgrn, c_prompt, c_sample,
              w_ada, b_ada, g_norm1, w_in, b_fox_f, g_q, g_k, hgrn_lb, g_hgrn_o, w_proj_a, w_proj_b, w_out,
              g_norm2, w_router, b_router, w_exp_gate, w_exp_up, w_exp_down, w_sh_gate, w_sh_up, w_sh_down):
    lb_all = jnp.cumsum(jax.nn.softmax(hgrn_lb.astype(jnp.float32), axis=0), axis=0)
    xp, xs = x_prompt, x_sample
    kp_l, vp_l, fp_l, sp_l = [], [], [], []
    ks_l, vs_l, fs_l, ss_l = [], [], [], []
    for l in range(DEPTH):
        lw = (w_ada[l], b_ada[l], g_norm1[l], w_in[l], b_fox_f[l], g_q[l], g_k[l], g_hgrn_o[l],
              w_proj_a[l], w_proj_b[l], w_out[l], g_norm2[l], w_router[l], b_router[l],
              w_exp_gate[l], w_exp_up[l], w_exp_down[l], w_sh_gate[l], w_sh_up[l], w_sh_down[l])
        xp, kp, vp, fp, sp = trunk_layer(xp, c_prompt, None, lb_all[l], *lw)
        past = (cache_fox_k[l], cache_fox_v[l], cache_fox_logf[l], state_hgrn[l])
        xs, kn, vn, fn, sn = trunk_layer(xs, c_sample, past, lb_all[l], *lw)
        kp_l.append(kp); vp_l.append(vp); fp_l.append(fp); sp_l.append(sp)
        ks_l.append(kn); vs_l.append(vn); fs_l.append(fn); ss_l.append(sn)
    new_fox_k_prompt = jnp.stack(kp_l, axis=0)
    new_fox_v_prompt = jnp.stack(vp_l, axis=0)
    new_fox_logf_prompt = jnp.stack(fp_l, axis=0)
    new_hgrn_prompt = jnp.stack(sp_l, axis=0)
    new_fox_k_sample = jnp.stack(ks_l, axis=0)
    new_fox_v_sample = jnp.stack(vs_l, axis=0)
    new_fox_logf_sample = jnp.stack(fs_l, axis=0)
    new_hgrn_sample = jnp.stack(ss_l, axis=0)
    return (xp, xs, new_fox_k_prompt, new_fox_v_prompt, new_fox_logf_prompt, new_hgrn_prompt,
            new_fox_k_sample, new_fox_v_sample, new_fox_logf_sample, new_hgrn_sample)
```

```python
import functools

import jax
import jax.numpy as jnp
from jax import lax
from jax.experimental import pallas as pl
from jax.experimental.pallas import tpu as pltpu

F32 = jnp.float32
BF16 = jnp.bfloat16
I32 = jnp.int32

D_MODEL = 1024
FOX_HEADS = 8
FOX_HEAD_DIM = 64
FOX_WIDTH = FOX_HEADS * FOX_HEAD_DIM
HGRN_HEADS = 4
HGRN_DK = 128
HGRN_WIDTH = HGRN_HEADS * HGRN_DK
N_EXPERTS = 256
TOP_K = 8
N_GROUPS = 8
TOPK_GROUPS = 4
GROUP_SIZE = N_EXPERTS // N_GROUPS
D_EXPERT = 256
D_SHARED = 256
ROUTED_SCALE = 2.5
NORM_EPS = 1e-6
NEG_BIG = -1e30

LANES = 128
SUB = 16
MOE_BLK = 256
VMEM_LIMIT = 56 * 1024 * 1024

NT_DIMS = (((1,), (1,)), ((), ()))
TN_DIMS = (((0,), (0,)), ((), ()))


def _cparams(sem):
    return pltpu.CompilerParams(dimension_semantics=sem, vmem_limit_bytes=VMEM_LIMIT)


def _const_spec(shape):
    nd = len(shape)
    return pl.BlockSpec(shape, lambda *_: (0,) * nd, pipeline_mode=pl.Buffered(1))


def _sigmoid(z):
    return 1.0 / (1.0 + jnp.exp(-z))


def _log_sigmoid(z):
    return jnp.minimum(z, 0.0) - jnp.log(1.0 + jnp.exp(-jnp.abs(z)))


def _split3(a):
    hi = a.astype(BF16)
    r1 = a - hi.astype(F32)
    mid = r1.astype(BF16)
    lo = (r1 - mid.astype(F32)).astype(BF16)
    return hi, mid, lo


def _ada_kernel(c_ref, w_ref, b_ref, o_ref):
    c = c_ref[...]
    s = c * _sigmoid(c)
    o_ref[...] = jnp.dot(s.astype(BF16), w_ref[...].astype(BF16), preferred_element_type=F32) + b_ref[...]


def _ada(c_all, w_ada, b_ada):
    bc, d = c_all.shape
    n = w_ada.shape[1]
    tn = 1024
    return pl.pallas_call(
        _ada_kernel,
        grid=(n // tn,),
        in_specs=[pl.BlockSpec((bc, d), lambda j: (0, 0)),
                  pl.BlockSpec((d, tn), lambda j: (0, j)),
                  pl.BlockSpec((1, tn), lambda j: (0, j))],
        out_specs=pl.BlockSpec((bc, tn), lambda j: (0, j)),
        out_shape=jax.ShapeDtypeStruct((bc, n), F32),
        compiler_params=_cparams(("parallel",)),
        name="ada",
    )(c_all, w_ada, b_ada.reshape(1, n))


def _inproj_kernel(x_ref, sh_ref, sc_ref, g1_ref, wqkv_ref, wff_ref, wfft_ref, bf_ref, bft_ref,
                   gq_ref, gk_ref, bd_ref, lbp_ref, wh_ref, wg_ref,
                   q_ref, k_ref, kb_ref, v_ref, vb_ref, lf_ref, lft_ref,
                   hq_ref, hl_ref, hi_ref, og_ref, ga_ref, gb_ref):
    x = x_ref[...]
    ms = jnp.mean(x * x, axis=-1, keepdims=True)
    h = x * lax.rsqrt(ms + NORM_EPS) * g1_ref[...]
    h = h * (1.0 + sc_ref[0]) + sh_ref[0]
    hb = h.astype(BF16)

    def headnorm(a, g):
        ss = jnp.dot((a * a).astype(BF16), bd_ref[...], preferred_element_type=F32)
        return a * lax.rsqrt(ss + NORM_EPS) * g

    fq = jnp.dot(hb, wqkv_ref[:, 0:FOX_WIDTH], preferred_element_type=F32)
    q_ref[...] = (headnorm(fq, gq_ref[...]) * (FOX_HEAD_DIM ** -0.5)).astype(BF16)
    fk = jnp.dot(hb, wqkv_ref[:, FOX_WIDTH:2 * FOX_WIDTH], preferred_element_type=F32)
    k = headnorm(fk, gk_ref[...])
    k_ref[...] = k
    kb_ref[...] = k.astype(BF16)
    fv = jnp.dot(hb, wqkv_ref[:, 2 * FOX_WIDTH:3 * FOX_WIDTH], preferred_element_type=F32)
    v_ref[...] = fv
    vb_ref[...] = fv.astype(BF16)

    ff = jnp.dot(hb, wff_ref[...], preferred_element_type=F32)
    lf_ref[...] = _log_sigmoid(ff[:, 0:FOX_HEADS] + bf_ref[...])
    fft = lax.dot_general(wfft_ref[...], hb, NT_DIMS, preferred_element_type=F32)
    lft_ref[0] = _log_sigmoid(fft[0:FOX_HEADS, :] + bft_ref[...])

    lbp = lbp_ref[...]
    e = jnp.exp(lbp - jnp.max(lbp, axis=0, keepdims=True))
    lb = e[0:1, :] / jnp.sum(e, axis=0, keepdims=True)

    w = HGRN_WIDTH
    hq = jnp.dot(hb, wh_ref[:, 0:w], preferred_element_type=F32)
    hq_ref[...] = hq * _sigmoid(hq)
    hf = jnp.dot(hb, wh_ref[:, w:2 * w], preferred_element_type=F32)
    hl_ref[...] = jnp.log(lb + (1.0 - lb) * _sigmoid(hf))
    hi_ref[...] = jnp.dot(hb, wh_ref[:, 2 * w:3 * w], preferred_element_type=F32)
    hg = jnp.dot(hb, wh_ref[:, 3 * w:4 * w], preferred_element_type=F32)
    og_ref[...] = _sigmoid(hg)
    ga = jnp.dot(hb, wg_ref[:, 0:D_MODEL], preferred_element_type=F32)
    ga_ref[...] = _sigmoid(ga).astype(BF16)
    gb = jnp.dot(hb, wg_ref[:, D_MODEL:2 * D_MODEL], preferred_element_type=F32)
    gb_ref[...] = _sigmoid(gb).astype(BF16)


def _mod_spec(arr, tm, ns):
    d = arr.shape[-1]
    if arr.shape[1] == 1:
        return pl.BlockSpec((1, 1, d), lambda b, i: (b, 0, 0))
    return pl.BlockSpec((1, tm, d), lambda b, i: (0, b * ns + i, 0))


def _inproj(x2, nb, tm, shift1, scale1, prm):
    t, d = x2.shape
    ns = t // nb // tm
    row = lambda w: pl.BlockSpec((tm, w), lambda b, i: (b * ns + i, 0))
    in_specs = [row(d), _mod_spec(shift1, tm, ns), _mod_spec(scale1, tm, ns),
                _const_spec((1, d)), _const_spec(prm["wqkv"].shape), _const_spec(prm["wff"].shape),
                _const_spec(prm["wfft"].shape), _const_spec((1, FOX_HEADS)), _const_spec((FOX_HEADS, 1)),
                _const_spec((1, FOX_WIDTH)), _const_spec((1, FOX_WIDTH)), _const_spec((FOX_WIDTH, FOX_WIDTH)),
                _const_spec(prm["lbp"].shape), _const_spec(prm["wh"].shape), _const_spec(prm["wg"].shape)]
    fw, hw = FOX_WIDTH, HGRN_WIDTH
    out_shape = [jax.ShapeDtypeStruct((t, fw), BF16),
                 jax.ShapeDtypeStruct((t, fw), F32),
                 jax.ShapeDtypeStruct((t, fw), BF16),
                 jax.ShapeDtypeStruct((t, fw), F32),
                 jax.ShapeDtypeStruct((t, fw), BF16),
                 jax.ShapeDtypeStruct((t, FOX_HEADS), F32),
                 jax.ShapeDtypeStruct((nb * ns, FOX_HEADS, tm), F32),
                 jax.ShapeDtypeStruct((t, hw), F32),
                 jax.ShapeDtypeStruct((t, hw), F32),
                 jax.ShapeDtypeStruct((t, hw), F32),
                 jax.ShapeDtypeStruct((t, hw), F32),
                 jax.ShapeDtypeStruct((t, d), BF16),
                 jax.ShapeDtypeStruct((t, d), BF16)]
    out_specs = [row(fw), row(fw), row(fw), row(fw), row(fw), row(FOX_HEADS),
                 pl.BlockSpec((1, FOX_HEADS, tm), lambda b, i: (b * ns + i, 0, 0)),
                 row(hw), row(hw), row(hw), row(hw), row(d), row(d)]
    return pl.pallas_call(
        _inproj_kernel, grid=(nb, ns), in_specs=in_specs, out_specs=out_specs, out_shape=out_shape,
        compiler_params=_cparams(("parallel", "parallel")), name="inproj",
    )(x2, shift1, scale1, prm["g1"], prm["wqkv"], prm["wff"], prm["wfft"], prm["bf"], prm["bft"],
      prm["gq"], prm["gk"], prm["bd"], prm["lbp"], prm["wh"], prm["wg"])


def _cumsum_kernel(x_ref, o_ref):
    x = x_ref[...]
    n = x.shape[-1]
    lane = lax.broadcasted_iota(I32, x.shape, 1)
    s = 1
    while s < n:
        x = x + jnp.where(lane >= s, pltpu.roll(x, s, axis=1), 0.0)
        s *= 2
    o_ref[...] = x


def _cumsum_lanes(x):
    return pl.pallas_call(
        _cumsum_kernel, out_shape=jax.ShapeDtypeStruct(x.shape, F32),
        compiler_params=pltpu.CompilerParams(vmem_limit_bytes=VMEM_LIMIT), name="cumsum",
    )(x)


def _fox_prompt_kernel(q_ref, k_ref, v_ref, fk_ref, o_ref, m_sc, l_sc, acc_sc, *, tq, tk):
    qi = pl.program_id(2)
    ki = pl.program_id(3)
    last = (qi * tq + tq - 1) // tk
    lane = lax.broadcasted_iota(I32, (1, LANES), 1)
    lo_half = lane < FOX_HEAD_DIM

    @pl.when(ki == 0)
    def _init():
        m_sc[...] = jnp.full(m_sc.shape, NEG_BIG, F32)
        l_sc[...] = jnp.zeros(l_sc.shape, F32)
        acc_sc[...] = jnp.zeros(acc_sc.shape, F32)

    @pl.when(ki <= last)
    def _compute():
        q = q_ref[...]
        k = k_ref[...]
        v = v_ref[...]
        fk = fk_ref[0, 0]
        row = qi * tq + lax.broadcasted_iota(I32, (tq, tk), 0)
        col = ki * tk + lax.broadcasted_iota(I32, (tq, tk), 1)
        causal = col <= row
        for hs in range(2):
            sel = lo_half if hs == 0 else jnp.logical_not(lo_half)
            qh = jnp.where(sel, q, jnp.zeros_like(q))
            s = lax.dot_general(qh, k, NT_DIMS, preferred_element_type=F32)
            s = jnp.where(causal, s - fk[hs:hs + 1, :], NEG_BIG)
            m_prev = m_sc[hs]
            m_new = jnp.maximum(m_prev, jnp.max(s, axis=1, keepdims=True))
            alpha = jnp.exp(m_prev - m_new)
            p = jnp.exp(s - m_new)
            l_sc[hs] = alpha * l_sc[hs] + jnp.sum(p, axis=1, keepdims=True)
            acc_sc[hs] = alpha * acc_sc[hs] + jnp.dot(p.astype(BF16), v, preferred_element_type=F32)
            m_sc[hs] = m_new

    @pl.when(ki == last)
    def _fin():
        o = jnp.where(lo_half, acc_sc[0] / l_sc[0], acc_sc[1] / l_sc[1])
        o_ref[...] = o.astype(o_ref.dtype)


def _fox_prompt(q, kb, vb, fcum, nb, seq, tq, tk):
    nq, nk = seq // tq, seq // tk
    hp = FOX_HEADS // 2
    last = lambda qi: (qi * tq + tq - 1) // tk
    qmap = lambda b, h, qi, ki: (b * nq + qi, h)
    kmap = lambda b, h, qi, ki: (b * nk + jnp.minimum(ki, last(qi)), h)
    fmap = lambda b, h, qi, ki: (b, h, 0, jnp.minimum(ki, last(qi)))
    return pl.pallas_call(
        functools.partial(_fox_prompt_kernel, tq=tq, tk=tk),
        grid=(nb, hp, nq, nk),
        in_specs=[pl.BlockSpec((tq, LANES), qmap), pl.BlockSpec((tk, LANES), kmap),
                  pl.BlockSpec((tk, LANES), kmap), pl.BlockSpec((1, 1, 2, tk), fmap)],
        out_specs=pl.BlockSpec((tq, LANES), qmap),
        out_shape=jax.ShapeDtypeStruct((nb * seq, FOX_WIDTH), BF16),
        scratch_shapes=[pltpu.VMEM((2, tq, 1), F32), pltpu.VMEM((2, tq, 1), F32),
                        pltpu.VMEM((2, tq, LANES), F32)],
        compiler_params=_cparams(("parallel", "parallel", "parallel", "arbitrary")),
        name="fox_prompt",
    )(q, kb, vb, fcum)


def _fox_sample_kernel(q_ref, kn_ref, vn_ref, kp_ref, vp_ref, fk_ref, o_ref, *, past, seq):
    lane = lax.broadcasted_iota(I32, (1, LANES), 1)
    lo_half = lane < FOX_HEAD_DIM
    q = q_ref[...]
    kn = kn_ref[...]
    vn = vn_ref[...]
    kp = kp_ref[0].astype(BF16)
    vp = vp_ref[0].astype(BF16)
    fk = fk_ref[0, 0]
    row = lax.broadcasted_iota(I32, (seq, seq), 0)
    col = lax.broadcasted_iota(I32, (seq, seq), 1)
    causal = col <= row
    outs = []
    for hs in range(2):
        sel = lo_half if hs == 0 else jnp.logical_not(lo_half)
        qh = jnp.where(sel, q, jnp.zeros_like(q))
        sp = lax.dot_general(qh, kp, NT_DIMS, preferred_element_type=F32) - fk[hs:hs + 1, 0:past]
        sn = lax.dot_general(qh, kn, NT_DIMS, preferred_element_type=F32) - fk[hs:hs + 1, past:past + seq]
        sn = jnp.where(causal, sn, NEG_BIG)
        m = jnp.maximum(jnp.max(sp, axis=1, keepdims=True), jnp.max(sn, axis=1, keepdims=True))
        pp = jnp.exp(sp - m)
        pn = jnp.exp(sn - m)
        l = jnp.sum(pp, axis=1, keepdims=True) + jnp.sum(pn, axis=1, keepdims=True)
        o = (jnp.dot(pp.astype(BF16), vp, preferred_element_type=F32)
             + jnp.dot(pn.astype(BF16), vn, preferred_element_type=F32))
        outs.append(o / l)
    o_ref[...] = jnp.where(lo_half, outs[0], outs[1]).astype(o_ref.dtype)


def _fox_sample(q, kb, vb, cache_k, cache_v, fcum, nb, seq, past):
    hp = FOX_HEADS // 2
    lpad = fcum.shape[-1]
    rmap = lambda b, h: (b, h)
    cmap = lambda b, h: (b, 0, h)
    return pl.pallas_call(
        functools.partial(_fox_sample_kernel, past=past, seq=seq),
        grid=(nb, hp),
        in_specs=[pl.BlockSpec((seq, LANES), rmap), pl.BlockSpec((seq, LANES), rmap),
                  pl.BlockSpec((seq, LANES), rmap),
                  pl.BlockSpec((1, past, LANES), cmap), pl.BlockSpec((1, past, LANES), cmap),
                  pl.BlockSpec((1, 1, 2, lpad), lambda b, h: (b, h, 0, 0))],
        out_specs=pl.BlockSpec((seq, LANES), rmap),
        out_shape=jax.ShapeDtypeStruct((nb * seq, FOX_WIDTH), BF16),
        compiler_params=_cparams(("parallel", "parallel")),
        name="fox_sample",
    )(q, kb, vb, cache_k, cache_v, fcum)


def _hgrn_kernel(*refs, tb, has_state):
    if has_state:
        q_ref, g_ref, i_ref, og_ref, gn_ref, tri_ref, s0_ref, y_ref, sfin_ref, st_sc, b_sc, o_sc = refs
    else:
        q_ref, g_ref, i_ref, og_ref, gn_ref, tri_ref, y_ref, sfin_ref, st_sc, b_sc, o_sc = refs
        s0_ref = None
    step = pl.program_id(1)
    nstep = pl.num_programs(1)

    @pl.when(step == 0)
    def _init():
        for h in range(HGRN_HEADS):
            if has_state:
                st_sc[h] = s0_ref[0, h].T
            else:
                st_sc[h] = jnp.zeros((HGRN_DK, HGRN_DK), F32)

    g = g_ref[...]
    tri = tri_ref[...]
    b = None
    for part in _split3(g):
        pb = jnp.dot(tri, part, preferred_element_type=F32)
        b = pb if b is None else b + pb
    b_sc[...] = b

    trow = lax.broadcasted_iota(I32, (SUB, 1), 0)

    def sub_chunk(c, carry):
        r0 = pl.multiple_of(c * SUB, SUB)
        for h in range(HGRN_HEADS):
            cs = slice(h * HGRN_DK, (h + 1) * HGRN_DK)
            q = q_ref[pl.ds(r0, SUB), cs]
            gg = g_ref[pl.ds(r0, SUB), cs]
            iv = i_ref[pl.ds(r0, SUB), cs]
            bb = b_sc[pl.ds(r0, SUB), cs]
            kk = 1.0 - jnp.exp(gg)
            st = st_sc[h]
            o = lax.dot_general((q * jnp.exp(bb)).astype(BF16), st.astype(BF16), NT_DIMS,
                                preferred_element_type=F32)
            for s in range(SUB):
                lo = 0 if s < 8 else 8
                ks = kk[s:s + 1, :]
                bs = bb[s:s + 1, :]
                ivs = iv[s:s + 1, :]
                e = jnp.exp(jnp.where(trow[lo:] >= s, bb[lo:] - bs, NEG_BIG))
                a = jnp.sum(q[lo:] * ks * e, axis=-1, keepdims=True)
                upd = a * ivs
                if lo:
                    upd = jnp.concatenate([jnp.zeros((8, HGRN_DK), F32), upd], axis=0)
                o = o + upd
            o_sc[pl.ds(r0, SUB), cs] = o
            bl = bb[SUB - 1:SUB, :]
            kd = kk * jnp.exp(bl - bb)
            u = lax.dot_general(iv.astype(BF16), kd.astype(BF16), TN_DIMS, preferred_element_type=F32)
            st_sc[h] = st * jnp.exp(bl) + u
        return carry

    lax.fori_loop(0, tb // SUB, sub_chunk, 0)

    for h in range(HGRN_HEADS):
        cs = slice(h * HGRN_DK, (h + 1) * HGRN_DK)
        o = o_sc[:, cs]
        ms = jnp.mean(o * o, axis=-1, keepdims=True)
        y_ref[:, cs] = (o * lax.rsqrt(ms + NORM_EPS) * gn_ref[...] * og_ref[:, cs]).astype(y_ref.dtype)

    @pl.when(step == nstep - 1)
    def _fin():
        for h in range(HGRN_HEADS):
            sfin_ref[0, h] = st_sc[h].T


def _hgrn(qs, gl, iv, og, gn, s0, nb, seq, tb):
    ns = seq // tb
    t = nb * seq
    row = pl.BlockSpec((tb, HGRN_WIDTH), lambda b, i: (b * ns + i, 0))
    r = jnp.arange(tb)
    tri = ((r[:, None] // SUB == r[None, :] // SUB) & (r[None, :] <= r[:, None])).astype(BF16)
    in_specs = [row, row, row, row, _const_spec((1, HGRN_DK)), _const_spec((tb, tb))]
    args = [qs, gl, iv, og, gn, tri]
    if s0 is not None:
        in_specs.append(pl.BlockSpec((1, HGRN_HEADS, HGRN_DK, HGRN_DK), lambda b, i: (b, 0, 0, 0)))
        args.append(s0)
    return pl.pallas_call(
        functools.partial(_hgrn_kernel, tb=tb, has_state=s0 is not None),
        grid=(nb, ns), in_specs=in_specs,
        out_specs=[row, pl.BlockSpec((1, HGRN_HEADS, HGRN_DK, HGRN_DK), lambda b, i: (b, 0, 0, 0))],
        out_shape=[jax.ShapeDtypeStruct((t, HGRN_WIDTH), BF16),
                   jax.ShapeDtypeStruct((nb, HGRN_HEADS, HGRN_DK, HGRN_DK), F32)],
        scratch_shapes=[pltpu.VMEM((HGRN_HEADS, HGRN_DK, HGRN_DK), F32),
                        pltpu.VMEM((tb, HGRN_WIDTH), F32), pltpu.VMEM((tb, HGRN_WIDTH), F32)],
        compiler_params=_cparams(("parallel", "arbitrary")), name="hgrn",
    )(*args)


def _post_kernel(x_ref, ya_ref, yb_ref, ga_ref, gb_ref, g1_ref, sh2_ref, sc2_ref, gn2_ref,
                 wpa_ref, wpb_ref, wo_ref, wrh_ref, wrl_ref, br_ref, tri_ref, cin_ref,
                 x1_ref, h2_ref, idx_ref, wt_ref, rank_ref, cout_ref, cnt_sc, *, tm):
    first = jnp.logical_and(pl.program_id(0) == 0, pl.program_id(1) == 0)

    @pl.when(first)
    def _init():
        cnt_sc[...] = cin_ref[...]

    merged = (ga_ref[...].astype(F32) * jnp.dot(ya_ref[...], wpa_ref[...], preferred_element_type=F32)
              + gb_ref[...].astype(F32) * jnp.dot(yb_ref[...], wpb_ref[...], preferred_element_type=F32))
    x1 = x_ref[...] + g1_ref[0] * jnp.dot(merged.astype(BF16), wo_ref[...], preferred_element_type=F32)
    x1_ref[...] = x1
    ms = jnp.mean(x1 * x1, axis=-1, keepdims=True)
    h2 = x1 * lax.rsqrt(ms + NORM_EPS) * gn2_ref[...]
    h2 = h2 * (1.0 + sc2_ref[0]) + sh2_ref[0]
    h2_ref[...] = h2

    hh = h2.astype(BF16)
    hl = (h2 - hh.astype(F32)).astype(BF16)
    wrh = wrh_ref[...]
    logits = (lax.dot_general(wrh, hh, NT_DIMS, preferred_element_type=F32)
              + lax.dot_general(wrh, hl, NT_DIMS, preferred_element_type=F32)
              + lax.dot_general(wrl_ref[...], hh, NT_DIMS, preferred_element_type=F32))
    scores = _sigmoid(logits)
    biased = scores + br_ref[...]

    b3 = biased.reshape(N_GROUPS, GROUP_SIZE, tm)
    it3 = lax.broadcasted_iota(I32, (N_GROUPS, GROUP_SIZE, tm), 1).astype(F32)
    m1 = jnp.max(b3, axis=1, keepdims=True)
    i1 = jnp.min(jnp.where(b3 == m1, it3, float(GROUP_SIZE)), axis=1, keepdims=True)
    m2 = jnp.max(jnp.where(it3 == i1, -jnp.inf, b3), axis=1, keepdims=True)
    gs = (m1 + m2).reshape(N_GROUPS, tm)

    gi = lax.broadcasted_iota(I32, (N_GROUPS, tm), 0)
    beat = jnp.zeros((N_GROUPS, tm), F32)
    for g in range(N_GROUPS):
        r = gs[g:g + 1, :]
        beat = beat + jnp.where((r > gs) | ((r == gs) & (g < gi)), 1.0, 0.0)
    gpen = jnp.where(beat < TOPK_GROUPS, 0.0, -jnp.inf)
    masked = (b3 + gpen.reshape(N_GROUPS, 1, tm)).reshape(N_EXPERTS, tm)

    ei = lax.broadcasted_iota(I32, (N_EXPERTS, tm), 0).astype(F32)
    idx_rows, w_rows = [], []
    chosen = jnp.zeros((N_EXPERTS, tm), F32)
    for _ in range(TOP_K):
        m = jnp.max(masked, axis=0, keepdims=True)
        ik = jnp.min(jnp.where(masked == m, ei, float(N_EXPERTS)), axis=0, keepdims=True)
        hit = ei == ik
        w_rows.append(jnp.sum(jnp.where(hit, scores, 0.0), axis=0, keepdims=True))
        idx_rows.append(ik)
        masked = jnp.where(hit, -jnp.inf, masked)
        chosen = jnp.where(hit, 1.0, chosen)
    idx_ref[...] = jnp.concatenate(idx_rows, axis=0).astype(I32)
    wts = jnp.concatenate(w_rows, axis=0)
    wt_ref[...] = wts / jnp.sum(wts, axis=0, keepdims=True) * ROUTED_SCALE

    before = cnt_sc[...] + jnp.dot(chosen.astype(BF16), tri_ref[...], preferred_element_type=F32)
    rank_rows = [jnp.sum(jnp.where(ei == idx_rows[kk], before, 0.0), axis=0, keepdims=True)
                 for kk in range(TOP_K)]
    rank_ref[...] = jnp.concatenate(rank_rows, axis=0).astype(I32)
    cnt_sc[...] = cnt_sc[...] + jnp.sum(chosen, axis=1, keepdims=True)
    cout_ref[...] = cnt_sc[...]


def _post(x2, ya, yb, ga, gb, gate1, shift2, scale2, prm, count_in, nb, tm):
    t, d = x2.shape
    ns = t // nb // tm
    row = lambda w: pl.BlockSpec((tm, w), lambda b, i: (b * ns + i, 0))
    colb = pl.BlockSpec((TOP_K, tm), lambda b, i: (0, b * ns + i))
    r = jnp.arange(tm)
    tri = (r[:, None] < r[None, :]).astype(BF16)
    in_specs = [row(d), row(FOX_WIDTH), row(HGRN_WIDTH), row(d), row(d),
                _mod_spec(gate1, tm, ns), _mod_spec(shift2, tm, ns), _mod_spec(scale2, tm, ns),
                _const_spec((1, d)), _const_spec((FOX_WIDTH, d)), _const_spec((HGRN_WIDTH, d)),
                _const_spec((d, d)), _const_spec((N_EXPERTS, d)), _const_spec((N_EXPERTS, d)),
                _const_spec((N_EXPERTS, 1)), _const_spec((tm, tm)), _const_spec((N_EXPERTS, 1))]
    out_shape = [jax.ShapeDtypeStruct((t, d), F32), jax.ShapeDtypeStruct((t, d), F32),
                 jax.ShapeDtypeStruct((TOP_K, t), I32), jax.ShapeDtypeStruct((TOP_K, t), F32),
                 jax.ShapeDtypeStruct((TOP_K, t), I32), jax.ShapeDtypeStruct((N_EXPERTS, 1), F32)]
    out_specs = [row(d), row(d), colb, colb, colb, pl.BlockSpec((N_EXPERTS, 1), lambda b, i: (0, 0))]
    return pl.pallas_call(
        functools.partial(_post_kernel, tm=tm), grid=(nb, ns), in_specs=in_specs,
        out_specs=out_specs, out_shape=out_shape,
        scratch_shapes=[pltpu.VMEM((N_EXPERTS, 1), F32)],
        compiler_params=_cparams(("arbitrary", "arbitrary")), name="post",
    )(x2, ya, yb, ga, gb, gate1, shift2, scale2, prm["g2"], prm["wpa"], prm["wpb"], prm["wo"],
      prm["wrh"], prm["wrl"], prm["br"], tri, count_in)


def _padfill_kernel(cnt_ref, pad_ref, pst_ref, xs_ref, zero_sc, sem):
    zero_sc[...] = jnp.zeros(zero_sc.shape, F32)

    def copy(dst_row):
        return pltpu.make_async_copy(zero_sc.at[pl.ds(0, 1)], xs_ref.at[pl.ds(dst_row, 1)], sem)

    def per_expert(e, total):
        base = pst_ref[e]

        def one(r, c):
            copy(base + r).start()
            return c

        lax.fori_loop(cnt_ref[e], pad_ref[e], one, 0)
        return total + pad_ref[e] - cnt_ref[e]

    total = lax.fori_loop(0, N_EXPERTS, per_expert, 0)

    def drain(_, c):
        copy(0).wait()
        return c

    lax.fori_loop(0, total, drain, 0)


def _padfill(counts, padded, pstart, n_rows, d):
    return pl.pallas_call(
        _padfill_kernel,
        grid_spec=pltpu.PrefetchScalarGridSpec(
            num_scalar_prefetch=3, grid=(1,), in_specs=[],
            out_specs=pl.BlockSpec(memory_space=pl.ANY),
            scratch_shapes=[pltpu.VMEM((8, d), F32), pltpu.SemaphoreType.DMA(())]),
        out_shape=jax.ShapeDtypeStruct((n_rows, d), F32),
        compiler_params=pltpu.CompilerParams(dimension_semantics=("arbitrary",), has_side_effects=True),
        name="moe_padfill",
    )(counts, padded, pstart)


def _dispatch_kernel(pos_ref, h2_ref, xs_in_ref, xs_ref, sem, *, tm):
    del xs_in_ref

    def one(t, c):
        for kk in range(TOP_K):
            pltpu.make_async_copy(h2_ref.at[pl.ds(t, 1)], xs_ref.at[pl.ds(pos_ref[kk, t], 1)], sem).start()
        return c

    lax.fori_loop(0, tm, one, 0)
    for kk in range(TOP_K):
        pltpu.make_async_copy(h2_ref, xs_ref.at[pl.ds(0, tm)], sem).wait()


def _dispatch(pos, h2, xs, tm):
    t, d = h2.shape
    return pl.pallas_call(
        functools.partial(_dispatch_kernel, tm=tm),
        grid=(t // tm,),
        in_specs=[pl.BlockSpec((TOP_K, tm), lambda i: (0, i), memory_space=pltpu.SMEM),
                  pl.BlockSpec((tm, d), lambda i: (i, 0)),
                  pl.BlockSpec(memory_space=pl.ANY)],
        out_specs=pl.BlockSpec(memory_space=pl.ANY),
        out_shape=jax.ShapeDtypeStruct(xs.shape, F32),
        scratch_shapes=[pltpu.SemaphoreType.DMA(())],
        input_output_aliases={2: 0},
        compiler_params=pltpu.CompilerParams(dimension_semantics=("arbitrary",), has_side_effects=True,
                                             vmem_limit_bytes=VMEM_LIMIT),
        name="moe_dispatch",
    )(pos, h2, xs)


def _moe_kernel(be_ref, nu_ref, x_ref, wg_ref, wu_ref, wd_ref, y_ref, wgu_sc, wd_sc):
    i = pl.program_id(0)
    e = be_ref[i]
    e_prev = be_ref[jnp.maximum(i - 1, 0)]

    @pl.when(jnp.logical_or(i == 0, e != e_prev))
    def _load():
        wgu_sc[:, 0:D_EXPERT] = wg_ref[0].astype(BF16)
        wgu_sc[:, D_EXPERT:2 * D_EXPERT] = wu_ref[0].astype(BF16)
        wd_sc[...] = wd_ref[0].astype(BF16)

    @pl.when(i < nu_ref[0])
    def _compute():
        gu = jnp.dot(x_ref[...].astype(BF16), wgu_sc[...], preferred_element_type=F32)
        gt = gu[:, 0:D_EXPERT]
        h = gt * _sigmoid(gt) * gu[:, D_EXPERT:2 * D_EXPERT]
        y_ref[...] = jnp.dot(h.astype(BF16), wd_sc[...], preferred_element_type=F32)


def _moe(blk_expert, n_used, xs, w_gate, w_up, w_down):
    n_rows, d = xs.shape
    nblk = n_rows // MOE_BLK
    xmap = lambda i, be, nu: (jnp.minimum(i, nu[0] - 1), 0)
    wmap = lambda i, be, nu: (be[i], 0, 0)
    return pl.pallas_call(
        _moe_kernel,
        grid_spec=pltpu.PrefetchScalarGridSpec(
            num_scalar_prefetch=2, grid=(nblk,),
            in_specs=[pl.BlockSpec((MOE_BLK, d), xmap),
                      pl.BlockSpec((1, d, D_EXPERT), wmap), pl.BlockSpec((1, d, D_EXPERT), wmap),
                      pl.BlockSpec((1, D_EXPERT, d), wmap)],
            out_specs=pl.BlockSpec((MOE_BLK, d), xmap),
            scratch_shapes=[pltpu.VMEM((d, 2 * D_EXPERT), BF16), pltpu.VMEM((D_EXPERT, d), BF16)]),
        out_shape=jax.ShapeDtypeStruct((n_rows, d), F32),
        compiler_params=_cparams(("arbitrary",)), name="moe_experts",
    )(blk_expert, n_used, xs, w_gate, w_up, w_down)


def _combine_kernel(pos_ref, x1_ref, h2_ref, g2_ref, wt_ref, wsgu_ref, wsd_ref, ys_ref, o_ref, buf, sem, *, tm):
    def copy(kk, t, row):
        return pltpu.make_async_copy(ys_ref.at[pl.ds(row, 1)], buf.at[kk, pl.ds(t, 1)], sem)

    def one(t, c):
        for kk in range(TOP_K):
            copy(kk, t, pos_ref[kk, t]).start()
        return c

    lax.fori_loop(0, tm, one, 0)

    gu = jnp.dot(h2_ref[...].astype(BF16), wsgu_ref[...], preferred_element_type=F32)
    gt = gu[:, 0:D_SHARED]
    hs = gt * _sigmoid(gt) * gu[:, D_SHARED:2 * D_SHARED]
    acc = jnp.dot(hs.astype(BF16), wsd_ref[...], preferred_element_type=F32)

    for kk in range(TOP_K):
        pltpu.make_async_copy(ys_ref.at[pl.ds(0, tm)], buf.at[kk], sem).wait()

    d = acc.shape[-1]
    for kk in range(TOP_K):
        wrow = jnp.broadcast_to(wt_ref[kk:kk + 1, :], (LANES, tm))
        wcol = wrow.T
        acc = acc + buf[kk] * jnp.concatenate([wcol] * (d // LANES), axis=1)
    o_ref[...] = x1_ref[...] + g2_ref[0] * acc


def _combine(pos, x1, h2, gate2, wts, wsgu, wsd, ys, nb, tm):
    t, d = x1.shape
    ns = t // nb // tm
    row = pl.BlockSpec((tm, d), lambda b, i: (b * ns + i, 0))
    return pl.pallas_call(
        functools.partial(_combine_kernel, tm=tm),
        grid=(nb, ns),
        in_specs=[pl.BlockSpec((TOP_K, tm), lambda b, i: (0, b * ns + i), memory_space=pltpu.SMEM),
                  row, row, _mod_spec(gate2, tm, ns),
                  pl.BlockSpec((TOP_K, tm), lambda b, i: (0, b * ns + i)),
                  _const_spec(wsgu.shape), _const_spec(wsd.shape),
                  pl.BlockSpec(memory_space=pl.ANY)],
        out_specs=row,
        out_shape=jax.ShapeDtypeStruct((t, d), F32),
        scratch_shapes=[pltpu.VMEM((TOP_K, tm, d), F32), pltpu.SemaphoreType.DMA(())],
        compiler_params=_cparams(("arbitrary", "arbitrary")), name="moe_combine",
    )(pos, x1, h2, gate2, wts, wsgu, wsd, ys)


def _prepare_params(g_norm1, w_in, b_fox_f, g_q, g_k, hgrn_lb, g_hgrn_o, w_proj_a, w_proj_b, w_out,
                    g_norm2, w_router, b_router, w_sh_gate, w_sh_up, w_sh_down):
    d = D_MODEL
    fw, hw = FOX_WIDTH, HGRN_WIDTH
    c0 = 3 * fw
    c1 = c0 + FOX_HEADS
    c2 = c1 + 4 * hw
    wff = w_in[:, c0:c1]
    head = jnp.arange(fw) // FOX_HEAD_DIM
    wr_t = w_router.T
    wrh = wr_t.astype(BF16)
    return dict(
        g1=g_norm1.reshape(1, d),
        wqkv=w_in[:, :c0].astype(BF16),
        wff=jnp.pad(wff, ((0, 0), (0, LANES - FOX_HEADS))).astype(BF16),
        wfft=jnp.pad(wff.T, ((0, 16 - FOX_HEADS), (0, 0))).astype(BF16),
        bf=b_fox_f.reshape(1, FOX_HEADS), bft=b_fox_f.reshape(FOX_HEADS, 1),
        gq=jnp.tile(g_q, FOX_HEADS).reshape(1, fw), gk=jnp.tile(g_k, FOX_HEADS).reshape(1, fw),
        bd=(head[:, None] == head[None, :]).astype(BF16) * (1.0 / FOX_HEAD_DIM),
        lbp=hgrn_lb,
        wh=w_in[:, c1:c2].astype(BF16), wg=w_in[:, c2:].astype(BF16),
        gn=g_hgrn_o.reshape(1, HGRN_DK),
        g2=g_norm2.reshape(1, d),
        wpa=w_proj_a.astype(BF16), wpb=w_proj_b.astype(BF16), wo=w_out.astype(BF16),
        wrh=wrh, wrl=(wr_t - wrh.astype(F32)).astype(BF16), br=b_router.reshape(N_EXPERTS, 1),
        wsgu=jnp.concatenate([w_sh_gate, w_sh_up], axis=1).astype(BF16), wsd=w_sh_down.astype(BF16),
    )


def _mixers(x2, nb, seq, tm, shift1, scale1, prm, past, tq, tk, tb):
    (q, k, kb, v, vb, lf, lft, hq, hl, hi, og, ga, gb) = _inproj(x2, nb if shift1.shape[1] == 1 else 1, tm,
                                                                 shift1, scale1, prm)
    t = nb * seq
    hp = FOX_HEADS // 2
    lft = lft.transpose(1, 0, 2).reshape(FOX_HEADS, nb, seq).transpose(1, 0, 2)
    if past is None:
        fcum = _cumsum_lanes(lft.reshape(nb * FOX_HEADS, seq)).reshape(nb, hp, 2, seq)
        ya = _fox_prompt(q, kb, vb, fcum, nb, seq, tq, tk)
        s0 = None
    else:
        cache_k, cache_v, cache_lf, s0 = past
        plen = cache_k.shape[1]
        ltot = plen + seq
        lpad = -(-ltot // LANES) * LANES
        lf_all = jnp.concatenate([cache_lf.transpose(0, 2, 1), lft,
                                  jnp.zeros((nb, FOX_HEADS, lpad - ltot), F32)], axis=-1)
        fcum = _cumsum_lanes(lf_all.reshape(nb * FOX_HEADS, lpad)).reshape(nb, hp, 2, lpad)
        ya = _fox_sample(q, kb, vb, cache_k.reshape(nb, plen, FOX_WIDTH), cache_v.reshape(nb, plen, FOX_WIDTH),
                         fcum, nb, seq, plen)
    yb, sfin = _hgrn(hq, hl, hi, og, prm["gn"], s0, nb, seq, tb)
    return k, v, lf, sfin, ya, yb, ga, gb


def kernel(x_prompt, x_sample, cache_fox_k, cache_fox_v, cache_fox_logf, state_hgrn, c_prompt, c_sample,
           w_ada, b_ada, g_norm1, w_in, b_fox_f, g_q, g_k, hgrn_lb, g_hgrn_o, w_proj_a, w_proj_b, w_out,
           g_norm2, w_router, b_router, w_exp_gate, w_exp_up, w_exp_down, w_sh_gate, w_sh_up, w_sh_down):
    assert w_ada.shape[0] == 1 and hgrn_lb.shape[0] == 2, "single-layer trunk"
    d = D_MODEL
    bp, sp, _ = x_prompt.shape
    bs, ss, _ = x_sample.shape
    tp, ts = bp * sp, bs * ss
    prm = _prepare_params(g_norm1[0], w_in[0], b_fox_f[0], g_q[0], g_k[0], hgrn_lb, g_hgrn_o[0],
                          w_proj_a[0], w_proj_b[0], w_out[0], g_norm2[0], w_router[0], b_router[0],
                          w_sh_gate[0], w_sh_up[0], w_sh_down[0])

    bc = bp + bs
    bc_pad = -(-bc // 8) * 8
    c_all = jnp.concatenate([c_prompt, c_sample, jnp.zeros((bc_pad - bc, d), F32)], axis=0)
    mod = _ada(c_all, w_ada[0], b_ada[0])
    mod_p = [mod[:bp, j * d:(j + 1) * d].reshape(bp, 1, d) for j in range(6)]
    mod_s = [jnp.repeat(mod[bp:bc, j * d:(j + 1) * d], ss, axis=0).reshape(1, ts, d) for j in range(6)]

    tm_p = min(512, sp)
    tm_s = min(512, ts)
    xp2 = x_prompt.reshape(tp, d)
    xs2 = x_sample.reshape(ts, d)
    tq = min(1024, sp)
    tk = min(512, sp)
    tb = min(256, sp)

    kp, vp, lfp, sfin_p, ya_p, yb_p, ga_p, gb_p = _mixers(
        xp2, bp, sp, tm_p, mod_p[0], mod_p[1], prm, None, tq, tk, tb)
    past = (cache_fox_k[0], cache_fox_v[0], cache_fox_logf[0], state_hgrn[0])
    ks, vs, lfs, sfin_s, ya_s, yb_s, ga_s, gb_s = _mixers(
        xs2, bs, ss, tm_s, mod_s[0], mod_s[1], prm, past, None, None, min(256, ss))

    zero_cnt = jnp.zeros((N_EXPERTS, 1), F32)
    x1_p, h2_p, idx_p, wt_p, rank_p, cnt_p = _post(xp2, ya_p, yb_p, ga_p, gb_p, mod_p[2], mod_p[3], mod_p[4],
                                                    prm, zero_cnt, bp, tm_p)
    x1_s, h2_s, idx_s, wt_s, rank_s, cnt_all = _post(xs2, ya_s, yb_s, ga_s, gb_s, mod_s[2], mod_s[3], mod_s[4],
                                                     prm, cnt_p, 1, tm_s)

    counts = cnt_all.reshape(N_EXPERTS).astype(I32)
    padded = (counts + MOE_BLK - 1) // MOE_BLK * MOE_BLK
    pend = jnp.cumsum(padded)
    pstart = pend - padded
    nblk = -(-((tp + ts) * TOP_K + N_EXPERTS * (MOE_BLK - 1)) // MOE_BLK)
    n_used = (pend[-1] // MOE_BLK).reshape(1)
    blk_expert = jnp.minimum(jnp.searchsorted(pend, jnp.arange(nblk, dtype=I32) * MOE_BLK, side="right"),
                             N_EXPERTS - 1).astype(I32)
    blk_expert = jnp.where(jnp.arange(nblk) < n_used[0], blk_expert, blk_expert[jnp.maximum(n_used[0] - 1, 0)])
    pos_p = pstart[idx_p] + rank_p
    pos_s = pstart[idx_s] + rank_s

    xs_pool = _padfill(counts, padded, pstart, nblk * MOE_BLK, d)
    tm_d = min(256, tm_p)
    xs_pool = _dispatch(pos_p, h2_p, xs_pool, tm_d)
    xs_pool = _dispatch(pos_s, h2_s, xs_pool, min(256, tm_s))
    ys_pool = _moe(blk_expert, n_used, xs_pool, w_exp_gate[0], w_exp_up[0], w_exp_down[0])

    tm_c = min(128, sp)
    y_p = _combine(pos_p, x1_p, h2_p, mod_p[5], wt_p, prm["wsgu"], prm["wsd"], ys_pool, bp, tm_c)
    y_s = _combine(pos_s, x1_s, h2_s, mod_s[5], wt_s, prm["wsgu"], prm["wsd"], ys_pool, 1, min(128, ts))

    return (y_p.reshape(bp, sp, d), y_s.reshape(bs, ss, d),
            kp.reshape(1, bp, sp, FOX_HEADS, FOX_HEAD_DIM), vp.reshape(1, bp, sp, FOX_HEADS, FOX_HEAD_DIM),
            lfp.reshape(1, bp, sp, FOX_HEADS), sfin_p[None],
            ks.reshape(1, bs, ss, FOX_HEADS, FOX_HEAD_DIM), vs.reshape(1, bs, ss, FOX_HEADS, FOX_HEAD_DIM),
            lfs.reshape(1, bs, ss, FOX_HEADS), sfin_s[None])
```

```python
import functools

import jax
import jax.numpy as jnp
from jax import lax
from jax.experimental import pallas as pl
from jax.experimental.pallas import tpu as pltpu

F32 = jnp.float32
BF16 = jnp.bfloat16
I32 = jnp.int32

D_MODEL = 1024
FOX_HEADS = 8
FOX_HEAD_DIM = 64
FOX_WIDTH = FOX_HEADS * FOX_HEAD_DIM
HGRN_HEADS = 4
HGRN_DK = 128
HGRN_WIDTH = HGRN_HEADS * HGRN_DK
N_EXPERTS = 256
TOP_K = 8
N_GROUPS = 8
TOPK_GROUPS = 4
GROUP_SIZE = N_EXPERTS // N_GROUPS
D_EXPERT = 256
D_SHARED = 256
ROUTED_SCALE = 2.5
NORM_EPS = 1e-6
NEG_BIG = -1e30
LOG2E = 1.4426950408889634
QK_SCALE = FOX_HEAD_DIM ** -0.5 * LOG2E

LANES = 128
SUB = 16
MOE_BLK = 256
VMEM_LIMIT = 56 * 1024 * 1024

NT_DIMS = (((1,), (1,)), ((), ()))
TN_DIMS = (((0,), (0,)), ((), ()))


def _cparams(sem):
    return pltpu.CompilerParams(dimension_semantics=sem, vmem_limit_bytes=VMEM_LIMIT)


def _const_spec(shape):
    nd = len(shape)
    return pl.BlockSpec(shape, lambda *_: (0,) * nd, pipeline_mode=pl.Buffered(1))


def _sigmoid(z):
    return 1.0 / (1.0 + jnp.exp(-z))


def _log_sigmoid(z):
    return jnp.minimum(z, 0.0) - jnp.log(1.0 + jnp.exp(-jnp.abs(z)))


def _split3(a):
    hi = a.astype(BF16)
    r1 = a - hi.astype(F32)
    mid = r1.astype(BF16)
    lo = (r1 - mid.astype(F32)).astype(BF16)
    return hi, mid, lo


def _ada_kernel(c_ref, w_ref, b_ref, o_ref):
    c = c_ref[...]
    s = c * _sigmoid(c)
    o_ref[...] = jnp.dot(s.astype(BF16), w_ref[...].astype(BF16), preferred_element_type=F32) + b_ref[...]


def _ada(c_all, w_ada, b_ada):
    bc, d = c_all.shape
    n = w_ada.shape[1]
    tn = 1024
    return pl.pallas_call(
        _ada_kernel,
        grid=(n // tn,),
        in_specs=[pl.BlockSpec((bc, d), lambda j: (0, 0)),
                  pl.BlockSpec((d, tn), lambda j: (0, j)),
                  pl.BlockSpec((1, tn), lambda j: (0, j))],
        out_specs=pl.BlockSpec((bc, tn), lambda j: (0, j)),
        out_shape=jax.ShapeDtypeStruct((bc, n), F32),
        compiler_params=_cparams(("parallel",)),
        name="ada",
    )(c_all, w_ada, b_ada.reshape(1, n))


def _inproj_kernel(x_ref, sh_ref, sc_ref, g1_ref, wqkv_ref, wff_ref, wfft_ref, bf_ref, bft_ref,
                   gq_ref, gk_ref, bd_ref, lbp_ref, wh_ref, wg_ref,
                   q_ref, k_ref, kb_ref, v_ref, vb_ref, lf_ref, lft_ref,
                   hq_ref, hl_ref, hi_ref, og_ref, ga_ref, gb_ref):
    x = x_ref[...]
    ms = jnp.mean(x * x, axis=-1, keepdims=True)
    h = x * lax.rsqrt(ms + NORM_EPS) * g1_ref[...]
    h = h * (1.0 + sc_ref[0]) + sh_ref[0]
    hb = h.astype(BF16)

    def headnorm(a, g):
        ss = jnp.dot((a * a).astype(BF16), bd_ref[...], preferred_element_type=F32)
        return a * lax.rsqrt(ss + NORM_EPS) * g

    fq = jnp.dot(hb, wqkv_ref[:, 0:FOX_WIDTH], preferred_element_type=F32)
    q_ref[...] = (headnorm(fq, gq_ref[...]) * QK_SCALE).astype(BF16)
    fk = jnp.dot(hb, wqkv_ref[:, FOX_WIDTH:2 * FOX_WIDTH], preferred_element_type=F32)
    k = headnorm(fk, gk_ref[...])
    k_ref[...] = k
    kb_ref[...] = k.astype(BF16)
    fv = jnp.dot(hb, wqkv_ref[:, 2 * FOX_WIDTH:3 * FOX_WIDTH], preferred_element_type=F32)
    v_ref[...] = fv
    vb_ref[...] = fv.astype(BF16)

    ff = jnp.dot(hb, wff_ref[...], preferred_element_type=F32)
    lf_ref[...] = _log_sigmoid(ff[:, 0:FOX_HEADS] + bf_ref[...])
    fft = lax.dot_general(wfft_ref[...], hb, NT_DIMS, preferred_element_type=F32)
    lft_ref[0] = _log_sigmoid(fft[0:FOX_HEADS, :] + bft_ref[...])

    lbp = lbp_ref[...]
    e = jnp.exp(lbp - jnp.max(lbp, axis=0, keepdims=True))
    lb = e[0:1, :] / jnp.sum(e, axis=0, keepdims=True)

    w = HGRN_WIDTH
    hq = jnp.dot(hb, wh_ref[:, 0:w], preferred_element_type=F32)
    hq_ref[...] = hq * _sigmoid(hq)
    hf = jnp.dot(hb, wh_ref[:, w:2 * w], preferred_element_type=F32)
    hl_ref[...] = jnp.log(lb + (1.0 - lb) * _sigmoid(hf))
    hi_ref[...] = jnp.dot(hb, wh_ref[:, 2 * w:3 * w], preferred_element_type=F32)
    hg = jnp.dot(hb, wh_ref[:, 3 * w:4 * w], preferred_element_type=F32)
    og_ref[...] = _sigmoid(hg)
    ga = jnp.dot(hb, wg_ref[:, 0:D_MODEL], preferred_element_type=F32)
    ga_ref[...] = _sigmoid(ga).astype(BF16)
    gb = jnp.dot(hb, wg_ref[:, D_MODEL:2 * D_MODEL], preferred_element_type=F32)
    gb_ref[...] = _sigmoid(gb).astype(BF16)


def _mod_spec(arr, tm, ns):
    d = arr.shape[-1]
    if arr.shape[1] == 1:
        return pl.BlockSpec((1, 1, d), lambda b, i: (b, 0, 0))
    return pl.BlockSpec((1, tm, d), lambda b, i: (0, b * ns + i, 0))


def _inproj(x2, nb, tm, shift1, scale1, prm):
    t, d = x2.shape
    ns = t // nb // tm
    row = lambda w: pl.BlockSpec((tm, w), lambda b, i: (b * ns + i, 0))
    in_specs = [row(d), _mod_spec(shift1, tm, ns), _mod_spec(scale1, tm, ns),
                _const_spec((1, d)), _const_spec(prm["wqkv"].shape), _const_spec(prm["wff"].shape),
                _const_spec(prm["wfft"].shape), _const_spec((1, FOX_HEADS)), _const_spec((FOX_HEADS, 1)),
                _const_spec((1, FOX_WIDTH)), _const_spec((1, FOX_WIDTH)), _const_spec((FOX_WIDTH, FOX_WIDTH)),
                _const_spec(prm["lbp"].shape), _const_spec(prm["wh"].shape), _const_spec(prm["wg"].shape)]
    fw, hw = FOX_WIDTH, HGRN_WIDTH
    out_shape = [jax.ShapeDtypeStruct((t, fw), BF16),
                 jax.ShapeDtypeStruct((t, fw), F32),
                 jax.ShapeDtypeStruct((t, fw), BF16),
                 jax.ShapeDtypeStruct((t, fw), F32),
                 jax.ShapeDtypeStruct((t, fw), BF16),
                 jax.ShapeDtypeStruct((t, FOX_HEADS), F32),
                 jax.ShapeDtypeStruct((nb * ns, FOX_HEADS, tm), F32),
                 jax.ShapeDtypeStruct((t, hw), F32),
                 jax.ShapeDtypeStruct((t, hw), F32),
                 jax.ShapeDtypeStruct((t, hw), F32),
                 jax.ShapeDtypeStruct((t, hw), F32),
                 jax.ShapeDtypeStruct((t, d), BF16),
                 jax.ShapeDtypeStruct((t, d), BF16)]
    out_specs = [row(fw), row(fw), row(fw), row(fw), row(fw), row(FOX_HEADS),
                 pl.BlockSpec((1, FOX_HEADS, tm), lambda b, i: (b * ns + i, 0, 0)),
                 row(hw), row(hw), row(hw), row(hw), row(d), row(d)]
    return pl.pallas_call(
        _inproj_kernel, grid=(nb, ns), in_specs=in_specs, out_specs=out_specs, out_shape=out_shape,
        compiler_params=_cparams(("parallel", "parallel")), name="inproj",
    )(x2, shift1, scale1, prm["g1"], prm["wqkv"], prm["wff"], prm["wfft"], prm["bf"], prm["bft"],
      prm["gq"], prm["gk"], prm["bd"], prm["lbp"], prm["wh"], prm["wg"])


def _cumsum_kernel(x_ref, o_ref):
    x = x_ref[...]
    n = x.shape[-1]
    lane = lax.broadcasted_iota(I32, x.shape, 1)
    s = 1
    while s < n:
        x = x + jnp.where(lane >= s, pltpu.roll(x, s, axis=1), 0.0)
        s *= 2
    o_ref[...] = x * LOG2E


def _cumsum_lanes(x):
    return pl.pallas_call(
        _cumsum_kernel, out_shape=jax.ShapeDtypeStruct(x.shape, F32),
        compiler_params=pltpu.CompilerParams(vmem_limit_bytes=VMEM_LIMIT), name="cumsum",
    )(x)


def _fox_prompt_kernel(qt_ref, kt_ref, q_ref, k_ref, v_ref, fk_ref, o_ref,
                       m0, m1, l0, l1, a0, a1, s0, s1, p0, p1, *, tile, rsub):
    step = pl.program_id(2)
    qi = qt_ref[step]
    ki = kt_ref[step]
    heads = ((m0, l0, a0, s0, p0), (m1, l1, a1, s1, p1))
    lane = lax.broadcasted_iota(I32, (1, LANES), 1)
    lo_half = lane < FOX_HEAD_DIM

    @pl.when(ki == 0)
    def _init():
        for m_sc, l_sc, a_sc, _, _ in heads:
            m_sc[...] = jnp.full(m_sc.shape, NEG_BIG, F32)
            l_sc[...] = jnp.zeros(l_sc.shape, F32)
            a_sc[...] = jnp.zeros(a_sc.shape, F32)

    def tile_update(diagonal):
        q = q_ref[...]
        k = k_ref[...]
        v = v_ref[...]
        fk = fk_ref[0, 0]
        nsub = tile // rsub
        if diagonal:
            ahead = (lax.broadcasted_iota(I32, (rsub, tile), 1)
                     - lax.broadcasted_iota(I32, (rsub, tile), 0))
        for hs, (m_sc, l_sc, a_sc, s_sc, p_sc) in enumerate(heads):
            sel = lo_half if hs == 0 else jnp.logical_not(lo_half)
            qh = jnp.where(sel, q, jnp.zeros_like(q))
            s_sc[...] = lax.dot_general(qh, k, NT_DIMS, preferred_element_type=F32)

            def biased(r):
                rows = slice(r * rsub, (r + 1) * rsub)
                cw = min(tile, -(-((r + 1) * rsub) // LANES) * LANES) if diagonal else tile
                s = s_sc[rows, 0:cw] - fk[hs:hs + 1, 0:cw]
                if diagonal:
                    s = jnp.where(ahead[:, 0:cw] <= r * rsub, s, NEG_BIG)
                return s, rows, cw

            m_old = m_sc[...]
            m_parts = []
            for r in range(nsub):
                s, rows, _ = biased(r)
                m_parts.append(jnp.maximum(m_old[rows], jnp.max(s, axis=1, keepdims=True)))
            l_parts = []
            for r in range(nsub):
                s, rows, cw = biased(r)
                nc = cw // LANES
                p = jnp.exp2(s - jnp.concatenate([m_parts[r]] * nc, axis=1))
                part = p[:, 0:LANES]
                for c in range(1, nc):
                    part = part + p[:, c * LANES:(c + 1) * LANES]
                l_parts.append(part)
                p_sc[rows, 0:cw] = p.astype(BF16)
                if cw < tile:
                    p_sc[rows, cw:tile] = jnp.zeros((rsub, tile - cw), BF16)
            m_new = jnp.concatenate(m_parts, axis=0)
            alpha = jnp.exp2(m_old - m_new)
            m_sc[...] = m_new
            l_sc[...] = alpha * l_sc[...] + jnp.concatenate(l_parts, axis=0)
            a_sc[...] = alpha * a_sc[...] + jnp.dot(p_sc[...], v, preferred_element_type=F32)

    @pl.when(ki < qi)
    def _full():
        tile_update(False)

    @pl.when(ki == qi)
    def _diag():
        tile_update(True)
        o0 = a0[...] / jnp.sum(l0[...], axis=1, keepdims=True)
        o1 = a1[...] / jnp.sum(l1[...], axis=1, keepdims=True)
        o_ref[...] = jnp.where(lo_half, o0, o1).astype(o_ref.dtype)


def _fox_prompt(q, kb, vb, fcum, nb, seq, tile, rsub):
    nt = seq // tile
    hp = FOX_HEADS // 2
    qt = jnp.asarray([qi for qi in range(nt) for _ in range(qi + 1)], I32)
    kt = jnp.asarray([ki for qi in range(nt) for ki in range(qi + 1)], I32)
    qmap = lambda b, h, s, qt, kt: (b * nt + qt[s], h)
    kmap = lambda b, h, s, qt, kt: (b * nt + kt[s], h)
    fmap = lambda b, h, s, qt, kt: (b, h, 0, kt[s])
    stat = pltpu.VMEM((tile, LANES), F32)
    return pl.pallas_call(
        functools.partial(_fox_prompt_kernel, tile=tile, rsub=rsub),
        grid_spec=pltpu.PrefetchScalarGridSpec(
            num_scalar_prefetch=2, grid=(nb, hp, nt * (nt + 1) // 2),
            in_specs=[pl.BlockSpec((tile, LANES), qmap), pl.BlockSpec((tile, LANES), kmap),
                      pl.BlockSpec((tile, LANES), kmap), pl.BlockSpec((1, 1, 2, tile), fmap)],
            out_specs=pl.BlockSpec((tile, LANES), qmap),
            scratch_shapes=[stat, stat, stat, stat, stat, stat,
                            pltpu.VMEM((tile, tile), F32), pltpu.VMEM((tile, tile), F32),
                            pltpu.VMEM((tile, tile), BF16), pltpu.VMEM((tile, tile), BF16)]),
        out_shape=jax.ShapeDtypeStruct((nb * seq, FOX_WIDTH), BF16),
        compiler_params=_cparams(("parallel", "parallel", "arbitrary")),
        name="fox_prompt",
    )(qt, kt, q, kb, vb, fcum)


def _fox_sample_kernel(q_ref, kn_ref, vn_ref, kp_ref, vp_ref, fk_ref, o_ref, *, past, seq):
    lane = lax.broadcasted_iota(I32, (1, LANES), 1)
    lo_half = lane < FOX_HEAD_DIM
    q = q_ref[...]
    kn = kn_ref[...]
    vn = vn_ref[...]
    kp = kp_ref[0].astype(BF16)
    vp = vp_ref[0].astype(BF16)
    fk = fk_ref[0, 0]
    row = lax.broadcasted_iota(I32, (seq, seq), 0)
    col = lax.broadcasted_iota(I32, (seq, seq), 1)
    causal = col <= row
    outs = []
    for hs in range(2):
        sel = lo_half if hs == 0 else jnp.logical_not(lo_half)
        qh = jnp.where(sel, q, jnp.zeros_like(q))
        sp = lax.dot_general(qh, kp, NT_DIMS, preferred_element_type=F32) - fk[hs:hs + 1, 0:past]
        sn = lax.dot_general(qh, kn, NT_DIMS, preferred_element_type=F32) - fk[hs:hs + 1, past:past + seq]
        sn = jnp.where(causal, sn, NEG_BIG)
        m = jnp.maximum(jnp.max(sp, axis=1, keepdims=True), jnp.max(sn, axis=1, keepdims=True))
        pp = jnp.exp2(sp - m)
        pn = jnp.exp2(sn - m)
        l = jnp.sum(pp, axis=1, keepdims=True) + jnp.sum(pn, axis=1, keepdims=True)
        o = (jnp.dot(pp.astype(BF16), vp, preferred_element_type=F32)
             + jnp.dot(pn.astype(BF16), vn, preferred_element_type=F32))
        outs.append(o / l)
    o_ref[...] = jnp.where(lo_half, outs[0], outs[1]).astype(o_ref.dtype)


def _fox_sample(q, kb, vb, cache_k, cache_v, fcum, nb, seq, past):
    hp = FOX_HEADS // 2
    lpad = fcum.shape[-1]
    rmap = lambda b, h: (b, h)
    cmap = lambda b, h: (b, 0, h)
    return pl.pallas_call(
        functools.partial(_fox_sample_kernel, past=past, seq=seq),
        grid=(nb, hp),
        in_specs=[pl.BlockSpec((seq, LANES), rmap), pl.BlockSpec((seq, LANES), rmap),
                  pl.BlockSpec((seq, LANES), rmap),
                  pl.BlockSpec((1, past, LANES), cmap), pl.BlockSpec((1, past, LANES), cmap),
                  pl.BlockSpec((1, 1, 2, lpad), lambda b, h: (b, h, 0, 0))],
        out_specs=pl.BlockSpec((seq, LANES), rmap),
        out_shape=jax.ShapeDtypeStruct((nb * seq, FOX_WIDTH), BF16),
        compiler_params=_cparams(("parallel", "parallel")),
        name="fox_sample",
    )(q, kb, vb, cache_k, cache_v, fcum)


def _hgrn_kernel(*refs, tb, has_state):
    if has_state:
        q_ref, g_ref, i_ref, og_ref, gn_ref, tri_ref, s0_ref, y_ref, sfin_ref, st_sc, b_sc, o_sc = refs
    else:
        q_ref, g_ref, i_ref, og_ref, gn_ref, tri_ref, y_ref, sfin_ref, st_sc, b_sc, o_sc = refs
        s0_ref = None
    step = pl.program_id(1)
    nstep = pl.num_programs(1)

    @pl.when(step == 0)
    def _init():
        for h in range(HGRN_HEADS):
            if has_state:
                st_sc[h] = s0_ref[0, h].T
            else:
                st_sc[h] = jnp.zeros((HGRN_DK, HGRN_DK), F32)

    g = g_ref[...]
    tri = tri_ref[...]
    b = None
    for part in _split3(g):
        pb = jnp.dot(tri, part, preferred_element_type=F32)
        b = pb if b is None else b + pb
    b_sc[...] = b

    trow = lax.broadcasted_iota(I32, (SUB, 1), 0)

    def sub_chunk(c, carry):
        r0 = pl.multiple_of(c * SUB, SUB)
        for h in range(HGRN_HEADS):
            cs = slice(h * HGRN_DK, (h + 1) * HGRN_DK)
            q = q_ref[pl.ds(r0, SUB), cs]
            gg = g_ref[pl.ds(r0, SUB), cs]
            iv = i_ref[pl.ds(r0, SUB), cs]
            bb = b_sc[pl.ds(r0, SUB), cs]
            kk = 1.0 - jnp.exp(gg)
            st = st_sc[h]
            o = lax.dot_general((q * jnp.exp(bb)).astype(BF16), st.astype(BF16), NT_DIMS,
                                preferred_element_type=F32)
            for s in range(SUB):
                lo = 0 if s < 8 else 8
                ks = kk[s:s + 1, :]
                bs = bb[s:s + 1, :]
                ivs = iv[s:s + 1, :]
                e = jnp.exp(jnp.where(trow[lo:] >= s, bb[lo:] - bs, NEG_BIG))
                a = jnp.sum(q[lo:] * ks * e, axis=-1, keepdims=True)
                upd = a * ivs
                if lo:
                    upd = jnp.concatenate([jnp.zeros((8, HGRN_DK), F32), upd], axis=0)
                o = o + upd
            o_sc[pl.ds(r0, SUB), cs] = o
            bl = bb[SUB - 1:SUB, :]
            kd = kk * jnp.exp(bl - bb)
            u = lax.dot_general(iv.astype(BF16), kd.astype(BF16), TN_DIMS, preferred_element_type=F32)
            st_sc[h] = st * jnp.exp(bl) + u
        return carry

    lax.fori_loop(0, tb // SUB, sub_chunk, 0)

    for h in range(HGRN_HEADS):
        cs = slice(h * HGRN_DK, (h + 1) * HGRN_DK)
        o = o_sc[:, cs]
        ms = jnp.mean(o * o, axis=-1, keepdims=True)
        y_ref[:, cs] = (o * lax.rsqrt(ms + NORM_EPS) * gn_ref[...] * og_ref[:, cs]).astype(y_ref.dtype)

    @pl.when(step == nstep - 1)
    def _fin():
        for h in range(HGRN_HEADS):
            sfin_ref[0, h] = st_sc[h].T


def _hgrn(qs, gl, iv, og, gn, s0, nb, seq, tb):
    ns = seq // tb
    t = nb * seq
    row = pl.BlockSpec((tb, HGRN_WIDTH), lambda b, i: (b * ns + i, 0))
    r = jnp.arange(tb)
    tri = ((r[:, None] // SUB == r[None, :] // SUB) & (r[None, :] <= r[:, None])).astype(BF16)
    in_specs = [row, row, row, row, _const_spec((1, HGRN_DK)), _const_spec((tb, tb))]
    args = [qs, gl, iv, og, gn, tri]
    if s0 is not None:
        in_specs.append(pl.BlockSpec((1, HGRN_HEADS, HGRN_DK, HGRN_DK), lambda b, i: (b, 0, 0, 0)))
        args.append(s0)
    return pl.pallas_call(
        functools.partial(_hgrn_kernel, tb=tb, has_state=s0 is not None),
        grid=(nb, ns), in_specs=in_specs,
        out_specs=[row, pl.BlockSpec((1, HGRN_HEADS, HGRN_DK, HGRN_DK), lambda b, i: (b, 0, 0, 0))],
        out_shape=[jax.ShapeDtypeStruct((t, HGRN_WIDTH), BF16),
                   jax.ShapeDtypeStruct((nb, HGRN_HEADS, HGRN_DK, HGRN_DK), F32)],
        scratch_shapes=[pltpu.VMEM((HGRN_HEADS, HGRN_DK, HGRN_DK), F32),
                        pltpu.VMEM((tb, HGRN_WIDTH), F32), pltpu.VMEM((tb, HGRN_WIDTH), F32)],
        compiler_params=_cparams(("parallel", "arbitrary")), name="hgrn",
    )(*args)


def _post_kernel(x_ref, ya_ref, yb_ref, ga_ref, gb_ref, g1_ref, sh2_ref, sc2_ref, gn2_ref,
                 wpa_ref, wpb_ref, wo_ref, wrh_ref, wrl_ref, br_ref, tri_ref, cin_ref,
                 x1_ref, h2_ref, idx_ref, wt_ref, rank_ref, cout_ref, cnt_sc, *, tm):
    first = jnp.logical_and(pl.program_id(0) == 0, pl.program_id(1) == 0)

    @pl.when(first)
    def _init():
        cnt_sc[...] = cin_ref[...]

    merged = (ga_ref[...].astype(F32) * jnp.dot(ya_ref[...], wpa_ref[...], preferred_element_type=F32)
              + gb_ref[...].astype(F32) * jnp.dot(yb_ref[...], wpb_ref[...], preferred_element_type=F32))
    x1 = x_ref[...] + g1_ref[0] * jnp.dot(merged.astype(BF16), wo_ref[...], preferred_element_type=F32)
    x1_ref[...] = x1
    ms = jnp.mean(x1 * x1, axis=-1, keepdims=True)
    h2 = x1 * lax.rsqrt(ms + NORM_EPS) * gn2_ref[...]
    h2 = h2 * (1.0 + sc2_ref[0]) + sh2_ref[0]
    h2_ref[...] = h2

    hh = h2.astype(BF16)
    hl = (h2 - hh.astype(F32)).astype(BF16)
    wrh = wrh_ref[...]
    logits = (lax.dot_general(wrh, hh, NT_DIMS, preferred_element_type=F32)
              + lax.dot_general(wrh, hl, NT_DIMS, preferred_element_type=F32)
              + lax.dot_general(wrl_ref[...], hh, NT_DIMS, preferred_element_type=F32))
    scores = _sigmoid(logits)
    biased = scores + br_ref[...]

    b3 = biased.reshape(N_GROUPS, GROUP_SIZE, tm)
    it3 = lax.broadcasted_iota(I32, (N_GROUPS, GROUP_SIZE, tm), 1).astype(F32)
    m1 = jnp.max(b3, axis=1, keepdims=True)
    i1 = jnp.min(jnp.where(b3 == m1, it3, float(GROUP_SIZE)), axis=1, keepdims=True)
    m2 = jnp.max(jnp.where(it3 == i1, -jnp.inf, b3), axis=1, keepdims=True)
    gs = (m1 + m2).reshape(N_GROUPS, tm)

    gi = lax.broadcasted_iota(I32, (N_GROUPS, tm), 0)
    beat = jnp.zeros((N_GROUPS, tm), F32)
    for g in range(N_GROUPS):
        r = gs[g:g + 1, :]
        beat = beat + jnp.where((r > gs) | ((r == gs) & (g < gi)), 1.0, 0.0)
    gpen = jnp.where(beat < TOPK_GROUPS, 0.0, -jnp.inf)
    masked = (b3 + gpen.reshape(N_GROUPS, 1, tm)).reshape(N_EXPERTS, tm)

    ei = lax.broadcasted_iota(I32, (N_EXPERTS, tm), 0).astype(F32)
    idx_rows, w_rows = [], []
    chosen = jnp.zeros((N_EXPERTS, tm), F32)
    for _ in range(TOP_K):
        m = jnp.max(masked, axis=0, keepdims=True)
        ik = jnp.min(jnp.where(masked == m, ei, float(N_EXPERTS)), axis=0, keepdims=True)
        hit = ei == ik
        w_rows.append(jnp.sum(jnp.where(hit, scores, 0.0), axis=0, keepdims=True))
        idx_rows.append(ik)
        masked = jnp.where(hit, -jnp.inf, masked)
        chosen = jnp.where(hit, 1.0, chosen)
    idx_ref[...] = jnp.concatenate(idx_rows, axis=0).astype(I32)
    wts = jnp.concatenate(w_rows, axis=0)
    wt_ref[...] = wts / jnp.sum(wts, axis=0, keepdims=True) * ROUTED_SCALE

    before = cnt_sc[...] + jnp.dot(chosen.astype(BF16), tri_ref[...], preferred_element_type=F32)
    rank_rows = [jnp.sum(jnp.where(ei == idx_rows[kk], before, 0.0), axis=0, keepdims=True)
                 for kk in range(TOP_K)]
    rank_ref[...] = jnp.concatenate(rank_rows, axis=0).astype(I32)
    cnt_sc[...] = cnt_sc[...] + jnp.sum(chosen, axis=1, keepdims=True)
    cout_ref[...] = cnt_sc[...]


def _post(x2, ya, yb, ga, gb, gate1, shift2, scale2, prm, count_in, nb, tm):
    t, d = x2.shape
    ns = t // nb // tm
    row = lambda w: pl.BlockSpec((tm, w), lambda b, i: (b * ns + i, 0))
    colb = pl.BlockSpec((TOP_K, tm), lambda b, i: (0, b * ns + i))
    r = jnp.arange(tm)
    tri = (r[:, None] < r[None, :]).astype(BF16)
    in_specs = [row(d), row(FOX_WIDTH), row(HGRN_WIDTH), row(d), row(d),
                _mod_spec(gate1, tm, ns), _mod_spec(shift2, tm, ns), _mod_spec(scale2, tm, ns),
                _const_spec((1, d)), _const_spec((FOX_WIDTH, d)), _const_spec((HGRN_WIDTH, d)),
                _const_spec((d, d)), _const_spec((N_EXPERTS, d)), _const_spec((N_EXPERTS, d)),
                _const_spec((N_EXPERTS, 1)), _const_spec((tm, tm)), _const_spec((N_EXPERTS, 1))]
    out_shape = [jax.ShapeDtypeStruct((t, d), F32), jax.ShapeDtypeStruct((t, d), F32),
                 jax.ShapeDtypeStruct((TOP_K, t), I32), jax.ShapeDtypeStruct((TOP_K, t), F32),
                 jax.ShapeDtypeStruct((TOP_K, t), I32), jax.ShapeDtypeStruct((N_EXPERTS, 1), F32)]
    out_specs = [row(d), row(d), colb, colb, colb, pl.BlockSpec((N_EXPERTS, 1), lambda b, i: (0, 0))]
    return pl.pallas_call(
        functools.partial(_post_kernel, tm=tm), grid=(nb, ns), in_specs=in_specs,
        out_specs=out_specs, out_shape=out_shape,
        scratch_shapes=[pltpu.VMEM((N_EXPERTS, 1), F32)],
        compiler_params=_cparams(("arbitrary", "arbitrary")), name="post",
    )(x2, ya, yb, ga, gb, gate1, shift2, scale2, prm["g2"], prm["wpa"], prm["wpb"], prm["wo"],
      prm["wrh"], prm["wrl"], prm["br"], tri, count_in)


def _pos_kernel(idx_ref, rank_ref, pst_ref, pos_ref, *, tm):
    ei = lax.broadcasted_iota(I32, (N_EXPERTS, tm), 0)
    pst = pst_ref[...]
    rows = [jnp.sum(jnp.where(ei == idx_ref[kk:kk + 1, :], pst, 0.0), axis=0, keepdims=True)
            for kk in range(TOP_K)]
    pos_ref[...] = jnp.concatenate(rows, axis=0).astype(I32) + rank_ref[...]


def _pos(idx, rank, pstart_col, tm):
    t = idx.shape[1]
    colb = pl.BlockSpec((TOP_K, tm), lambda i: (0, i))
    return pl.pallas_call(
        functools.partial(_pos_kernel, tm=tm), grid=(t // tm,),
        in_specs=[colb, colb, _const_spec((N_EXPERTS, 1))], out_specs=colb,
        out_shape=jax.ShapeDtypeStruct((TOP_K, t), I32),
        compiler_params=_cparams(("parallel",)), name="moe_pos",
    )(idx, rank, pstart_col)


def _padfill_kernel(cnt_ref, pad_ref, pst_ref, xs_ref, zero_sc, sem):
    zero_sc[...] = jnp.zeros(zero_sc.shape, F32)
    sub = 8
    sizes = [1 << b for b in reversed(range(3, MOE_BLK.bit_length() - 1))]

    def chunks(e, act):
        n = pad_ref[e] - cnt_ref[e]
        row = pst_ref[e] + cnt_ref[e]
        head = jnp.minimum((-cnt_ref[e]) & (sub - 1), n)
        for j in range(sub - 1):
            @pl.when(j < head)
            def _():
                act(pltpu.make_async_copy(zero_sc.at[pl.ds(0, 1)], xs_ref.at[pl.ds(row + j, 1)], sem))

        row = pl.multiple_of(row + head, sub)
        rest = n - head
        for size in sizes:
            @pl.when((rest & size) != 0)
            def _():
                act(pltpu.make_async_copy(zero_sc.at[pl.ds(0, size)], xs_ref.at[pl.ds(row, size)], sem))

            row = pl.multiple_of(row + (rest & size), sub)

    def issue(e, c):
        chunks(e, lambda cp: cp.start())
        return c

    def drain(e, c):
        chunks(e, lambda cp: cp.wait())
        return c

    lax.fori_loop(0, N_EXPERTS, issue, 0)
    lax.fori_loop(0, N_EXPERTS, drain, 0)


def _padfill(counts, padded, pstart, n_rows, d):
    return pl.pallas_call(
        _padfill_kernel,
        grid_spec=pltpu.PrefetchScalarGridSpec(
            num_scalar_prefetch=3, grid=(1,), in_specs=[],
            out_specs=pl.BlockSpec(memory_space=pl.ANY),
            scratch_shapes=[pltpu.VMEM((MOE_BLK // 2, d), F32), pltpu.SemaphoreType.DMA(())]),
        out_shape=jax.ShapeDtypeStruct((n_rows, d), F32),
        compiler_params=pltpu.CompilerParams(dimension_semantics=("arbitrary",), has_side_effects=True),
        name="moe_padfill",
    )(counts, padded, pstart)


def _dispatch_kernel(pos_ref, h2_ref, xs_in_ref, xs_ref, sem, *, tm):
    del xs_in_ref

    def one(t, c):
        for kk in range(TOP_K):
            pltpu.make_async_copy(h2_ref.at[pl.ds(t, 1)], xs_ref.at[pl.ds(pos_ref[kk, t], 1)],
                                  sem).start(priority=kk % 2)
        return c

    lax.fori_loop(0, tm, one, 0)
    for kk in range(TOP_K):
        pltpu.make_async_copy(h2_ref, xs_ref.at[pl.ds(0, tm)], sem).wait()


def _dispatch(pos, h2, xs, tm):
    t, d = h2.shape
    return pl.pallas_call(
        functools.partial(_dispatch_kernel, tm=tm),
        grid=(t // tm,),
        in_specs=[pl.BlockSpec((TOP_K, tm), lambda i: (0, i), memory_space=pltpu.SMEM),
                  pl.BlockSpec((tm, d), lambda i: (i, 0)),
                  pl.BlockSpec(memory_space=pl.ANY)],
        out_specs=pl.BlockSpec(memory_space=pl.ANY),
        out_shape=jax.ShapeDtypeStruct(xs.shape, F32),
        scratch_shapes=[pltpu.SemaphoreType.DMA(())],
        input_output_aliases={2: 0},
        compiler_params=pltpu.CompilerParams(dimension_semantics=("arbitrary",), has_side_effects=True,
                                             vmem_limit_bytes=VMEM_LIMIT),
        name="moe_dispatch",
    )(pos, h2, xs)


def _moe_kernel(be_ref, nu_ref, x_ref, wg_ref, wu_ref, wd_ref, y_ref, wgu_sc, wd_sc):
    i = pl.program_id(0)
    e = be_ref[i]
    e_prev = be_ref[jnp.maximum(i - 1, 0)]

    @pl.when(jnp.logical_or(i == 0, e != e_prev))
    def _load():
        wgu_sc[:, 0:D_EXPERT] = wg_ref[0].astype(BF16)
        wgu_sc[:, D_EXPERT:2 * D_EXPERT] = wu_ref[0].astype(BF16)
        wd_sc[...] = wd_ref[0].astype(BF16)

    @pl.when(i < nu_ref[0])
    def _compute():
        gu = jnp.dot(x_ref[...].astype(BF16), wgu_sc[...], preferred_element_type=F32)
        gt = gu[:, 0:D_EXPERT]
        h = gt * _sigmoid(gt) * gu[:, D_EXPERT:2 * D_EXPERT]
        y_ref[...] = jnp.dot(h.astype(BF16), wd_sc[...], preferred_element_type=F32)


def _moe(blk_expert, n_used, xs, w_gate, w_up, w_down):
    n_rows, d = xs.shape
    nblk = n_rows // MOE_BLK
    xmap = lambda i, be, nu: (jnp.minimum(i, nu[0] - 1), 0)
    wmap = lambda i, be, nu: (be[i], 0, 0)
    return pl.pallas_call(
        _moe_kernel,
        grid_spec=pltpu.PrefetchScalarGridSpec(
            num_scalar_prefetch=2, grid=(nblk,),
            in_specs=[pl.BlockSpec((MOE_BLK, d), xmap),
                      pl.BlockSpec((1, d, D_EXPERT), wmap), pl.BlockSpec((1, d, D_EXPERT), wmap),
                      pl.BlockSpec((1, D_EXPERT, d), wmap)],
            out_specs=pl.BlockSpec((MOE_BLK, d), xmap),
            scratch_shapes=[pltpu.VMEM((d, 2 * D_EXPERT), BF16), pltpu.VMEM((D_EXPERT, d), BF16)]),
        out_shape=jax.ShapeDtypeStruct((n_rows, d), F32),
        compiler_params=_cparams(("arbitrary",)), name="moe_experts",
    )(blk_expert, n_used, xs, w_gate, w_up, w_down)


def _combine_kernel(pos_ref, x1_ref, h2_ref, g2_ref, wt_ref, wsgu_ref, wsd_ref, ys_ref, o_ref, buf, sem, *, tm):
    def copy(kk, t, row):
        return pltpu.make_async_copy(ys_ref.at[pl.ds(row, 1)], buf.at[kk, pl.ds(t, 1)], sem)

    def one(t, c):
        for kk in range(TOP_K):
            copy(kk, t, pos_ref[kk, t]).start(priority=kk % 2)
        return c

    lax.fori_loop(0, tm, one, 0)

    gu = jnp.dot(h2_ref[...].astype(BF16), wsgu_ref[...], preferred_element_type=F32)
    gt = gu[:, 0:D_SHARED]
    hs = gt * _sigmoid(gt) * gu[:, D_SHARED:2 * D_SHARED]
    acc = jnp.dot(hs.astype(BF16), wsd_ref[...], preferred_element_type=F32)

    for kk in range(TOP_K):
        pltpu.make_async_copy(ys_ref.at[pl.ds(0, tm)], buf.at[kk], sem).wait()

    d = acc.shape[-1]
    for kk in range(TOP_K):
        wrow = jnp.broadcast_to(wt_ref[kk:kk + 1, :], (LANES, tm))
        wcol = wrow.T
        acc = acc + buf[kk] * jnp.concatenate([wcol] * (d // LANES), axis=1)
    o_ref[...] = x1_ref[...] + g2_ref[0] * acc


def _combine(pos, x1, h2, gate2, wts, wsgu, wsd, ys, nb, tm):
    t, d = x1.shape
    ns = t // nb // tm
    row = pl.BlockSpec((tm, d), lambda b, i: (b * ns + i, 0))
    return pl.pallas_call(
        functools.partial(_combine_kernel, tm=tm),
        grid=(nb, ns),
        in_specs=[pl.BlockSpec((TOP_K, tm), lambda b, i: (0, b * ns + i), memory_space=pltpu.SMEM),
                  row, row, _mod_spec(gate2, tm, ns),
                  pl.BlockSpec((TOP_K, tm), lambda b, i: (0, b * ns + i)),
                  _const_spec(wsgu.shape), _const_spec(wsd.shape),
                  pl.BlockSpec(memory_space=pl.ANY)],
        out_specs=row,
        out_shape=jax.ShapeDtypeStruct((t, d), F32),
        scratch_shapes=[pltpu.VMEM((TOP_K, tm, d), F32), pltpu.SemaphoreType.DMA(())],
        compiler_params=_cparams(("arbitrary", "arbitrary")), name="moe_combine",
    )(pos, x1, h2, gate2, wts, wsgu, wsd, ys)


def _prepare_params(g_norm1, w_in, b_fox_f, g_q, g_k, hgrn_lb, g_hgrn_o, w_proj_a, w_proj_b, w_out,
                    g_norm2, w_router, b_router, w_sh_gate, w_sh_up, w_sh_down):
    d = D_MODEL
    fw, hw = FOX_WIDTH, HGRN_WIDTH
    c0 = 3 * fw
    c1 = c0 + FOX_HEADS
    c2 = c1 + 4 * hw
    wff = w_in[:, c0:c1]
    head = jnp.arange(fw) // FOX_HEAD_DIM
    wr_t = w_router.T
    wrh = wr_t.astype(BF16)
    return dict(
        g1=g_norm1.reshape(1, d),
        wqkv=w_in[:, :c0].astype(BF16),
        wff=jnp.pad(wff, ((0, 0), (0, LANES - FOX_HEADS))).astype(BF16),
        wfft=jnp.pad(wff.T, ((0, 16 - FOX_HEADS), (0, 0))).astype(BF16),
        bf=b_fox_f.reshape(1, FOX_HEADS), bft=b_fox_f.reshape(FOX_HEADS, 1),
        gq=jnp.tile(g_q, FOX_HEADS).reshape(1, fw), gk=jnp.tile(g_k, FOX_HEADS).reshape(1, fw),
        bd=(head[:, None] == head[None, :]).astype(BF16) * (1.0 / FOX_HEAD_DIM),
        lbp=hgrn_lb,
        wh=w_in[:, c1:c2].astype(BF16), wg=w_in[:, c2:].astype(BF16),
        gn=g_hgrn_o.reshape(1, HGRN_DK),
        g2=g_norm2.reshape(1, d),
        wpa=w_proj_a.astype(BF16), wpb=w_proj_b.astype(BF16), wo=w_out.astype(BF16),
        wrh=wrh, wrl=(wr_t - wrh.astype(F32)).astype(BF16), br=b_router.reshape(N_EXPERTS, 1),
        wsgu=jnp.concatenate([w_sh_gate, w_sh_up], axis=1).astype(BF16), wsd=w_sh_down.astype(BF16),
    )


def _mixers(x2, nb, seq, tm, shift1, scale1, prm, past, tile, tb):
    (q, k, kb, v, vb, lf, lft, hq, hl, hi, og, ga, gb) = _inproj(x2, nb if shift1.shape[1] == 1 else 1, tm,
                                                                 shift1, scale1, prm)
    t = nb * seq
    hp = FOX_HEADS // 2
    lft = lft.transpose(1, 0, 2).reshape(FOX_HEADS, nb, seq).transpose(1, 0, 2)
    if past is None:
        fcum = _cumsum_lanes(lft.reshape(nb * FOX_HEADS, seq)).reshape(nb, hp, 2, seq)
        ya = _fox_prompt(q, kb, vb, fcum, nb, seq, tile, min(32, tile))
        s0 = None
    else:
        cache_k, cache_v, cache_lf, s0 = past
        plen = cache_k.shape[1]
        ltot = plen + seq
        lpad = -(-ltot // LANES) * LANES
        lf_all = jnp.concatenate([cache_lf.transpose(0, 2, 1), lft,
                                  jnp.zeros((nb, FOX_HEADS, lpad - ltot), F32)], axis=-1)
        fcum = _cumsum_lanes(lf_all.reshape(nb * FOX_HEADS, lpad)).reshape(nb, hp, 2, lpad)
        ya = _fox_sample(q, kb, vb, cache_k.reshape(nb, plen, FOX_WIDTH), cache_v.reshape(nb, plen, FOX_WIDTH),
                         fcum, nb, seq, plen)
    yb, sfin = _hgrn(hq, hl, hi, og, prm["gn"], s0, nb, seq, tb)
    return k, v, lf, sfin, ya, yb, ga, gb


def kernel(x_prompt, x_sample, cache_fox_k, cache_fox_v, cache_fox_logf, state_hgrn, c_prompt, c_sample,
           w_ada, b_ada, g_norm1, w_in, b_fox_f, g_q, g_k, hgrn_lb, g_hgrn_o, w_proj_a, w_proj_b, w_out,
           g_norm2, w_router, b_router, w_exp_gate, w_exp_up, w_exp_down, w_sh_gate, w_sh_up, w_sh_down):
    assert w_ada.shape[0] == 1 and hgrn_lb.shape[0] == 2, "single-layer trunk"
    d = D_MODEL
    bp, sp, _ = x_prompt.shape
    bs, ss, _ = x_sample.shape
    tp, ts = bp * sp, bs * ss
    prm = _prepare_params(g_norm1[0], w_in[0], b_fox_f[0], g_q[0], g_k[0], hgrn_lb, g_hgrn_o[0],
                          w_proj_a[0], w_proj_b[0], w_out[0], g_norm2[0], w_router[0], b_router[0],
                          w_sh_gate[0], w_sh_up[0], w_sh_down[0])

    bc = bp + bs
    bc_pad = -(-bc // 8) * 8
    c_all = jnp.concatenate([c_prompt, c_sample, jnp.zeros((bc_pad - bc, d), F32)], axis=0)
    mod = _ada(c_all, w_ada[0], b_ada[0])
    mod_p = [mod[:bp, j * d:(j + 1) * d].reshape(bp, 1, d) for j in range(6)]
    mod_s = [jnp.repeat(mod[bp:bc, j * d:(j + 1) * d], ss, axis=0).reshape(1, ts, d) for j in range(6)]

    tm_p = min(512, sp)
    tm_s = min(512, ts)
    xp2 = x_prompt.reshape(tp, d)
    xs2 = x_sample.reshape(ts, d)
    tile = min(1024, sp)
    tb = min(256, sp)

    kp, vp, lfp, sfin_p, ya_p, yb_p, ga_p, gb_p = _mixers(
        xp2, bp, sp, tm_p, mod_p[0], mod_p[1], prm, None, tile, tb)
    past = (cache_fox_k[0], cache_fox_v[0], cache_fox_logf[0], state_hgrn[0])
    ks, vs, lfs, sfin_s, ya_s, yb_s, ga_s, gb_s = _mixers(
        xs2, bs, ss, tm_s, mod_s[0], mod_s[1], prm, past, None, min(256, ss))

    zero_cnt = jnp.zeros((N_EXPERTS, 1), F32)
    x1_p, h2_p, idx_p, wt_p, rank_p, cnt_p = _post(xp2, ya_p, yb_p, ga_p, gb_p, mod_p[2], mod_p[3], mod_p[4],
                                                    prm, zero_cnt, bp, tm_p)
    x1_s, h2_s, idx_s, wt_s, rank_s, cnt_all = _post(xs2, ya_s, yb_s, ga_s, gb_s, mod_s[2], mod_s[3], mod_s[4],
                                                     prm, cnt_p, 1, tm_s)

    counts = cnt_all.reshape(N_EXPERTS).astype(I32)
    padded = (counts + MOE_BLK - 1) // MOE_BLK * MOE_BLK
    pend = jnp.cumsum(padded)
    pstart = pend - padded
    nblk = -(-((tp + ts) * TOP_K + N_EXPERTS * (MOE_BLK - 1)) // MOE_BLK)
    n_used = (pend[-1] // MOE_BLK).reshape(1)
    blk_row0 = jnp.arange(nblk, dtype=I32) * MOE_BLK
    blk_expert = jnp.minimum(jnp.sum((pend[None, :] <= blk_row0[:, None]).astype(I32), axis=1), N_EXPERTS - 1)
    blk_expert = jnp.where(jnp.arange(nblk) < n_used[0], blk_expert, blk_expert[jnp.maximum(n_used[0] - 1, 0)])
    pstart_col = pstart.astype(F32).reshape(N_EXPERTS, 1)
    pos_p = _pos(idx_p, rank_p, pstart_col, tm_p)
    pos_s = _pos(idx_s, rank_s, pstart_col, tm_s)

    xs_pool = _padfill(counts, padded, pstart, nblk * MOE_BLK, d)
    tm_d = min(256, tm_p)
    xs_pool = _dispatch(pos_p, h2_p, xs_pool, tm_d)
    xs_pool = _dispatch(pos_s, h2_s, xs_pool, min(256, tm_s))
    ys_pool = _moe(blk_expert, n_used, xs_pool, w_exp_gate[0], w_exp_up[0], w_exp_down[0])

    tm_c = min(128, sp)
    y_p = _combine(pos_p, x1_p, h2_p, mod_p[5], wt_p, prm["wsgu"], prm["wsd"], ys_pool, bp, tm_c)
    y_s = _combine(pos_s, x1_s, h2_s, mod_s[5], wt_s, prm["wsgu"], prm["wsd"], ys_pool, 1, min(128, ts))

    return (y_p.reshape(bp, sp, d), y_s.reshape(bs, ss, d),
            kp.reshape(1, bp, sp, FOX_HEADS, FOX_HEAD_DIM), vp.reshape(1, bp, sp, FOX_HEADS, FOX_HEAD_DIM),
            lfp.reshape(1, bp, sp, FOX_HEADS), sfin_p[None],
            ks.reshape(1, bs, ss, FOX_HEADS, FOX_HEAD_DIM), vs.reshape(1, bs, ss, FOX_HEADS, FOX_HEAD_DIM),
            lfs.reshape(1, bs, ss, FOX_HEADS), sfin_s[None])
```

```python
import functools

import jax
import jax.numpy as jnp
from jax import lax
from jax.experimental import pallas as pl
from jax.experimental.pallas import tpu as pltpu

F32 = jnp.float32
BF16 = jnp.bfloat16
I32 = jnp.int32

D_MODEL = 1024
FOX_HEADS = 8
FOX_HEAD_DIM = 64
FOX_WIDTH = FOX_HEADS * FOX_HEAD_DIM
HGRN_HEADS = 4
HGRN_DK = 128
HGRN_WIDTH = HGRN_HEADS * HGRN_DK
N_EXPERTS = 256
TOP_K = 8
N_GROUPS = 8
TOPK_GROUPS = 4
GROUP_SIZE = N_EXPERTS // N_GROUPS
D_EXPERT = 256
D_SHARED = 256
ROUTED_SCALE = 2.5
NORM_EPS = 1e-6
NEG_BIG = -1e30
LOG2E = 1.4426950408889634
QK_SCALE = FOX_HEAD_DIM ** -0.5 * LOG2E

LANES = 128
SUB = 16
MOE_BLK = 256
VMEM_LIMIT = 56 * 1024 * 1024

NT_DIMS = (((1,), (1,)), ((), ()))
TN_DIMS = (((0,), (0,)), ((), ()))


def _cparams(sem):
    return pltpu.CompilerParams(dimension_semantics=sem, vmem_limit_bytes=VMEM_LIMIT)


def _const_spec(shape):
    nd = len(shape)
    return pl.BlockSpec(shape, lambda *_: (0,) * nd, pipeline_mode=pl.Buffered(1))


def _sigmoid(z):
    return 1.0 / (1.0 + jnp.exp(-z))


def _log_sigmoid(z):
    return jnp.minimum(z, 0.0) - jnp.log(1.0 + jnp.exp(-jnp.abs(z)))


def _split3(a):
    hi = a.astype(BF16)
    r1 = a - hi.astype(F32)
    mid = r1.astype(BF16)
    lo = (r1 - mid.astype(F32)).astype(BF16)
    return hi, mid, lo


def _ada_kernel(c_ref, w_ref, b_ref, o_ref):
    c = c_ref[...]
    s = c * _sigmoid(c)
    o_ref[...] = jnp.dot(s.astype(BF16), w_ref[...].astype(BF16), preferred_element_type=F32) + b_ref[...]


def _ada(c_all, w_ada, b_ada):
    bc, d = c_all.shape
    n = w_ada.shape[1]
    tn = 1024
    return pl.pallas_call(
        _ada_kernel,
        grid=(n // tn,),
        in_specs=[pl.BlockSpec((bc, d), lambda j: (0, 0)),
                  pl.BlockSpec((d, tn), lambda j: (0, j)),
                  pl.BlockSpec((1, tn), lambda j: (0, j))],
        out_specs=pl.BlockSpec((bc, tn), lambda j: (0, j)),
        out_shape=jax.ShapeDtypeStruct((bc, n), F32),
        compiler_params=_cparams(("parallel",)),
        name="ada",
    )(c_all, w_ada, b_ada.reshape(1, n))


def _inproj_kernel(x_ref, sh_ref, sc_ref, g1_ref, wqkv_ref, wff_ref, wfft_ref, bf_ref, bft_ref,
                   gq_ref, gk_ref, bd_ref, lbp_ref, wh_ref, wg_ref,
                   q_ref, k_ref, kb_ref, v_ref, vb_ref, lf_ref, lft_ref,
                   hq_ref, hl_ref, hi_ref, og_ref, ga_ref, gb_ref):
    x = x_ref[...]
    ms = jnp.mean(x * x, axis=-1, keepdims=True)
    h = x * lax.rsqrt(ms + NORM_EPS) * g1_ref[...]
    h = h * (1.0 + sc_ref[0]) + sh_ref[0]
    hb = h.astype(BF16)

    def headnorm(a, g):
        ss = jnp.dot((a * a).astype(BF16), bd_ref[...], preferred_element_type=F32)
        return a * lax.rsqrt(ss + NORM_EPS) * g

    fq = jnp.dot(hb, wqkv_ref[:, 0:FOX_WIDTH], preferred_element_type=F32)
    q_ref[...] = (headnorm(fq, gq_ref[...]) * QK_SCALE).astype(BF16)
    fk = jnp.dot(hb, wqkv_ref[:, FOX_WIDTH:2 * FOX_WIDTH], preferred_element_type=F32)
    k = headnorm(fk, gk_ref[...])
    k_ref[...] = k
    kb_ref[...] = k.astype(BF16)
    fv = jnp.dot(hb, wqkv_ref[:, 2 * FOX_WIDTH:3 * FOX_WIDTH], preferred_element_type=F32)
    v_ref[...] = fv
    vb_ref[...] = fv.astype(BF16)

    ff = jnp.dot(hb, wff_ref[...], preferred_element_type=F32)
    lf_ref[...] = _log_sigmoid(ff[:, 0:FOX_HEADS] + bf_ref[...])
    fft = lax.dot_general(wfft_ref[...], hb, NT_DIMS, preferred_element_type=F32)
    lft_ref[0] = _log_sigmoid(fft[0:FOX_HEADS, :] + bft_ref[...])

    lbp = lbp_ref[...]
    e = jnp.exp(lbp - jnp.max(lbp, axis=0, keepdims=True))
    lb = e[0:1, :] / jnp.sum(e, axis=0, keepdims=True)

    w = HGRN_WIDTH
    hq = jnp.dot(hb, wh_ref[:, 0:w], preferred_element_type=F32)
    hq_ref[...] = hq * _sigmoid(hq)
    hf = jnp.dot(hb, wh_ref[:, w:2 * w], preferred_element_type=F32)
    hl_ref[...] = jnp.log(lb + (1.0 - lb) * _sigmoid(hf))
    hi_ref[...] = jnp.dot(hb, wh_ref[:, 2 * w:3 * w], preferred_element_type=F32)
    hg = jnp.dot(hb, wh_ref[:, 3 * w:4 * w], preferred_element_type=F32)
    og_ref[...] = _sigmoid(hg)
    ga = jnp.dot(hb, wg_ref[:, 0:D_MODEL], preferred_element_type=F32)
    ga_ref[...] = _sigmoid(ga).astype(BF16)
    gb = jnp.dot(hb, wg_ref[:, D_MODEL:2 * D_MODEL], preferred_element_type=F32)
    gb_ref[...] = _sigmoid(gb).astype(BF16)


def _mod_spec(arr, tm, ns):
    d = arr.shape[-1]
    if arr.shape[1] == 1:
        return pl.BlockSpec((1, 1, d), lambda b, i: (b, 0, 0))
    return pl.BlockSpec((1, tm, d), lambda b, i: (0, b * ns + i, 0))


def _inproj(x2, nb, tm, shift1, scale1, prm):
    t, d = x2.shape
    ns = t // nb // tm
    row = lambda w: pl.BlockSpec((tm, w), lambda b, i: (b * ns + i, 0))
    in_specs = [row(d), _mod_spec(shift1, tm, ns), _mod_spec(scale1, tm, ns),
                _const_spec((1, d)), _const_spec(prm["wqkv"].shape), _const_spec(prm["wff"].shape),
                _const_spec(prm["wfft"].shape), _const_spec((1, FOX_HEADS)), _const_spec((FOX_HEADS, 1)),
                _const_spec((1, FOX_WIDTH)), _const_spec((1, FOX_WIDTH)), _const_spec((FOX_WIDTH, FOX_WIDTH)),
                _const_spec(prm["lbp"].shape), _const_spec(prm["wh"].shape), _const_spec(prm["wg"].shape)]
    fw, hw = FOX_WIDTH, HGRN_WIDTH
    out_shape = [jax.ShapeDtypeStruct((t, fw), BF16),
                 jax.ShapeDtypeStruct((t, fw), F32),
                 jax.ShapeDtypeStruct((t, fw), BF16),
                 jax.ShapeDtypeStruct((t, fw), F32),
                 jax.ShapeDtypeStruct((t, fw), BF16),
                 jax.ShapeDtypeStruct((t, FOX_HEADS), F32),
                 jax.ShapeDtypeStruct((nb * ns, FOX_HEADS, tm), F32),
                 jax.ShapeDtypeStruct((t, hw), F32),
                 jax.ShapeDtypeStruct((t, hw), F32),
                 jax.ShapeDtypeStruct((t, hw), F32),
                 jax.ShapeDtypeStruct((t, hw), F32),
                 jax.ShapeDtypeStruct((t, d), BF16),
                 jax.ShapeDtypeStruct((t, d), BF16)]
    out_specs = [row(fw), row(fw), row(fw), row(fw), row(fw), row(FOX_HEADS),
                 pl.BlockSpec((1, FOX_HEADS, tm), lambda b, i: (b * ns + i, 0, 0)),
                 row(hw), row(hw), row(hw), row(hw), row(d), row(d)]
    return pl.pallas_call(
        _inproj_kernel, grid=(nb, ns), in_specs=in_specs, out_specs=out_specs, out_shape=out_shape,
        compiler_params=_cparams(("parallel", "parallel")), name="inproj",
    )(x2, shift1, scale1, prm["g1"], prm["wqkv"], prm["wff"], prm["wfft"], prm["bf"], prm["bft"],
      prm["gq"], prm["gk"], prm["bd"], prm["lbp"], prm["wh"], prm["wg"])


def _cumsum_kernel(x_ref, o_ref):
    x = x_ref[...]
    n = x.shape[-1]
    lane = lax.broadcasted_iota(I32, x.shape, 1)
    s = 1
    while s < n:
        x = x + jnp.where(lane >= s, pltpu.roll(x, s, axis=1), 0.0)
        s *= 2
    o_ref[...] = x * LOG2E


def _cumsum_lanes(x):
    return pl.pallas_call(
        _cumsum_kernel, out_shape=jax.ShapeDtypeStruct(x.shape, F32),
        compiler_params=pltpu.CompilerParams(vmem_limit_bytes=VMEM_LIMIT), name="cumsum",
    )(x)


def _fox_prompt_kernel(qt_ref, kt_ref, q_ref, k_ref, v_ref, fk_ref, o_ref,
                       m0, m1, l0, l1, a0, a1, s0, s1, p0, p1, *, tile, rsub):
    step = pl.program_id(2)
    qi = qt_ref[step]
    ki = kt_ref[step]
    heads = ((m0, l0, a0, s0, p0), (m1, l1, a1, s1, p1))
    lane = lax.broadcasted_iota(I32, (1, LANES), 1)
    lo_half = lane < FOX_HEAD_DIM

    @pl.when(ki == 0)
    def _init():
        for m_sc, l_sc, a_sc, _, _ in heads:
            m_sc[...] = jnp.full(m_sc.shape, NEG_BIG, F32)
            l_sc[...] = jnp.zeros(l_sc.shape, F32)
            a_sc[...] = jnp.zeros(a_sc.shape, F32)

    def tile_update(diagonal):
        q = q_ref[...]
        k = k_ref[...]
        v = v_ref[...]
        fk = fk_ref[0, 0]
        nsub = tile // rsub
        if diagonal:
            ahead = (lax.broadcasted_iota(I32, (rsub, tile), 1)
                     - lax.broadcasted_iota(I32, (rsub, tile), 0))
        for hs, (m_sc, l_sc, a_sc, s_sc, p_sc) in enumerate(heads):
            sel = lo_half if hs == 0 else jnp.logical_not(lo_half)
            qh = jnp.where(sel, q, jnp.zeros_like(q))
            s_sc[...] = lax.dot_general(qh, k, NT_DIMS, preferred_element_type=F32)

            def biased(r):
                rows = slice(r * rsub, (r + 1) * rsub)
                cw = min(tile, -(-((r + 1) * rsub) // LANES) * LANES) if diagonal else tile
                s = s_sc[rows, 0:cw] - fk[hs:hs + 1, 0:cw]
                if diagonal:
                    s = jnp.where(ahead[:, 0:cw] <= r * rsub, s, NEG_BIG)
                return s, rows, cw

            m_old = m_sc[...]
            m_parts = []
            for r in range(nsub):
                s, rows, _ = biased(r)
                m_parts.append(jnp.maximum(m_old[rows], jnp.max(s, axis=1, keepdims=True)))
            l_parts = []
            for r in range(nsub):
                s, rows, cw = biased(r)
                nc = cw // LANES
                p = jnp.exp2(s - jnp.concatenate([m_parts[r]] * nc, axis=1))
                part = p[:, 0:LANES]
                for c in range(1, nc):
                    part = part + p[:, c * LANES:(c + 1) * LANES]
                l_parts.append(part)
                p_sc[rows, 0:cw] = p.astype(BF16)
                if cw < tile:
                    p_sc[rows, cw:tile] = jnp.zeros((rsub, tile - cw), BF16)
            m_new = jnp.concatenate(m_parts, axis=0)
            alpha = jnp.exp2(m_old - m_new)
            m_sc[...] = m_new
            l_sc[...] = alpha * l_sc[...] + jnp.concatenate(l_parts, axis=0)
            a_sc[...] = alpha * a_sc[...] + jnp.dot(p_sc[...], v, preferred_element_type=F32)

    @pl.when(ki < qi)
    def _full():
        tile_update(False)

    @pl.when(ki == qi)
    def _diag():
        tile_update(True)
        o0 = a0[...] / jnp.sum(l0[...], axis=1, keepdims=True)
        o1 = a1[...] / jnp.sum(l1[...], axis=1, keepdims=True)
        o_ref[...] = jnp.where(lo_half, o0, o1).astype(o_ref.dtype)


def _fox_prompt(q, kb, vb, fcum, nb, seq, tile, rsub):
    nt = seq // tile
    hp = FOX_HEADS // 2
    qt = jnp.asarray([qi for qi in range(nt) for _ in range(qi + 1)], I32)
    kt = jnp.asarray([ki for qi in range(nt) for ki in range(qi + 1)], I32)
    qmap = lambda b, h, s, qt, kt: (b * nt + qt[s], h)
    kmap = lambda b, h, s, qt, kt: (b * nt + kt[s], h)
    fmap = lambda b, h, s, qt, kt: (b, h, 0, kt[s])
    stat = pltpu.VMEM((tile, LANES), F32)
    return pl.pallas_call(
        functools.partial(_fox_prompt_kernel, tile=tile, rsub=rsub),
        grid_spec=pltpu.PrefetchScalarGridSpec(
            num_scalar_prefetch=2, grid=(nb, hp, nt * (nt + 1) // 2),
            in_specs=[pl.BlockSpec((tile, LANES), qmap), pl.BlockSpec((tile, LANES), kmap),
                      pl.BlockSpec((tile, LANES), kmap), pl.BlockSpec((1, 1, 2, tile), fmap)],
            out_specs=pl.BlockSpec((tile, LANES), qmap),
            scratch_shapes=[stat, stat, stat, stat, stat, stat,
                            pltpu.VMEM((tile, tile), F32), pltpu.VMEM((tile, tile), F32),
                            pltpu.VMEM((tile, tile), BF16), pltpu.VMEM((tile, tile), BF16)]),
        out_shape=jax.ShapeDtypeStruct((nb * seq, FOX_WIDTH), BF16),
        compiler_params=_cparams(("parallel", "parallel", "arbitrary")),
        name="fox_prompt",
    )(qt, kt, q, kb, vb, fcum)


def _fox_sample_kernel(q_ref, kn_ref, vn_ref, kp_ref, vp_ref, fk_ref, o_ref, *, past, seq):
    lane = lax.broadcasted_iota(I32, (1, LANES), 1)
    lo_half = lane < FOX_HEAD_DIM
    q = q_ref[...]
    kn = kn_ref[...]
    vn = vn_ref[...]
    kp = kp_ref[0].astype(BF16)
    vp = vp_ref[0].astype(BF16)
    fk = fk_ref[0, 0]
    row = lax.broadcasted_iota(I32, (seq, seq), 0)
    col = lax.broadcasted_iota(I32, (seq, seq), 1)
    causal = col <= row
    outs = []
    for hs in range(2):
        sel = lo_half if hs == 0 else jnp.logical_not(lo_half)
        qh = jnp.where(sel, q, jnp.zeros_like(q))
        sp = lax.dot_general(qh, kp, NT_DIMS, preferred_element_type=F32) - fk[hs:hs + 1, 0:past]
        sn = lax.dot_general(qh, kn, NT_DIMS, preferred_element_type=F32) - fk[hs:hs + 1, past:past + seq]
        sn = jnp.where(causal, sn, NEG_BIG)
        m = jnp.maximum(jnp.max(sp, axis=1, keepdims=True), jnp.max(sn, axis=1, keepdims=True))
        pp = jnp.exp2(sp - m)
        pn = jnp.exp2(sn - m)
        l = jnp.sum(pp, axis=1, keepdims=True) + jnp.sum(pn, axis=1, keepdims=True)
        o = (jnp.dot(pp.astype(BF16), vp, preferred_element_type=F32)
             + jnp.dot(pn.astype(BF16), vn, preferred_element_type=F32))
        outs.append(o / l)
    o_ref[...] = jnp.where(lo_half, outs[0], outs[1]).astype(o_ref.dtype)


def _fox_sample(q, kb, vb, cache_k, cache_v, fcum, nb, seq, past):
    hp = FOX_HEADS // 2
    lpad = fcum.shape[-1]
    rmap = lambda b, h: (b, h)
    cmap = lambda b, h: (b, 0, h)
    return pl.pallas_call(
        functools.partial(_fox_sample_kernel, past=past, seq=seq),
        grid=(nb, hp),
        in_specs=[pl.BlockSpec((seq, LANES), rmap), pl.BlockSpec((seq, LANES), rmap),
                  pl.BlockSpec((seq, LANES), rmap),
                  pl.BlockSpec((1, past, LANES), cmap), pl.BlockSpec((1, past, LANES), cmap),
                  pl.BlockSpec((1, 1, 2, lpad), lambda b, h: (b, h, 0, 0))],
        out_specs=pl.BlockSpec((seq, LANES), rmap),
        out_shape=jax.ShapeDtypeStruct((nb * seq, FOX_WIDTH), BF16),
        compiler_params=_cparams(("parallel", "parallel")),
        name="fox_sample",
    )(q, kb, vb, cache_k, cache_v, fcum)


def _hgrn_kernel(*refs, tb, has_state):
    if has_state:
        q_ref, g_ref, i_ref, og_ref, gn_ref, tri_ref, s0_ref, y_ref, sfin_ref, st_sc, b_sc, o_sc = refs
    else:
        q_ref, g_ref, i_ref, og_ref, gn_ref, tri_ref, y_ref, sfin_ref, st_sc, b_sc, o_sc = refs
        s0_ref = None
    step = pl.program_id(1)
    nstep = pl.num_programs(1)

    @pl.when(step == 0)
    def _init():
        for h in range(HGRN_HEADS):
            if has_state:
                st_sc[h] = s0_ref[0, h].T
            else:
                st_sc[h] = jnp.zeros((HGRN_DK, HGRN_DK), F32)

    g = g_ref[...]
    tri = tri_ref[...]
    b = None
    for part in _split3(g):
        pb = jnp.dot(tri, part, preferred_element_type=F32)
        b = pb if b is None else b + pb
    b_sc[...] = b

    trow = lax.broadcasted_iota(I32, (SUB, 1), 0)

    def sub_chunk(c, carry):
        r0 = pl.multiple_of(c * SUB, SUB)
        for h in range(HGRN_HEADS):
            cs = slice(h * HGRN_DK, (h + 1) * HGRN_DK)
            q = q_ref[pl.ds(r0, SUB), cs]
            gg = g_ref[pl.ds(r0, SUB), cs]
            iv = i_ref[pl.ds(r0, SUB), cs]
            bb = b_sc[pl.ds(r0, SUB), cs]
            kk = 1.0 - jnp.exp(gg)
            st = st_sc[h]
            o = lax.dot_general((q * jnp.exp(bb)).astype(BF16), st.astype(BF16), NT_DIMS,
                                preferred_element_type=F32)
            for s in range(SUB):
                lo = 0 if s < 8 else 8
                ks = kk[s:s + 1, :]
                bs = bb[s:s + 1, :]
                ivs = iv[s:s + 1, :]
                e = jnp.exp(jnp.where(trow[lo:] >= s, bb[lo:] - bs, NEG_BIG))
                a = jnp.sum(q[lo:] * ks * e, axis=-1, keepdims=True)
                upd = a * ivs
                if lo:
                    upd = jnp.concatenate([jnp.zeros((8, HGRN_DK), F32), upd], axis=0)
                o = o + upd
            o_sc[pl.ds(r0, SUB), cs] = o
            bl = bb[SUB - 1:SUB, :]
            kd = kk * jnp.exp(bl - bb)
            u = lax.dot_general(iv.astype(BF16), kd.astype(BF16), TN_DIMS, preferred_element_type=F32)
            st_sc[h] = st * jnp.exp(bl) + u
        return carry

    lax.fori_loop(0, tb // SUB, sub_chunk, 0)

    for h in range(HGRN_HEADS):
        cs = slice(h * HGRN_DK, (h + 1) * HGRN_DK)
        o = o_sc[:, cs]
        ms = jnp.mean(o * o, axis=-1, keepdims=True)
        y_ref[:, cs] = (o * lax.rsqrt(ms + NORM_EPS) * gn_ref[...] * og_ref[:, cs]).astype(y_ref.dtype)

    @pl.when(step == nstep - 1)
    def _fin():
        for h in range(HGRN_HEADS):
            sfin_ref[0, h] = st_sc[h].T


def _hgrn(qs, gl, iv, og, gn, s0, nb, seq, tb):
    ns = seq // tb
    t = nb * seq
    row = pl.BlockSpec((tb, HGRN_WIDTH), lambda b, i: (b * ns + i, 0))
    r = jnp.arange(tb)
    tri = ((r[:, None] // SUB == r[None, :] // SUB) & (r[None, :] <= r[:, None])).astype(BF16)
    in_specs = [row, row, row, row, _const_spec((1, HGRN_DK)), _const_spec((tb, tb))]
    args = [qs, gl, iv, og, gn, tri]
    if s0 is not None:
        in_specs.append(pl.BlockSpec((1, HGRN_HEADS, HGRN_DK, HGRN_DK), lambda b, i: (b, 0, 0, 0)))
        args.append(s0)
    return pl.pallas_call(
        functools.partial(_hgrn_kernel, tb=tb, has_state=s0 is not None),
        grid=(nb, ns), in_specs=in_specs,
        out_specs=[row, pl.BlockSpec((1, HGRN_HEADS, HGRN_DK, HGRN_DK), lambda b, i: (b, 0, 0, 0))],
        out_shape=[jax.ShapeDtypeStruct((t, HGRN_WIDTH), BF16),
                   jax.ShapeDtypeStruct((nb, HGRN_HEADS, HGRN_DK, HGRN_DK), F32)],
        scratch_shapes=[pltpu.VMEM((HGRN_HEADS, HGRN_DK, HGRN_DK), F32),
                        pltpu.VMEM((tb, HGRN_WIDTH), F32), pltpu.VMEM((tb, HGRN_WIDTH), F32)],
        compiler_params=_cparams(("parallel", "arbitrary")), name="hgrn",
    )(*args)


def _post_kernel(x_ref, ya_ref, yb_ref, ga_ref, gb_ref, g1_ref, sh2_ref, sc2_ref, gn2_ref,
                 wpa_ref, wpb_ref, wo_ref, wrh_ref, wrl_ref, br_ref, tri_ref, cin_ref,
                 x1_ref, h2_ref, idx_ref, wt_ref, rank_ref, cout_ref, cnt_sc, *, tm):
    first = jnp.logical_and(pl.program_id(0) == 0, pl.program_id(1) == 0)

    @pl.when(first)
    def _init():
        cnt_sc[...] = cin_ref[...]

    merged = (ga_ref[...].astype(F32) * jnp.dot(ya_ref[...], wpa_ref[...], preferred_element_type=F32)
              + gb_ref[...].astype(F32) * jnp.dot(yb_ref[...], wpb_ref[...], preferred_element_type=F32))
    x1 = x_ref[...] + g1_ref[0] * jnp.dot(merged.astype(BF16), wo_ref[...], preferred_element_type=F32)
    x1_ref[...] = x1
    ms = jnp.mean(x1 * x1, axis=-1, keepdims=True)
    h2 = x1 * lax.rsqrt(ms + NORM_EPS) * gn2_ref[...]
    h2 = h2 * (1.0 + sc2_ref[0]) + sh2_ref[0]
    h2_ref[...] = h2

    hh = h2.astype(BF16)
    hl = (h2 - hh.astype(F32)).astype(BF16)
    wrh = wrh_ref[...]
    logits = (lax.dot_general(wrh, hh, NT_DIMS, preferred_element_type=F32)
              + lax.dot_general(wrh, hl, NT_DIMS, preferred_element_type=F32)
              + lax.dot_general(wrl_ref[...], hh, NT_DIMS, preferred_element_type=F32))
    scores = _sigmoid(logits)
    biased = scores + br_ref[...]

    b3 = biased.reshape(N_GROUPS, GROUP_SIZE, tm)
    it3 = lax.broadcasted_iota(I32, (N_GROUPS, GROUP_SIZE, tm), 1).astype(F32)
    m1 = jnp.max(b3, axis=1, keepdims=True)
    i1 = jnp.min(jnp.where(b3 == m1, it3, float(GROUP_SIZE)), axis=1, keepdims=True)
    m2 = jnp.max(jnp.where(it3 == i1, -jnp.inf, b3), axis=1, keepdims=True)
    gs = (m1 + m2).reshape(N_GROUPS, tm)

    gi = lax.broadcasted_iota(I32, (N_GROUPS, tm), 0)
    beat = jnp.zeros((N_GROUPS, tm), F32)
    for g in range(N_GROUPS):
        r = gs[g:g + 1, :]
        beat = beat + jnp.where((r > gs) | ((r == gs) & (g < gi)), 1.0, 0.0)
    gpen = jnp.where(beat < TOPK_GROUPS, 0.0, -jnp.inf)
    masked = (b3 + gpen.reshape(N_GROUPS, 1, tm)).reshape(N_EXPERTS, tm)

    ei = lax.broadcasted_iota(I32, (N_EXPERTS, tm), 0).astype(F32)
    idx_rows, w_rows = [], []
    chosen = jnp.zeros((N_EXPERTS, tm), F32)
    for _ in range(TOP_K):
        m = jnp.max(masked, axis=0, keepdims=True)
        ik = jnp.min(jnp.where(masked == m, ei, float(N_EXPERTS)), axis=0, keepdims=True)
        hit = ei == ik
        w_rows.append(jnp.sum(jnp.where(hit, scores, 0.0), axis=0, keepdims=True))
        idx_rows.append(ik)
        masked = jnp.where(hit, -jnp.inf, masked)
        chosen = jnp.where(hit, 1.0, chosen)
    idx_ref[...] = jnp.concatenate(idx_rows, axis=0).astype(I32)
    wts = jnp.concatenate(w_rows, axis=0)
    wt_ref[...] = wts / jnp.sum(wts, axis=0, keepdims=True) * ROUTED_SCALE

    before = cnt_sc[...] + jnp.dot(chosen.astype(BF16), tri_ref[...], preferred_element_type=F32)
    rank_rows = [jnp.sum(jnp.where(ei == idx_rows[kk], before, 0.0), axis=0, keepdims=True)
                 for kk in range(TOP_K)]
    rank_ref[...] = jnp.concatenate(rank_rows, axis=0).astype(I32)
    cnt_sc[...] = cnt_sc[...] + jnp.sum(chosen, axis=1, keepdims=True)
    cout_ref[...] = cnt_sc[...]


def _post(x2, ya, yb, ga, gb, gate1, shift2, scale2, prm, count_in, nb, tm):
    t, d = x2.shape
    ns = t // nb // tm
    row = lambda w: pl.BlockSpec((tm, w), lambda b, i: (b * ns + i, 0))
    colb = pl.BlockSpec((TOP_K, tm), lambda b, i: (0, b * ns + i))
    r = jnp.arange(tm)
    tri = (r[:, None] < r[None, :]).astype(BF16)
    in_specs = [row(d), row(FOX_WIDTH), row(HGRN_WIDTH), row(d), row(d),
                _mod_spec(gate1, tm, ns), _mod_spec(shift2, tm, ns), _mod_spec(scale2, tm, ns),
                _const_spec((1, d)), _const_spec((FOX_WIDTH, d)), _const_spec((HGRN_WIDTH, d)),
                _const_spec((d, d)), _const_spec((N_EXPERTS, d)), _const_spec((N_EXPERTS, d)),
                _const_spec((N_EXPERTS, 1)), _const_spec((tm, tm)), _const_spec((N_EXPERTS, 1))]
    out_shape = [jax.ShapeDtypeStruct((t, d), F32), jax.ShapeDtypeStruct((t, d), F32),
                 jax.ShapeDtypeStruct((TOP_K, t), I32), jax.ShapeDtypeStruct((TOP_K, t), F32),
                 jax.ShapeDtypeStruct((TOP_K, t), I32), jax.ShapeDtypeStruct((N_EXPERTS, 1), F32)]
    out_specs = [row(d), row(d), colb, colb, colb, pl.BlockSpec((N_EXPERTS, 1), lambda b, i: (0, 0))]
    return pl.pallas_call(
        functools.partial(_post_kernel, tm=tm), grid=(nb, ns), in_specs=in_specs,
        out_specs=out_specs, out_shape=out_shape,
        scratch_shapes=[pltpu.VMEM((N_EXPERTS, 1), F32)],
        compiler_params=_cparams(("arbitrary", "arbitrary")), name="post",
    )(x2, ya, yb, ga, gb, gate1, shift2, scale2, prm["g2"], prm["wpa"], prm["wpb"], prm["wo"],
      prm["wrh"], prm["wrl"], prm["br"], tri, count_in)


def _pos_kernel(idx_ref, rank_ref, pst_ref, pos_ref, *, tm):
    ei = lax.broadcasted_iota(I32, (N_EXPERTS, tm), 0)
    pst = pst_ref[...]
    rows = [jnp.sum(jnp.where(ei == idx_ref[kk:kk + 1, :], pst, 0.0), axis=0, keepdims=True)
            for kk in range(TOP_K)]
    pos_ref[...] = jnp.concatenate(rows, axis=0).astype(I32) + rank_ref[...]


def _pos(idx, rank, pstart_col, tm):
    t = idx.shape[1]
    colb = pl.BlockSpec((TOP_K, tm), lambda i: (0, i))
    return pl.pallas_call(
        functools.partial(_pos_kernel, tm=tm), grid=(t // tm,),
        in_specs=[colb, colb, _const_spec((N_EXPERTS, 1))], out_specs=colb,
        out_shape=jax.ShapeDtypeStruct((TOP_K, t), I32),
        compiler_params=_cparams(("parallel",)), name="moe_pos",
    )(idx, rank, pstart_col)


def _padfill_kernel(cnt_ref, pad_ref, pst_ref, xs_ref, zero_sc, sem):
    zero_sc[...] = jnp.zeros(zero_sc.shape, F32)
    sub = 8
    sizes = [1 << b for b in reversed(range(3, MOE_BLK.bit_length() - 1))]

    def chunks(e, act):
        n = pad_ref[e] - cnt_ref[e]
        row = pst_ref[e] + cnt_ref[e]
        head = jnp.minimum((-cnt_ref[e]) & (sub - 1), n)
        for j in range(sub - 1):
            @pl.when(j < head)
            def _():
                act(pltpu.make_async_copy(zero_sc.at[pl.ds(0, 1)], xs_ref.at[pl.ds(row + j, 1)], sem))

        row = pl.multiple_of(row + head, sub)
        rest = n - head
        for size in sizes:
            @pl.when((rest & size) != 0)
            def _():
                act(pltpu.make_async_copy(zero_sc.at[pl.ds(0, size)], xs_ref.at[pl.ds(row, size)], sem))

            row = pl.multiple_of(row + (rest & size), sub)

    def issue(e, c):
        chunks(e, lambda cp: cp.start())
        return c

    def drain(e, c):
        chunks(e, lambda cp: cp.wait())
        return c

    lax.fori_loop(0, N_EXPERTS, issue, 0)
    lax.fori_loop(0, N_EXPERTS, drain, 0)


def _padfill(counts, padded, pstart, n_rows, d):
    return pl.pallas_call(
        _padfill_kernel,
        grid_spec=pltpu.PrefetchScalarGridSpec(
            num_scalar_prefetch=3, grid=(1,), in_specs=[],
            out_specs=pl.BlockSpec(memory_space=pl.ANY),
            scratch_shapes=[pltpu.VMEM((MOE_BLK // 2, d), F32), pltpu.SemaphoreType.DMA(())]),
        out_shape=jax.ShapeDtypeStruct((n_rows, d), F32),
        compiler_params=pltpu.CompilerParams(dimension_semantics=("arbitrary",), has_side_effects=True),
        name="moe_padfill",
    )(counts, padded, pstart)


def _dispatch_kernel(pos_ref, h2_ref, xs_in_ref, xs_ref, sem, *, tm):
    del xs_in_ref

    for t in range(tm):
        for kk in range(TOP_K):
            pltpu.make_async_copy(h2_ref.at[pl.ds(t, 1)], xs_ref.at[pl.ds(pos_ref[kk, t], 1)],
                                  sem).start(priority=kk % 2)
    for kk in range(TOP_K):
        pltpu.make_async_copy(h2_ref, xs_ref.at[pl.ds(0, tm)], sem).wait()


def _dispatch(pos, h2, xs, tm):
    t, d = h2.shape
    return pl.pallas_call(
        functools.partial(_dispatch_kernel, tm=tm),
        grid=(t // tm,),
        in_specs=[pl.BlockSpec((TOP_K, tm), lambda i: (0, i), memory_space=pltpu.SMEM),
                  pl.BlockSpec((tm, d), lambda i: (i, 0)),
                  pl.BlockSpec(memory_space=pl.ANY)],
        out_specs=pl.BlockSpec(memory_space=pl.ANY),
        out_shape=jax.ShapeDtypeStruct(xs.shape, F32),
        scratch_shapes=[pltpu.SemaphoreType.DMA(())],
        input_output_aliases={2: 0},
        compiler_params=pltpu.CompilerParams(dimension_semantics=("arbitrary",), has_side_effects=True,
                                             vmem_limit_bytes=VMEM_LIMIT),
        name="moe_dispatch",
    )(pos, h2, xs)


def _moe_kernel(be_ref, nu_ref, first_ref, slot_ref, nxt_ref, x_ref, wg_ref, wu_ref, wd_ref, y_ref,
                wg_buf, wu_buf, wd_buf, wgu_sc, wd_sc, sem):
    i = pl.program_id(0)
    e = be_ref[i]

    def fetch(expert, slot):
        return (pltpu.make_async_copy(wg_ref.at[expert], wg_buf.at[slot], sem.at[slot]),
                pltpu.make_async_copy(wu_ref.at[expert], wu_buf.at[slot], sem.at[slot]),
                pltpu.make_async_copy(wd_ref.at[expert], wd_buf.at[slot], sem.at[slot]))

    @pl.when(first_ref[i] == 1)
    def _switch_expert():
        slot = slot_ref[i]

        @pl.when(i == 0)
        def _():
            for cp in fetch(e, slot):
                cp.start()

        for cp in fetch(e, slot):
            cp.wait()
        nxt = nxt_ref[e]

        @pl.when(nxt >= 0)
        def _():
            for cp in fetch(nxt, 1 - slot):
                cp.start()

        wgu_sc[:, 0:D_EXPERT] = wg_buf[slot].astype(BF16)
        wgu_sc[:, D_EXPERT:2 * D_EXPERT] = wu_buf[slot].astype(BF16)
        wd_sc[...] = wd_buf[slot].astype(BF16)

    @pl.when(i < nu_ref[0])
    def _compute():
        gu = jnp.dot(x_ref[...].astype(BF16), wgu_sc[...], preferred_element_type=F32)
        gt = gu[:, 0:D_EXPERT]
        h = gt * _sigmoid(gt) * gu[:, D_EXPERT:2 * D_EXPERT]
        y_ref[...] = jnp.dot(h.astype(BF16), wd_sc[...], preferred_element_type=F32)


def _moe(blk_expert, n_used, first, slot, nxt, xs, w_gate, w_up, w_down):
    n_rows, d = xs.shape
    nblk = n_rows // MOE_BLK
    xmap = lambda i, be, nu, fi, sl, nx: (jnp.minimum(i, nu[0] - 1), 0)
    hbm = pl.BlockSpec(memory_space=pl.ANY)
    return pl.pallas_call(
        _moe_kernel,
        grid_spec=pltpu.PrefetchScalarGridSpec(
            num_scalar_prefetch=5, grid=(nblk,),
            in_specs=[pl.BlockSpec((MOE_BLK, d), xmap), hbm, hbm, hbm],
            out_specs=pl.BlockSpec((MOE_BLK, d), xmap),
            scratch_shapes=[pltpu.VMEM((2, d, D_EXPERT), F32), pltpu.VMEM((2, d, D_EXPERT), F32),
                            pltpu.VMEM((2, D_EXPERT, d), F32),
                            pltpu.VMEM((d, 2 * D_EXPERT), BF16), pltpu.VMEM((D_EXPERT, d), BF16),
                            pltpu.SemaphoreType.DMA((2,))]),
        out_shape=jax.ShapeDtypeStruct((n_rows, d), F32),
        compiler_params=_cparams(("arbitrary",)), name="moe_experts",
    )(blk_expert, n_used, first, slot, nxt, xs, w_gate, w_up, w_down)


def _combine_kernel(pos_ref, posn_ref, x1_ref, h2_ref, g2_ref, wt_ref, wsgu_ref, wsd_ref, ys_ref, o_ref,
                    buf, sem, *, tm):
    lin = pl.program_id(0) * pl.num_programs(1) + pl.program_id(1)
    total = pl.num_programs(0) * pl.num_programs(1)
    slot = lin % 2

    def issue(p_ref, s):
        for t in range(tm):
            for kk in range(TOP_K):
                pltpu.make_async_copy(ys_ref.at[pl.ds(p_ref[kk, t], 1)], buf.at[s, kk, pl.ds(t, 1)],
                                      sem.at[s]).start(priority=kk % 2)

    @pl.when(lin == 0)
    def _():
        issue(pos_ref, 0)

    @pl.when(lin + 1 < total)
    def _():
        issue(posn_ref, 1 - slot)

    gu = jnp.dot(h2_ref[...].astype(BF16), wsgu_ref[...], preferred_element_type=F32)
    gt = gu[:, 0:D_SHARED]
    hs = gt * _sigmoid(gt) * gu[:, D_SHARED:2 * D_SHARED]
    acc = jnp.dot(hs.astype(BF16), wsd_ref[...], preferred_element_type=F32)

    for kk in range(TOP_K):
        pltpu.make_async_copy(ys_ref.at[pl.ds(0, tm)], buf.at[slot, kk], sem.at[slot]).wait()

    d = acc.shape[-1]
    for kk in range(TOP_K):
        wrow = jnp.broadcast_to(wt_ref[kk:kk + 1, :], (LANES, tm))
        wcol = wrow.T
        acc = acc + buf[slot, kk] * jnp.concatenate([wcol] * (d // LANES), axis=1)
    o_ref[...] = x1_ref[...] + g2_ref[0] * acc


def _combine(pos, x1, h2, gate2, wts, wsgu, wsd, ys, nb, tm):
    t, d = x1.shape
    ns = t // nb // tm
    row = pl.BlockSpec((tm, d), lambda b, i: (b * ns + i, 0))
    last = nb * ns - 1
    return pl.pallas_call(
        functools.partial(_combine_kernel, tm=tm),
        grid=(nb, ns),
        in_specs=[pl.BlockSpec((TOP_K, tm), lambda b, i: (0, b * ns + i), memory_space=pltpu.SMEM),
                  pl.BlockSpec((TOP_K, tm), lambda b, i: (0, jnp.minimum(b * ns + i + 1, last)),
                               memory_space=pltpu.SMEM),
                  row, row, _mod_spec(gate2, tm, ns),
                  pl.BlockSpec((TOP_K, tm), lambda b, i: (0, b * ns + i)),
                  _const_spec(wsgu.shape), _const_spec(wsd.shape),
                  pl.BlockSpec(memory_space=pl.ANY)],
        out_specs=row,
        out_shape=jax.ShapeDtypeStruct((t, d), F32),
        scratch_shapes=[pltpu.VMEM((2, TOP_K, tm, d), F32), pltpu.SemaphoreType.DMA((2,))],
        compiler_params=_cparams(("arbitrary", "arbitrary")), name="moe_combine",
    )(pos, pos, x1, h2, gate2, wts, wsgu, wsd, ys)


def _prepare_params(g_norm1, w_in, b_fox_f, g_q, g_k, hgrn_lb, g_hgrn_o, w_proj_a, w_proj_b, w_out,
                    g_norm2, w_router, b_router, w_sh_gate, w_sh_up, w_sh_down):
    d = D_MODEL
    fw, hw = FOX_WIDTH, HGRN_WIDTH
    c0 = 3 * fw
    c1 = c0 + FOX_HEADS
    c2 = c1 + 4 * hw
    wff = w_in[:, c0:c1]
    head = jnp.arange(fw) // FOX_HEAD_DIM
    wr_t = w_router.T
    wrh = wr_t.astype(BF16)
    return dict(
        g1=g_norm1.reshape(1, d),
        wqkv=w_in[:, :c0].astype(BF16),
        wff=jnp.pad(wff, ((0, 0), (0, LANES - FOX_HEADS))).astype(BF16),
        wfft=jnp.pad(wff.T, ((0, 16 - FOX_HEADS), (0, 0))).astype(BF16),
        bf=b_fox_f.reshape(1, FOX_HEADS), bft=b_fox_f.reshape(FOX_HEADS, 1),
        gq=jnp.tile(g_q, FOX_HEADS).reshape(1, fw), gk=jnp.tile(g_k, FOX_HEADS).reshape(1, fw),
        bd=(head[:, None] == head[None, :]).astype(BF16) * (1.0 / FOX_HEAD_DIM),
        lbp=hgrn_lb,
        wh=w_in[:, c1:c2].astype(BF16), wg=w_in[:, c2:].astype(BF16),
        gn=g_hgrn_o.reshape(1, HGRN_DK),
        g2=g_norm2.reshape(1, d),
        wpa=w_proj_a.astype(BF16), wpb=w_proj_b.astype(BF16), wo=w_out.astype(BF16),
        wrh=wrh, wrl=(wr_t - wrh.astype(F32)).astype(BF16), br=b_router.reshape(N_EXPERTS, 1),
        wsgu=jnp.concatenate([w_sh_gate, w_sh_up], axis=1).astype(BF16), wsd=w_sh_down.astype(BF16),
    )


def _mixers(x2, nb, seq, tm, shift1, scale1, prm, past, tile, tb):
    (q, k, kb, v, vb, lf, lft, hq, hl, hi, og, ga, gb) = _inproj(x2, nb if shift1.shape[1] == 1 else 1, tm,
                                                                 shift1, scale1, prm)
    t = nb * seq
    hp = FOX_HEADS // 2
    lft = lft.transpose(1, 0, 2).reshape(FOX_HEADS, nb, seq).transpose(1, 0, 2)
    if past is None:
        fcum = _cumsum_lanes(lft.reshape(nb * FOX_HEADS, seq)).reshape(nb, hp, 2, seq)
        ya = _fox_prompt(q, kb, vb, fcum, nb, seq, tile, min(32, tile))
        s0 = None
    else:
        cache_k, cache_v, cache_lf, s0 = past
        plen = cache_k.shape[1]
        ltot = plen + seq
        lpad = -(-ltot // LANES) * LANES
        lf_all = jnp.concatenate([cache_lf.transpose(0, 2, 1), lft,
                                  jnp.zeros((nb, FOX_HEADS, lpad - ltot), F32)], axis=-1)
        fcum = _cumsum_lanes(lf_all.reshape(nb * FOX_HEADS, lpad)).reshape(nb, hp, 2, lpad)
        ya = _fox_sample(q, kb, vb, cache_k.reshape(nb, plen, FOX_WIDTH), cache_v.reshape(nb, plen, FOX_WIDTH),
                         fcum, nb, seq, plen)
    yb, sfin = _hgrn(hq, hl, hi, og, prm["gn"], s0, nb, seq, tb)
    return k, v, lf, sfin, ya, yb, ga, gb


def kernel(x_prompt, x_sample, cache_fox_k, cache_fox_v, cache_fox_logf, state_hgrn, c_prompt, c_sample,
           w_ada, b_ada, g_norm1, w_in, b_fox_f, g_q, g_k, hgrn_lb, g_hgrn_o, w_proj_a, w_proj_b, w_out,
           g_norm2, w_router, b_router, w_exp_gate, w_exp_up, w_exp_down, w_sh_gate, w_sh_up, w_sh_down):
    assert w_ada.shape[0] == 1 and hgrn_lb.shape[0] == 2, "single-layer trunk"
    d = D_MODEL
    bp, sp, _ = x_prompt.shape
    bs, ss, _ = x_sample.shape
    tp, ts = bp * sp, bs * ss
    prm = _prepare_params(g_norm1[0], w_in[0], b_fox_f[0], g_q[0], g_k[0], hgrn_lb, g_hgrn_o[0],
                          w_proj_a[0], w_proj_b[0], w_out[0], g_norm2[0], w_router[0], b_router[0],
                          w_sh_gate[0], w_sh_up[0], w_sh_down[0])

    bc = bp + bs
    bc_pad = -(-bc // 8) * 8
    c_all = jnp.concatenate([c_prompt, c_sample, jnp.zeros((bc_pad - bc, d), F32)], axis=0)
    mod = _ada(c_all, w_ada[0], b_ada[0])
    mod_p = [mod[:bp, j * d:(j + 1) * d].reshape(bp, 1, d) for j in range(6)]
    mod_s = [jnp.repeat(mod[bp:bc, j * d:(j + 1) * d], ss, axis=0).reshape(1, ts, d) for j in range(6)]

    tm_p = min(512, sp)
    tm_s = min(512, ts)
    xp2 = x_prompt.reshape(tp, d)
    xs2 = x_sample.reshape(ts, d)
    tile = min(1024, sp)
    tb = min(256, sp)

    kp, vp, lfp, sfin_p, ya_p, yb_p, ga_p, gb_p = _mixers(
        xp2, bp, sp, tm_p, mod_p[0], mod_p[1], prm, None, tile, tb)
    past = (cache_fox_k[0], cache_fox_v[0], cache_fox_logf[0], state_hgrn[0])
    ks, vs, lfs, sfin_s, ya_s, yb_s, ga_s, gb_s = _mixers(
        xs2, bs, ss, tm_s, mod_s[0], mod_s[1], prm, past, None, min(256, ss))

    zero_cnt = jnp.zeros((N_EXPERTS, 1), F32)
    x1_p, h2_p, idx_p, wt_p, rank_p, cnt_p = _post(xp2, ya_p, yb_p, ga_p, gb_p, mod_p[2], mod_p[3], mod_p[4],
                                                    prm, zero_cnt, bp, tm_p)
    x1_s, h2_s, idx_s, wt_s, rank_s, cnt_all = _post(xs2, ya_s, yb_s, ga_s, gb_s, mod_s[2], mod_s[3], mod_s[4],
                                                     prm, cnt_p, 1, tm_s)

    counts = cnt_all.reshape(N_EXPERTS).astype(I32)
    padded = (counts + MOE_BLK - 1) // MOE_BLK * MOE_BLK
    pend = jnp.cumsum(padded)
    pstart = pend - padded
    nblk = -(-((tp + ts) * TOP_K + N_EXPERTS * (MOE_BLK - 1)) // MOE_BLK)
    n_used = (pend[-1] // MOE_BLK).reshape(1)
    blk_row0 = jnp.arange(nblk, dtype=I32) * MOE_BLK
    blk_expert = jnp.minimum(jnp.sum((pend[None, :] <= blk_row0[:, None]).astype(I32), axis=1), N_EXPERTS - 1)
    blk_used = jnp.arange(nblk) < n_used[0]
    blk_expert = jnp.where(blk_used, blk_expert, blk_expert[jnp.maximum(n_used[0] - 1, 0)])
    blk_first = blk_used & jnp.concatenate([jnp.ones((1,), bool), blk_expert[1:] != blk_expert[:-1]])
    blk_slot = ((jnp.cumsum(blk_first.astype(I32)) - 1) & 1).astype(I32)
    eids = jnp.arange(N_EXPERTS, dtype=I32)
    later = (eids[None, :] > eids[:, None]) & (padded[None, :] > 0)
    nxt_expert = jnp.min(jnp.where(later, eids[None, :], N_EXPERTS), axis=1)
    nxt_expert = jnp.where(nxt_expert < N_EXPERTS, nxt_expert, -1).astype(I32)
    pstart_col = pstart.astype(F32).reshape(N_EXPERTS, 1)
    pos_p = _pos(idx_p, rank_p, pstart_col, tm_p)
    pos_s = _pos(idx_s, rank_s, pstart_col, tm_s)

    xs_pool = _padfill(counts, padded, pstart, nblk * MOE_BLK, d)
    xs_pool = _dispatch(pos_p, h2_p, xs_pool, min(128, tm_p))
    xs_pool = _dispatch(pos_s, h2_s, xs_pool, min(128, tm_s))
    ys_pool = _moe(blk_expert, n_used, blk_first.astype(I32), blk_slot, nxt_expert, xs_pool,
                   w_exp_gate[0], w_exp_up[0], w_exp_down[0])

    tm_c = min(128, sp)
    y_p = _combine(pos_p, x1_p, h2_p, mod_p[5], wt_p, prm["wsgu"], prm["wsd"], ys_pool, bp, tm_c)
    y_s = _combine(pos_s, x1_s, h2_s, mod_s[5], wt_s, prm["wsgu"], prm["wsd"], ys_pool, 1, min(128, ts))

    return (y_p.reshape(bp, sp, d), y_s.reshape(bs, ss, d),
            kp.reshape(1, bp, sp, FOX_HEADS, FOX_HEAD_DIM), vp.reshape(1, bp, sp, FOX_HEADS, FOX_HEAD_DIM),
            lfp.reshape(1, bp, sp, FOX_HEADS), sfin_p[None],
            ks.reshape(1, bs, ss, FOX_HEADS, FOX_HEAD_DIM), vs.reshape(1, bs, ss, FOX_HEADS, FOX_HEAD_DIM),
            lfs.reshape(1, bs, ss, FOX_HEADS), sfin_s[None])
```

```python
import functools

import jax
import jax.numpy as jnp
from jax import lax
from jax.experimental import pallas as pl
from jax.experimental.pallas import tpu as pltpu

F32 = jnp.float32
BF16 = jnp.bfloat16
I32 = jnp.int32

D_MODEL = 1024
FOX_HEADS = 8
FOX_HEAD_DIM = 64
FOX_WIDTH = FOX_HEADS * FOX_HEAD_DIM
HGRN_HEADS = 4
HGRN_DK = 128
HGRN_WIDTH = HGRN_HEADS * HGRN_DK
N_EXPERTS = 256
TOP_K = 8
N_GROUPS = 8
TOPK_GROUPS = 4
GROUP_SIZE = N_EXPERTS // N_GROUPS
D_EXPERT = 256
D_SHARED = 256
ROUTED_SCALE = 2.5
NORM_EPS = 1e-6
NEG_BIG = -1e30
LOG2E = 1.4426950408889634
QK_SCALE = FOX_HEAD_DIM ** -0.5 * LOG2E

LANES = 128
SUB = 16
MOE_BLK = 512
VMEM_LIMIT = 56 * 1024 * 1024

NT_DIMS = (((1,), (1,)), ((), ()))
TN_DIMS = (((0,), (0,)), ((), ()))


def _cparams(sem):
    return pltpu.CompilerParams(dimension_semantics=sem, vmem_limit_bytes=VMEM_LIMIT)


def _const_spec(shape):
    nd = len(shape)
    return pl.BlockSpec(shape, lambda *_: (0,) * nd, pipeline_mode=pl.Buffered(1))


def _sigmoid(z):
    return 1.0 / (1.0 + jnp.exp(-z))


def _log_sigmoid(z):
    return jnp.minimum(z, 0.0) - jnp.log(1.0 + jnp.exp(-jnp.abs(z)))


ROW_CHUNKS = D_MODEL // LANES


def _row_chunk(ref, c):
    return ref.at[pl.ds(c, ref.shape[0] // ROW_CHUNKS, stride=ROW_CHUNKS), :]


def _store_rows(ref, val):
    for c in range(ROW_CHUNKS):
        _row_chunk(ref, c)[...] = val[:, c * LANES:(c + 1) * LANES]


def _load_rows(ref):
    return jnp.concatenate([_row_chunk(ref, c)[...] for c in range(ROW_CHUNKS)], axis=1)


def _split3(a):
    hi = a.astype(BF16)
    r1 = a - hi.astype(F32)
    mid = r1.astype(BF16)
    lo = (r1 - mid.astype(F32)).astype(BF16)
    return hi, mid, lo


def _ada_kernel(c_ref, w_ref, b_ref, o_ref):
    c = c_ref[...]
    s = c * _sigmoid(c)
    o_ref[...] = jnp.dot(s.astype(BF16), w_ref[...].astype(BF16), preferred_element_type=F32) + b_ref[...]


def _ada(c_all, w_ada, b_ada):
    bc, d = c_all.shape
    n = w_ada.shape[1]
    tn = 1024
    return pl.pallas_call(
        _ada_kernel,
        grid=(n // tn,),
        in_specs=[pl.BlockSpec((bc, d), lambda j: (0, 0)),
                  pl.BlockSpec((d, tn), lambda j: (0, j)),
                  pl.BlockSpec((1, tn), lambda j: (0, j))],
        out_specs=pl.BlockSpec((bc, tn), lambda j: (0, j)),
        out_shape=jax.ShapeDtypeStruct((bc, n), F32),
        compiler_params=_cparams(("parallel",)),
        name="ada",
    )(c_all, w_ada, b_ada.reshape(1, n))


def _inproj_kernel(x_ref, sh_ref, sc_ref, g1_ref, wqkv_ref, wff_ref, wfft_ref, bf_ref, bft_ref,
                   gq_ref, gk_ref, bd_ref, lbp_ref, wh_ref, wg_ref,
                   q_ref, k_ref, kb_ref, v_ref, vb_ref, lf_ref, lft_ref,
                   hq_ref, hl_ref, hi_ref, og_ref, ga_ref, gb_ref):
    x = x_ref[...]
    ms = jnp.mean(x * x, axis=-1, keepdims=True)
    h = x * lax.rsqrt(ms + NORM_EPS) * g1_ref[...]
    h = h * (1.0 + sc_ref[0]) + sh_ref[0]
    hb = h.astype(BF16)

    def headnorm(a, g):
        ss = jnp.dot((a * a).astype(BF16), bd_ref[...], preferred_element_type=F32)
        return a * lax.rsqrt(ss + NORM_EPS) * g

    fq = jnp.dot(hb, wqkv_ref[:, 0:FOX_WIDTH], preferred_element_type=F32)
    q_ref[...] = (headnorm(fq, gq_ref[...]) * QK_SCALE).astype(BF16)
    fk = jnp.dot(hb, wqkv_ref[:, FOX_WIDTH:2 * FOX_WIDTH], preferred_element_type=F32)
    k = headnorm(fk, gk_ref[...])
    kb_ref[...] = k.astype(BF16)
    fv = jnp.dot(hb, wqkv_ref[:, 2 * FOX_WIDTH:3 * FOX_WIDTH], preferred_element_type=F32)
    vb_ref[...] = fv.astype(BF16)
    for hd in range(FOX_HEADS):
        cols = slice(hd * FOX_HEAD_DIM, (hd + 1) * FOX_HEAD_DIM)
        k_ref[:, hd, :] = k[:, cols]
        v_ref[:, hd, :] = fv[:, cols]

    ff = jnp.dot(hb, wff_ref[...], preferred_element_type=F32)
    lf_ref[...] = _log_sigmoid(ff[:, 0:FOX_HEADS] + bf_ref[...])
    fft = lax.dot_general(wfft_ref[...], hb, NT_DIMS, preferred_element_type=F32)
    lft_ref[0] = _log_sigmoid(fft[0:FOX_HEADS, :] + bft_ref[...])

    lbp = lbp_ref[...]
    e = jnp.exp(lbp - jnp.max(lbp, axis=0, keepdims=True))
    lb = e[0:1, :] / jnp.sum(e, axis=0, keepdims=True)

    w = HGRN_WIDTH
    hq = jnp.dot(hb, wh_ref[:, 0:w], preferred_element_type=F32)
    hq_ref[...] = hq * _sigmoid(hq)
    hf = jnp.dot(hb, wh_ref[:, w:2 * w], preferred_element_type=F32)
    hl_ref[...] = jnp.log(lb + (1.0 - lb) * _sigmoid(hf))
    hi_ref[...] = jnp.dot(hb, wh_ref[:, 2 * w:3 * w], preferred_element_type=F32)
    hg = jnp.dot(hb, wh_ref[:, 3 * w:4 * w], preferred_element_type=F32)
    og_ref[...] = _sigmoid(hg)
    ga = jnp.dot(hb, wg_ref[:, 0:D_MODEL], preferred_element_type=F32)
    ga_ref[...] = _sigmoid(ga).astype(BF16)
    gb = jnp.dot(hb, wg_ref[:, D_MODEL:2 * D_MODEL], preferred_element_type=F32)
    gb_ref[...] = _sigmoid(gb).astype(BF16)


def _mod_spec(arr, tm, ns):
    d = arr.shape[-1]
    if arr.shape[1] == 1:
        return pl.BlockSpec((1, 1, d), lambda b, i: (b, 0, 0))
    return pl.BlockSpec((1, tm, d), lambda b, i: (0, b * ns + i, 0))


def _inproj(x2, nb, tm, shift1, scale1, prm):
    t, d = x2.shape
    ns = t // nb // tm
    row = lambda w: pl.BlockSpec((tm, w), lambda b, i: (b * ns + i, 0))
    in_specs = [row(d), _mod_spec(shift1, tm, ns), _mod_spec(scale1, tm, ns),
                _const_spec((1, d)), _const_spec(prm["wqkv"].shape), _const_spec(prm["wff"].shape),
                _const_spec(prm["wfft"].shape), _const_spec((1, FOX_HEADS)), _const_spec((FOX_HEADS, 1)),
                _const_spec((1, FOX_WIDTH)), _const_spec((1, FOX_WIDTH)), _const_spec((FOX_WIDTH, FOX_WIDTH)),
                _const_spec(prm["lbp"].shape), _const_spec(prm["wh"].shape), _const_spec(prm["wg"].shape)]
    fw, hw = FOX_WIDTH, HGRN_WIDTH
    heads = pl.BlockSpec((tm, FOX_HEADS, FOX_HEAD_DIM), lambda b, i: (b * ns + i, 0, 0))
    out_shape = [jax.ShapeDtypeStruct((t, fw), BF16),
                 jax.ShapeDtypeStruct((t, FOX_HEADS, FOX_HEAD_DIM), F32),
                 jax.ShapeDtypeStruct((t, fw), BF16),
                 jax.ShapeDtypeStruct((t, FOX_HEADS, FOX_HEAD_DIM), F32),
                 jax.ShapeDtypeStruct((t, fw), BF16),
                 jax.ShapeDtypeStruct((t, FOX_HEADS), F32),
                 jax.ShapeDtypeStruct((nb * ns, FOX_HEADS, tm), F32),
                 jax.ShapeDtypeStruct((t, hw), F32),
                 jax.ShapeDtypeStruct((t, hw), F32),
                 jax.ShapeDtypeStruct((t, hw), F32),
                 jax.ShapeDtypeStruct((t, hw), F32),
                 jax.ShapeDtypeStruct((t, d), BF16),
                 jax.ShapeDtypeStruct((t, d), BF16)]
    out_specs = [row(fw), heads, row(fw), heads, row(fw), row(FOX_HEADS),
                 pl.BlockSpec((1, FOX_HEADS, tm), lambda b, i: (b * ns + i, 0, 0)),
                 row(hw), row(hw), row(hw), row(hw), row(d), row(d)]
    return pl.pallas_call(
        _inproj_kernel, grid=(nb, ns), in_specs=in_specs, out_specs=out_specs, out_shape=out_shape,
        compiler_params=_cparams(("parallel", "parallel")), name="inproj",
    )(x2, shift1, scale1, prm["g1"], prm["wqkv"], prm["wff"], prm["wfft"], prm["bf"], prm["bft"],
      prm["gq"], prm["gk"], prm["bd"], prm["lbp"], prm["wh"], prm["wg"])


def _cumsum_kernel(x_ref, o_ref):
    x = x_ref[...]
    n = x.shape[-1]
    lane = lax.broadcasted_iota(I32, x.shape, 1)
    s = 1
    while s < n:
        x = x + jnp.where(lane >= s, pltpu.roll(x, s, axis=1), 0.0)
        s *= 2
    o_ref[...] = x * LOG2E


def _cumsum_lanes(x):
    return pl.pallas_call(
        _cumsum_kernel, out_shape=jax.ShapeDtypeStruct(x.shape, F32),
        compiler_params=pltpu.CompilerParams(vmem_limit_bytes=VMEM_LIMIT), name="cumsum",
    )(x)


def _fox_prompt_kernel(qt_ref, kt_ref, q_ref, k_ref, v_ref, fk_ref, o_ref,
                       m0, m1, l0, l1, a0, a1, s0, s1, p0, p1, *, tile, rsub):
    step = pl.program_id(2)
    qi = qt_ref[step]
    ki = kt_ref[step]
    heads = ((m0, l0, a0, s0, p0), (m1, l1, a1, s1, p1))
    lane = lax.broadcasted_iota(I32, (1, LANES), 1)
    lo_half = lane < FOX_HEAD_DIM

    @pl.when(ki == 0)
    def _init():
        for m_sc, l_sc, a_sc, _, _ in heads:
            m_sc[...] = jnp.full(m_sc.shape, NEG_BIG, F32)
            l_sc[...] = jnp.zeros(l_sc.shape, F32)
            a_sc[...] = jnp.zeros(a_sc.shape, F32)

    def tile_update(diagonal):
        q = q_ref[...]
        k = k_ref[...]
        v = v_ref[...]
        fk = fk_ref[0, 0]
        nsub = tile // rsub
        if diagonal:
            ahead = (lax.broadcasted_iota(I32, (rsub, tile), 1)
                     - lax.broadcasted_iota(I32, (rsub, tile), 0))
        for hs, (m_sc, l_sc, a_sc, s_sc, p_sc) in enumerate(heads):
            sel = lo_half if hs == 0 else jnp.logical_not(lo_half)
            qh = jnp.where(sel, q, jnp.zeros_like(q))
            s_sc[...] = lax.dot_general(qh, k, NT_DIMS, preferred_element_type=F32)

            def biased(r):
                rows = slice(r * rsub, (r + 1) * rsub)
                cw = min(tile, -(-((r + 1) * rsub) // LANES) * LANES) if diagonal else tile
                s = s_sc[rows, 0:cw] - fk[hs:hs + 1, 0:cw]
                if diagonal:
                    s = jnp.where(ahead[:, 0:cw] <= r * rsub, s, NEG_BIG)
                return s, rows, cw

            m_old = m_sc[...]
            m_parts = []
            for r in range(nsub):
                s, rows, _ = biased(r)
                m_parts.append(jnp.maximum(m_old[rows], jnp.max(s, axis=1, keepdims=True)))
            l_parts = []
            for r in range(nsub):
                s, rows, cw = biased(r)
                nc = cw // LANES
                p = jnp.exp2(s - jnp.concatenate([m_parts[r]] * nc, axis=1))
                part = p[:, 0:LANES]
                for c in range(1, nc):
                    part = part + p[:, c * LANES:(c + 1) * LANES]
                l_parts.append(part)
                p_sc[rows, 0:cw] = p.astype(BF16)
                if cw < tile:
                    p_sc[rows, cw:tile] = jnp.zeros((rsub, tile - cw), BF16)
            m_new = jnp.concatenate(m_parts, axis=0)
            alpha = jnp.exp2(m_old - m_new)
            m_sc[...] = m_new
            l_sc[...] = alpha * l_sc[...] + jnp.concatenate(l_parts, axis=0)
            a_sc[...] = alpha * a_sc[...] + jnp.dot(p_sc[...], v, preferred_element_type=F32)

    @pl.when(ki < qi)
    def _full():
        tile_update(False)

    @pl.when(ki == qi)
    def _diag():
        tile_update(True)
        o0 = a0[...] / jnp.sum(l0[...], axis=1, keepdims=True)
        o1 = a1[...] / jnp.sum(l1[...], axis=1, keepdims=True)
        o_ref[...] = jnp.where(lo_half, o0, o1).astype(o_ref.dtype)


def _fox_prompt(q, kb, vb, fcum, nb, seq, tile, rsub):
    nt = seq // tile
    hp = FOX_HEADS // 2
    qt = jnp.asarray([qi for qi in range(nt) for _ in range(qi + 1)], I32)
    kt = jnp.asarray([ki for qi in range(nt) for ki in range(qi + 1)], I32)
    qmap = lambda b, h, s, qt, kt: (b * nt + qt[s], h)
    kmap = lambda b, h, s, qt, kt: (b * nt + kt[s], h)
    fmap = lambda b, h, s, qt, kt: (b, h, 0, kt[s])
    stat = pltpu.VMEM((tile, LANES), F32)
    return pl.pallas_call(
        functools.partial(_fox_prompt_kernel, tile=tile, rsub=rsub),
        grid_spec=pltpu.PrefetchScalarGridSpec(
            num_scalar_prefetch=2, grid=(nb, hp, nt * (nt + 1) // 2),
            in_specs=[pl.BlockSpec((tile, LANES), qmap), pl.BlockSpec((tile, LANES), kmap),
                      pl.BlockSpec((tile, LANES), kmap), pl.BlockSpec((1, 1, 2, tile), fmap)],
            out_specs=pl.BlockSpec((tile, LANES), qmap),
            scratch_shapes=[stat, stat, stat, stat, stat, stat,
                            pltpu.VMEM((tile, tile), F32), pltpu.VMEM((tile, tile), F32),
                            pltpu.VMEM((tile, tile), BF16), pltpu.VMEM((tile, tile), BF16)]),
        out_shape=jax.ShapeDtypeStruct((nb * seq, FOX_WIDTH), BF16),
        compiler_params=_cparams(("parallel", "parallel", "arbitrary")),
        name="fox_prompt",
    )(qt, kt, q, kb, vb, fcum)


def _fox_sample_kernel(q_ref, kn_ref, vn_ref, kp_ref, vp_ref, fk_ref, o_ref, *, past, seq):
    lane = lax.broadcasted_iota(I32, (1, LANES), 1)
    lo_half = lane < FOX_HEAD_DIM
    q = q_ref[...]
    kn = kn_ref[...]
    vn = vn_ref[...]
    kp = kp_ref[0].astype(BF16)
    vp = vp_ref[0].astype(BF16)
    fk = fk_ref[0, 0]
    row = lax.broadcasted_iota(I32, (seq, seq), 0)
    col = lax.broadcasted_iota(I32, (seq, seq), 1)
    causal = col <= row
    outs = []
    for hs in range(2):
        sel = lo_half if hs == 0 else jnp.logical_not(lo_half)
        qh = jnp.where(sel, q, jnp.zeros_like(q))
        sp = lax.dot_general(qh, kp, NT_DIMS, preferred_element_type=F32) - fk[hs:hs + 1, 0:past]
        sn = lax.dot_general(qh, kn, NT_DIMS, preferred_element_type=F32) - fk[hs:hs + 1, past:past + seq]
        sn = jnp.where(causal, sn, NEG_BIG)
        m = jnp.maximum(jnp.max(sp, axis=1, keepdims=True), jnp.max(sn, axis=1, keepdims=True))
        pp = jnp.exp2(sp - m)
        pn = jnp.exp2(sn - m)
        l = jnp.sum(pp, axis=1, keepdims=True) + jnp.sum(pn, axis=1, keepdims=True)
        o = (jnp.dot(pp.astype(BF16), vp, preferred_element_type=F32)
             + jnp.dot(pn.astype(BF16), vn, preferred_element_type=F32))
        outs.append(o / l)
    o_ref[...] = jnp.where(lo_half, outs[0], outs[1]).astype(o_ref.dtype)


def _fox_sample(q, kb, vb, cache_k, cache_v, fcum, nb, seq, past):
    hp = FOX_HEADS // 2
    lpad = fcum.shape[-1]
    rmap = lambda b, h: (b, h)
    cmap = lambda b, h: (b, 0, h)
    return pl.pallas_call(
        functools.partial(_fox_sample_kernel, past=past, seq=seq),
        grid=(nb, hp),
        in_specs=[pl.BlockSpec((seq, LANES), rmap), pl.BlockSpec((seq, LANES), rmap),
                  pl.BlockSpec((seq, LANES), rmap),
                  pl.BlockSpec((1, past, LANES), cmap), pl.BlockSpec((1, past, LANES), cmap),
                  pl.BlockSpec((1, 1, 2, lpad), lambda b, h: (b, h, 0, 0))],
        out_specs=pl.BlockSpec((seq, LANES), rmap),
        out_shape=jax.ShapeDtypeStruct((nb * seq, FOX_WIDTH), BF16),
        compiler_params=_cparams(("parallel", "parallel")),
        name="fox_sample",
    )(q, kb, vb, cache_k, cache_v, fcum)


def _hgrn_kernel(*refs, tb, has_state):
    if has_state:
        q_ref, g_ref, i_ref, og_ref, gn_ref, tri_ref, s0_ref, y_ref, sfin_ref, st_sc, b_sc, o_sc = refs
    else:
        q_ref, g_ref, i_ref, og_ref, gn_ref, tri_ref, y_ref, sfin_ref, st_sc, b_sc, o_sc = refs
        s0_ref = None
    step = pl.program_id(1)
    nstep = pl.num_programs(1)

    @pl.when(step == 0)
    def _init():
        for h in range(HGRN_HEADS):
            if has_state:
                st_sc[h] = s0_ref[0, h].T
            else:
                st_sc[h] = jnp.zeros((HGRN_DK, HGRN_DK), F32)

    g = g_ref[...]
    tri = tri_ref[...]
    b = None
    for part in _split3(g):
        pb = jnp.dot(tri, part, preferred_element_type=F32)
        b = pb if b is None else b + pb
    b_sc[...] = b

    trow = lax.broadcasted_iota(I32, (SUB, 1), 0)

    def sub_chunk(c, carry):
        r0 = pl.multiple_of(c * SUB, SUB)
        for h in range(HGRN_HEADS):
            cs = slice(h * HGRN_DK, (h + 1) * HGRN_DK)
            q = q_ref[pl.ds(r0, SUB), cs]
            gg = g_ref[pl.ds(r0, SUB), cs]
            iv = i_ref[pl.ds(r0, SUB), cs]
            bb = b_sc[pl.ds(r0, SUB), cs]
            kk = 1.0 - jnp.exp(gg)
            st = st_sc[h]
            o = lax.dot_general((q * jnp.exp(bb)).astype(BF16), st.astype(BF16), NT_DIMS,
                                preferred_element_type=F32)
            for s in range(SUB):
                lo = 0 if s < 8 else 8
                ks = kk[s:s + 1, :]
                bs = bb[s:s + 1, :]
                ivs = iv[s:s + 1, :]
                e = jnp.exp(jnp.where(trow[lo:] >= s, bb[lo:] - bs, NEG_BIG))
                a = jnp.sum(q[lo:] * ks * e, axis=-1, keepdims=True)
                upd = a * ivs
                if lo:
                    upd = jnp.concatenate([jnp.zeros((8, HGRN_DK), F32), upd], axis=0)
                o = o + upd
            o_sc[pl.ds(r0, SUB), cs] = o
            bl = bb[SUB - 1:SUB, :]
            kd = kk * jnp.exp(bl - bb)
            u = lax.dot_general(iv.astype(BF16), kd.astype(BF16), TN_DIMS, preferred_element_type=F32)
            st_sc[h] = st * jnp.exp(bl) + u
        return carry

    lax.fori_loop(0, tb // SUB, sub_chunk, 0)

    for h in range(HGRN_HEADS):
        cs = slice(h * HGRN_DK, (h + 1) * HGRN_DK)
        o = o_sc[:, cs]
        ms = jnp.mean(o * o, axis=-1, keepdims=True)
        y_ref[:, cs] = (o * lax.rsqrt(ms + NORM_EPS) * gn_ref[...] * og_ref[:, cs]).astype(y_ref.dtype)

    @pl.when(step == nstep - 1)
    def _fin():
        for h in range(HGRN_HEADS):
            sfin_ref[0, h] = st_sc[h].T


def _hgrn(qs, gl, iv, og, gn, s0, nb, seq, tb):
    ns = seq // tb
    t = nb * seq
    row = pl.BlockSpec((tb, HGRN_WIDTH), lambda b, i: (b * ns + i, 0))
    r = jnp.arange(tb)
    tri = ((r[:, None] // SUB == r[None, :] // SUB) & (r[None, :] <= r[:, None])).astype(BF16)
    in_specs = [row, row, row, row, _const_spec((1, HGRN_DK)), _const_spec((tb, tb))]
    args = [qs, gl, iv, og, gn, tri]
    if s0 is not None:
        in_specs.append(pl.BlockSpec((1, HGRN_HEADS, HGRN_DK, HGRN_DK), lambda b, i: (b, 0, 0, 0)))
        args.append(s0)
    return pl.pallas_call(
        functools.partial(_hgrn_kernel, tb=tb, has_state=s0 is not None),
        grid=(nb, ns), in_specs=in_specs,
        out_specs=[row, pl.BlockSpec((1, HGRN_HEADS, HGRN_DK, HGRN_DK), lambda b, i: (b, 0, 0, 0))],
        out_shape=[jax.ShapeDtypeStruct((t, HGRN_WIDTH), BF16),
                   jax.ShapeDtypeStruct((nb, HGRN_HEADS, HGRN_DK, HGRN_DK), F32)],
        scratch_shapes=[pltpu.VMEM((HGRN_HEADS, HGRN_DK, HGRN_DK), F32),
                        pltpu.VMEM((tb, HGRN_WIDTH), F32), pltpu.VMEM((tb, HGRN_WIDTH), F32)],
        compiler_params=_cparams(("parallel", "arbitrary")), name="hgrn",
    )(*args)


def _post_kernel(x_ref, ya_ref, yb_ref, ga_ref, gb_ref, g1_ref, sh2_ref, sc2_ref, gn2_ref,
                 wpa_ref, wpb_ref, wo_ref, wrh_ref, wrl_ref, br_ref, tri_ref, cin_ref,
                 x1_ref, h2_ref, idx_ref, wt_ref, rank_ref, cout_ref, cnt_sc, *, tm):
    first = jnp.logical_and(pl.program_id(0) == 0, pl.program_id(1) == 0)

    @pl.when(first)
    def _init():
        cnt_sc[...] = cin_ref[...]

    merged = (ga_ref[...].astype(F32) * jnp.dot(ya_ref[...], wpa_ref[...], preferred_element_type=F32)
              + gb_ref[...].astype(F32) * jnp.dot(yb_ref[...], wpb_ref[...], preferred_element_type=F32))
    x1 = x_ref[...] + g1_ref[0] * jnp.dot(merged.astype(BF16), wo_ref[...], preferred_element_type=F32)
    x1_ref[...] = x1
    ms = jnp.mean(x1 * x1, axis=-1, keepdims=True)
    h2 = x1 * lax.rsqrt(ms + NORM_EPS) * gn2_ref[...]
    h2 = h2 * (1.0 + sc2_ref[0]) + sh2_ref[0]
    _store_rows(h2_ref, h2)

    hh = h2.astype(BF16)
    hl = (h2 - hh.astype(F32)).astype(BF16)
    wrh = wrh_ref[...]
    logits = (lax.dot_general(wrh, hh, NT_DIMS, preferred_element_type=F32)
              + lax.dot_general(wrh, hl, NT_DIMS, preferred_element_type=F32)
              + lax.dot_general(wrl_ref[...], hh, NT_DIMS, preferred_element_type=F32))
    scores = _sigmoid(logits)
    biased = scores + br_ref[...]

    b3 = biased.reshape(N_GROUPS, GROUP_SIZE, tm)
    it3 = lax.broadcasted_iota(I32, (N_GROUPS, GROUP_SIZE, tm), 1).astype(F32)
    m1 = jnp.max(b3, axis=1, keepdims=True)
    i1 = jnp.min(jnp.where(b3 == m1, it3, float(GROUP_SIZE)), axis=1, keepdims=True)
    m2 = jnp.max(jnp.where(it3 == i1, -jnp.inf, b3), axis=1, keepdims=True)
    gs = (m1 + m2).reshape(N_GROUPS, tm)

    gi = lax.broadcasted_iota(I32, (N_GROUPS, tm), 0)
    beat = jnp.zeros((N_GROUPS, tm), F32)
    for g in range(N_GROUPS):
        r = gs[g:g + 1, :]
        beat = beat + jnp.where((r > gs) | ((r == gs) & (g < gi)), 1.0, 0.0)
    gpen = jnp.where(beat < TOPK_GROUPS, 0.0, -jnp.inf)
    masked = (b3 + gpen.reshape(N_GROUPS, 1, tm)).reshape(N_EXPERTS, tm)

    ei = lax.broadcasted_iota(I32, (N_EXPERTS, tm), 0).astype(F32)
    idx_rows, w_rows = [], []
    chosen = jnp.zeros((N_EXPERTS, tm), F32)
    for _ in range(TOP_K):
        m = jnp.max(masked, axis=0, keepdims=True)
        ik = jnp.min(jnp.where(masked == m, ei, float(N_EXPERTS)), axis=0, keepdims=True)
        hit = ei == ik
        w_rows.append(jnp.sum(jnp.where(hit, scores, 0.0), axis=0, keepdims=True))
        idx_rows.append(ik)
        masked = jnp.where(hit, -jnp.inf, masked)
        chosen = jnp.where(hit, 1.0, chosen)
    idx_ref[...] = jnp.concatenate(idx_rows, axis=0).astype(I32)
    wts = jnp.concatenate(w_rows, axis=0)
    wt_ref[...] = wts / jnp.sum(wts, axis=0, keepdims=True) * ROUTED_SCALE

    before = cnt_sc[...] + jnp.dot(chosen.astype(BF16), tri_ref[...], preferred_element_type=F32)
    rank_rows = [jnp.sum(jnp.where(ei == idx_rows[kk], before, 0.0), axis=0, keepdims=True)
                 for kk in range(TOP_K)]
    rank_ref[...] = jnp.concatenate(rank_rows, axis=0).astype(I32)
    cnt_sc[...] = cnt_sc[...] + jnp.sum(chosen, axis=1, keepdims=True)
    cout_ref[...] = cnt_sc[...]


def _post(x2, ya, yb, ga, gb, gate1, shift2, scale2, prm, count_in, nb, tm):
    t, d = x2.shape
    ns = t // nb // tm
    row = lambda w: pl.BlockSpec((tm, w), lambda b, i: (b * ns + i, 0))
    colb = pl.BlockSpec((TOP_K, tm), lambda b, i: (0, b * ns + i))
    r = jnp.arange(tm)
    tri = (r[:, None] < r[None, :]).astype(BF16)
    in_specs = [row(d), row(FOX_WIDTH), row(HGRN_WIDTH), row(d), row(d),
                _mod_spec(gate1, tm, ns), _mod_spec(shift2, tm, ns), _mod_spec(scale2, tm, ns),
                _const_spec((1, d)), _const_spec((FOX_WIDTH, d)), _const_spec((HGRN_WIDTH, d)),
                _const_spec((d, d)), _const_spec((N_EXPERTS, d)), _const_spec((N_EXPERTS, d)),
                _const_spec((N_EXPERTS, 1)), _const_spec((tm, tm)), _const_spec((N_EXPERTS, 1))]
    out_shape = [jax.ShapeDtypeStruct((t, d), F32), jax.ShapeDtypeStruct((t * ROW_CHUNKS, LANES), F32),
                 jax.ShapeDtypeStruct((TOP_K, t), I32), jax.ShapeDtypeStruct((TOP_K, t), F32),
                 jax.ShapeDtypeStruct((TOP_K, t), I32), jax.ShapeDtypeStruct((N_EXPERTS, 1), F32)]
    out_specs = [row(d), pl.BlockSpec((tm * ROW_CHUNKS, LANES), lambda b, i: (b * ns + i, 0)),
                 colb, colb, colb, pl.BlockSpec((N_EXPERTS, 1), lambda b, i: (0, 0))]
    return pl.pallas_call(
        functools.partial(_post_kernel, tm=tm), grid=(nb, ns), in_specs=in_specs,
        out_specs=out_specs, out_shape=out_shape,
        scratch_shapes=[pltpu.VMEM((N_EXPERTS, 1), F32)],
        compiler_params=_cparams(("arbitrary", "arbitrary")), name="post",
    )(x2, ya, yb, ga, gb, gate1, shift2, scale2, prm["g2"], prm["wpa"], prm["wpb"], prm["wo"],
      prm["wrh"], prm["wrl"], prm["br"], tri, count_in)


def _pos_kernel(idx_ref, rank_ref, pst_ref, pos_ref, *, tm):
    ei = lax.broadcasted_iota(I32, (N_EXPERTS, tm), 0)
    pst = pst_ref[...]
    rows = [jnp.sum(jnp.where(ei == idx_ref[kk:kk + 1, :], pst, 0.0), axis=0, keepdims=True)
            for kk in range(TOP_K)]
    pos_ref[...] = jnp.concatenate(rows, axis=0).astype(I32) + rank_ref[...]


def _pos(idx, rank, pstart_col, tm):
    t = idx.shape[1]
    colb = pl.BlockSpec((TOP_K, tm), lambda i: (0, i))
    return pl.pallas_call(
        functools.partial(_pos_kernel, tm=tm), grid=(t // tm,),
        in_specs=[colb, colb, _const_spec((N_EXPERTS, 1))], out_specs=colb,
        out_shape=jax.ShapeDtypeStruct((TOP_K, t), I32),
        compiler_params=_cparams(("parallel",)), name="moe_pos",
    )(idx, rank, pstart_col)


def _padfill_kernel(cnt_ref, pad_ref, pst_ref, xs_ref, zero_sc, sem):
    zero_sc[...] = jnp.zeros(zero_sc.shape, zero_sc.dtype)
    sizes = [1 << b for b in reversed(range(MOE_BLK.bit_length() - 1))]

    def chunks(e, act):
        n = pad_ref[e] - cnt_ref[e]
        row = pst_ref[e] + cnt_ref[e]
        for size in sizes:
            @pl.when((n & size) != 0)
            def _():
                act(pltpu.make_async_copy(zero_sc.at[pl.ds(0, size)], xs_ref.at[pl.ds(row, size)], sem))

            row = row + (n & size)

    def issue(e, c):
        chunks(e, lambda cp: cp.start())
        return c

    def drain(e, c):
        chunks(e, lambda cp: cp.wait())
        return c

    lax.fori_loop(0, N_EXPERTS, issue, 0)
    lax.fori_loop(0, N_EXPERTS, drain, 0)


def _padfill(counts, padded, pstart, n_rows, d):
    row_tile = (d // LANES, LANES)
    return pl.pallas_call(
        _padfill_kernel,
        grid_spec=pltpu.PrefetchScalarGridSpec(
            num_scalar_prefetch=3, grid=(1,), in_specs=[],
            out_specs=pl.BlockSpec(memory_space=pl.ANY),
            scratch_shapes=[pltpu.VMEM((MOE_BLK // 2,) + row_tile, F32), pltpu.SemaphoreType.DMA(())]),
        out_shape=jax.ShapeDtypeStruct((n_rows,) + row_tile, F32),
        compiler_params=pltpu.CompilerParams(dimension_semantics=("arbitrary",), has_side_effects=True),
        name="moe_padfill",
    )(counts, padded, pstart)


def _dispatch_kernel(pos_ref, h2_ref, xs_in_ref, xs_ref, sem, *, tm):
    del xs_in_ref

    for t in range(tm):
        for kk in range(TOP_K):
            pltpu.make_async_copy(h2_ref.at[t], xs_ref.at[pos_ref[kk, t]], sem).start(priority=kk % 2)
    for kk in range(TOP_K):
        pltpu.make_async_copy(h2_ref, xs_ref.at[pl.ds(0, tm)], sem).wait()


def _dispatch(pos, h2_flat, xs, tm):
    h2 = h2_flat.reshape(-1, ROW_CHUNKS, LANES)
    t = h2.shape[0]
    return pl.pallas_call(
        functools.partial(_dispatch_kernel, tm=tm),
        grid=(t // tm,),
        in_specs=[pl.BlockSpec((TOP_K, tm), lambda i: (0, i), memory_space=pltpu.SMEM),
                  pl.BlockSpec((tm,) + h2.shape[1:], lambda i: (i, 0, 0)),
                  pl.BlockSpec(memory_space=pl.ANY)],
        out_specs=pl.BlockSpec(memory_space=pl.ANY),
        out_shape=jax.ShapeDtypeStruct(xs.shape, xs.dtype),
        scratch_shapes=[pltpu.SemaphoreType.DMA(())],
        input_output_aliases={2: 0},
        compiler_params=pltpu.CompilerParams(dimension_semantics=("arbitrary",), has_side_effects=True,
                                             vmem_limit_bytes=VMEM_LIMIT),
        name="moe_dispatch",
    )(pos, h2, xs)


def _moe_kernel(be_ref, nu_ref, first_ref, slot_ref, nxt_ref, x_ref, wg_ref, wu_ref, wd_ref, y_ref,
                wg_buf, wu_buf, wd_buf, wgu_sc, wd_sc, sem):
    i = pl.program_id(0)
    e = be_ref[i]

    def fetch(expert, slot):
        return (pltpu.make_async_copy(wg_ref.at[expert], wg_buf.at[slot], sem.at[slot]),
                pltpu.make_async_copy(wu_ref.at[expert], wu_buf.at[slot], sem.at[slot]),
                pltpu.make_async_copy(wd_ref.at[expert], wd_buf.at[slot], sem.at[slot]))

    @pl.when(first_ref[i] == 1)
    def _switch_expert():
        slot = slot_ref[i]

        @pl.when(i == 0)
        def _():
            for cp in fetch(e, slot):
                cp.start()

        for cp in fetch(e, slot):
            cp.wait()
        nxt = nxt_ref[e]

        @pl.when(nxt >= 0)
        def _():
            for cp in fetch(nxt, 1 - slot):
                cp.start()

        wgu_sc[:, 0:D_EXPERT] = wg_buf[slot].astype(BF16)
        wgu_sc[:, D_EXPERT:2 * D_EXPERT] = wu_buf[slot].astype(BF16)
        wd_sc[...] = wd_buf[slot].astype(BF16)

    @pl.when(i < nu_ref[0])
    def _compute():
        gu = jnp.dot(_load_rows(x_ref).astype(BF16), wgu_sc[...], preferred_element_type=F32)
        gt = gu[:, 0:D_EXPERT]
        h = gt * _sigmoid(gt) * gu[:, D_EXPERT:2 * D_EXPERT]
        _store_rows(y_ref, jnp.dot(h.astype(BF16), wd_sc[...], preferred_element_type=F32))


def _moe(blk_expert, n_used, first, slot, nxt, xs_pool, w_gate, w_up, w_down):
    n_rows = xs_pool.shape[0]
    xs = xs_pool.reshape(n_rows * ROW_CHUNKS, LANES)
    d = w_gate.shape[1]
    nblk = n_rows // MOE_BLK
    xmap = lambda i, be, nu, fi, sl, nx: (jnp.minimum(i, nu[0] - 1), 0)
    hbm = pl.BlockSpec(memory_space=pl.ANY)
    return pl.pallas_call(
        _moe_kernel,
        grid_spec=pltpu.PrefetchScalarGridSpec(
            num_scalar_prefetch=5, grid=(nblk,),
            in_specs=[pl.BlockSpec((MOE_BLK * ROW_CHUNKS, LANES), xmap), hbm, hbm, hbm],
            out_specs=pl.BlockSpec((MOE_BLK * ROW_CHUNKS, LANES), xmap),
            scratch_shapes=[pltpu.VMEM((2, d, D_EXPERT), F32), pltpu.VMEM((2, d, D_EXPERT), F32),
                            pltpu.VMEM((2, D_EXPERT, d), F32),
                            pltpu.VMEM((d, 2 * D_EXPERT), BF16), pltpu.VMEM((D_EXPERT, d), BF16),
                            pltpu.SemaphoreType.DMA((2,))]),
        out_shape=jax.ShapeDtypeStruct(xs.shape, F32),
        compiler_params=_cparams(("arbitrary",)), name="moe_experts",
    )(blk_expert, n_used, first, slot, nxt, xs, w_gate, w_up, w_down)


def _combine_kernel(pos_ref, posn_ref, x1_ref, h2_ref, g2_ref, wt_ref, wsgu_ref, wsd_ref, ys_ref, ysflat_ref,
                    o_ref, buf, sem, *, tm):
    lin = pl.program_id(0) * pl.num_programs(1) + pl.program_id(1)
    total = pl.num_programs(0) * pl.num_programs(1)
    slot = lin % 2

    def issue(p_ref, s):
        for t in range(tm):
            for kk in range(TOP_K):
                pltpu.make_async_copy(ys_ref.at[p_ref[kk, t]], buf.at[s, kk, pl.ds(t * ROW_CHUNKS, ROW_CHUNKS)],
                                      sem.at[s]).start(priority=kk % 2)

    @pl.when(lin == 0)
    def _():
        issue(pos_ref, 0)

    @pl.when(lin + 1 < total)
    def _():
        issue(posn_ref, 1 - slot)

    gu = jnp.dot(_load_rows(h2_ref).astype(BF16), wsgu_ref[...], preferred_element_type=F32)
    gt = gu[:, 0:D_SHARED]
    hs = gt * _sigmoid(gt) * gu[:, D_SHARED:2 * D_SHARED]
    shared = jnp.dot(hs.astype(BF16), wsd_ref[...], preferred_element_type=F32)

    for kk in range(TOP_K):
        pltpu.make_async_copy(ysflat_ref.at[pl.ds(0, tm * ROW_CHUNKS)], buf.at[slot, kk], sem.at[slot]).wait()

    nchunk = shared.shape[-1] // LANES
    acc = [shared[:, c * LANES:(c + 1) * LANES] for c in range(nchunk)]
    for kk in range(TOP_K):
        wrow = jnp.broadcast_to(wt_ref[kk:kk + 1, :], (LANES, tm))
        wcol = wrow.T
        for c in range(nchunk):
            acc[c] = acc[c] + _row_chunk(buf.at[slot, kk], c)[...] * wcol
    o_ref[...] = x1_ref[...] + g2_ref[0] * jnp.concatenate(acc, axis=1)


def _combine(pos, x1, h2, gate2, wts, wsgu, wsd, ys_flat, nb, tm):
    t, d = x1.shape
    ns = t // nb // tm
    row = pl.BlockSpec((tm, d), lambda b, i: (b * ns + i, 0))
    last = nb * ns - 1
    ys = ys_flat.reshape(-1, ROW_CHUNKS, LANES)
    return pl.pallas_call(
        functools.partial(_combine_kernel, tm=tm),
        grid=(nb, ns),
        in_specs=[pl.BlockSpec((TOP_K, tm), lambda b, i: (0, b * ns + i), memory_space=pltpu.SMEM),
                  pl.BlockSpec((TOP_K, tm), lambda b, i: (0, jnp.minimum(b * ns + i + 1, last)),
                               memory_space=pltpu.SMEM),
                  row, pl.BlockSpec((tm * ROW_CHUNKS, LANES), lambda b, i: (b * ns + i, 0)),
                  _mod_spec(gate2, tm, ns),
                  pl.BlockSpec((TOP_K, tm), lambda b, i: (0, b * ns + i)),
                  _const_spec(wsgu.shape), _const_spec(wsd.shape),
                  pl.BlockSpec(memory_space=pl.ANY), pl.BlockSpec(memory_space=pl.ANY)],
        out_specs=row,
        out_shape=jax.ShapeDtypeStruct((t, d), F32),
        scratch_shapes=[pltpu.VMEM((2, TOP_K, tm * ROW_CHUNKS, LANES), F32), pltpu.SemaphoreType.DMA((2,))],
        compiler_params=_cparams(("arbitrary", "arbitrary")), name="moe_combine",
    )(pos, pos, x1, h2, gate2, wts, wsgu, wsd, ys, ys_flat)


def _prepare_params(g_norm1, w_in, b_fox_f, g_q, g_k, hgrn_lb, g_hgrn_o, w_proj_a, w_proj_b, w_out,
                    g_norm2, w_router, b_router, w_sh_gate, w_sh_up, w_sh_down):
    d = D_MODEL
    fw, hw = FOX_WIDTH, HGRN_WIDTH
    c0 = 3 * fw
    c1 = c0 + FOX_HEADS
    c2 = c1 + 4 * hw
    wff = w_in[:, c0:c1]
    head = jnp.arange(fw) // FOX_HEAD_DIM
    wr_t = w_router.T
    wrh = wr_t.astype(BF16)
    return dict(
        g1=g_norm1.reshape(1, d),
        wqkv=w_in[:, :c0].astype(BF16),
        wff=jnp.pad(wff, ((0, 0), (0, LANES - FOX_HEADS))).astype(BF16),
        wfft=jnp.pad(wff.T, ((0, 16 - FOX_HEADS), (0, 0))).astype(BF16),
        bf=b_fox_f.reshape(1, FOX_HEADS), bft=b_fox_f.reshape(FOX_HEADS, 1),
        gq=jnp.tile(g_q, FOX_HEADS).reshape(1, fw), gk=jnp.tile(g_k, FOX_HEADS).reshape(1, fw),
        bd=(head[:, None] == head[None, :]).astype(BF16) * (1.0 / FOX_HEAD_DIM),
        lbp=hgrn_lb,
        wh=w_in[:, c1:c2].astype(BF16), wg=w_in[:, c2:].astype(BF16),
        gn=g_hgrn_o.reshape(1, HGRN_DK),
        g2=g_norm2.reshape(1, d),
        wpa=w_proj_a.astype(BF16), wpb=w_proj_b.astype(BF16), wo=w_out.astype(BF16),
        wrh=wrh, wrl=(wr_t - wrh.astype(F32)).astype(BF16), br=b_router.reshape(N_EXPERTS, 1),
        wsgu=jnp.concatenate([w_sh_gate, w_sh_up], axis=1).astype(BF16), wsd=w_sh_down.astype(BF16),
    )


def _mixers(x2, nb, seq, tm, shift1, scale1, prm, past, tile, tb):
    (q, k, kb, v, vb, lf, lft, hq, hl, hi, og, ga, gb) = _inproj(x2, nb if shift1.shape[1] == 1 else 1, tm,
                                                                 shift1, scale1, prm)
    t = nb * seq
    hp = FOX_HEADS // 2
    lft = lft.transpose(1, 0, 2).reshape(FOX_HEADS, nb, seq).transpose(1, 0, 2)
    if past is None:
        fcum = _cumsum_lanes(lft.reshape(nb * FOX_HEADS, seq)).reshape(nb, hp, 2, seq)
        ya = _fox_prompt(q, kb, vb, fcum, nb, seq, tile, min(32, tile))
        s0 = None
    else:
        cache_k, cache_v, cache_lf, s0 = past
        plen = cache_k.shape[1]
        ltot = plen + seq
        lpad = -(-ltot // LANES) * LANES
        lf_all = jnp.concatenate([cache_lf.transpose(0, 2, 1), lft,
                                  jnp.zeros((nb, FOX_HEADS, lpad - ltot), F32)], axis=-1)
        fcum = _cumsum_lanes(lf_all.reshape(nb * FOX_HEADS, lpad)).reshape(nb, hp, 2, lpad)
        ya = _fox_sample(q, kb, vb, cache_k.reshape(nb, plen, FOX_WIDTH), cache_v.reshape(nb, plen, FOX_WIDTH),
                         fcum, nb, seq, plen)
    yb, sfin = _hgrn(hq, hl, hi, og, prm["gn"], s0, nb, seq, tb)
    return k, v, lf, sfin, ya, yb, ga, gb


def kernel(x_prompt, x_sample, cache_fox_k, cache_fox_v, cache_fox_logf, state_hgrn, c_prompt, c_sample,
           w_ada, b_ada, g_norm1, w_in, b_fox_f, g_q, g_k, hgrn_lb, g_hgrn_o, w_proj_a, w_proj_b, w_out,
           g_norm2, w_router, b_router, w_exp_gate, w_exp_up, w_exp_down, w_sh_gate, w_sh_up, w_sh_down):
    assert w_ada.shape[0] == 1 and hgrn_lb.shape[0] == 2, "single-layer trunk"
    d = D_MODEL
    bp, sp, _ = x_prompt.shape
    bs, ss, _ = x_sample.shape
    tp, ts = bp * sp, bs * ss
    prm = _prepare_params(g_norm1[0], w_in[0], b_fox_f[0], g_q[0], g_k[0], hgrn_lb, g_hgrn_o[0],
                          w_proj_a[0], w_proj_b[0], w_out[0], g_norm2[0], w_router[0], b_router[0],
                          w_sh_gate[0], w_sh_up[0], w_sh_down[0])

    bc = bp + bs
    bc_pad = -(-bc // 8) * 8
    c_all = jnp.concatenate([c_prompt, c_sample, jnp.zeros((bc_pad - bc, d), F32)], axis=0)
    mod = _ada(c_all, w_ada[0], b_ada[0])
    mod_p = [mod[:bp, j * d:(j + 1) * d].reshape(bp, 1, d) for j in range(6)]
    mod_s = [jnp.repeat(mod[bp:bc, j * d:(j + 1) * d], ss, axis=0).reshape(1, ts, d) for j in range(6)]

    tm_p = min(512, sp)
    tm_s = min(256, ts)
    xp2 = x_prompt.reshape(tp, d)
    xs2 = x_sample.reshape(ts, d)
    tile = min(1024, sp)
    tb = min(256, sp)

    kp, vp, lfp, sfin_p, ya_p, yb_p, ga_p, gb_p = _mixers(
        xp2, bp, sp, tm_p, mod_p[0], mod_p[1], prm, None, tile, tb)
    past = (cache_fox_k[0], cache_fox_v[0], cache_fox_logf[0], state_hgrn[0])
    ks, vs, lfs, sfin_s, ya_s, yb_s, ga_s, gb_s = _mixers(
        xs2, bs, ss, tm_s, mod_s[0], mod_s[1], prm, past, None, min(256, ss))

    zero_cnt = jnp.zeros((N_EXPERTS, 1), F32)
    x1_p, h2_p, idx_p, wt_p, rank_p, cnt_p = _post(xp2, ya_p, yb_p, ga_p, gb_p, mod_p[2], mod_p[3], mod_p[4],
                                                    prm, zero_cnt, bp, tm_p)
    x1_s, h2_s, idx_s, wt_s, rank_s, cnt_all = _post(xs2, ya_s, yb_s, ga_s, gb_s, mod_s[2], mod_s[3], mod_s[4],
                                                     prm, cnt_p, 1, tm_s)

    counts = cnt_all.reshape(N_EXPERTS).astype(I32)
    padded = (counts + MOE_BLK - 1) // MOE_BLK * MOE_BLK
    pend = jnp.cumsum(padded)
    pstart = pend - padded
    nblk = -(-((tp + ts) * TOP_K + N_EXPERTS * (MOE_BLK - 1)) // MOE_BLK)
    n_used = (pend[-1] // MOE_BLK).reshape(1)
    blk_row0 = jnp.arange(nblk, dtype=I32) * MOE_BLK
    blk_expert = jnp.minimum(jnp.sum((pend[None, :] <= blk_row0[:, None]).astype(I32), axis=1), N_EXPERTS - 1)
    blk_used = jnp.arange(nblk) < n_used[0]
    blk_expert = jnp.where(blk_used, blk_expert, blk_expert[jnp.maximum(n_used[0] - 1, 0)])
    blk_first = blk_used & jnp.concatenate([jnp.ones((1,), bool), blk_expert[1:] != blk_expert[:-1]])
    blk_slot = ((jnp.cumsum(blk_first.astype(I32)) - 1) & 1).astype(I32)
    eids = jnp.arange(N_EXPERTS, dtype=I32)
    later = (eids[None, :] > eids[:, None]) & (padded[None, :] > 0)
    nxt_expert = jnp.min(jnp.where(later, eids[None, :], N_EXPERTS), axis=1)
    nxt_expert = jnp.where(nxt_expert < N_EXPERTS, nxt_expert, -1).astype(I32)
    pstart_col = pstart.astype(F32).reshape(N_EXPERTS, 1)
    pos_p = _pos(idx_p, rank_p, pstart_col, tm_p)
    pos_s = _pos(idx_s, rank_s, pstart_col, tm_s)

    xs_pool = _padfill(counts, padded, pstart, nblk * MOE_BLK, d)
    xs_pool = _dispatch(pos_p, h2_p, xs_pool, min(128, tm_p))
    xs_pool = _dispatch(pos_s, h2_s, xs_pool, min(128, tm_s))
    ys_pool = _moe(blk_expert, n_used, blk_first.astype(I32), blk_slot, nxt_expert, xs_pool,
                   w_exp_gate[0], w_exp_up[0], w_exp_down[0])

    tm_c = min(128, sp)
    y_p = _combine(pos_p, x1_p, h2_p, mod_p[5], wt_p, prm["wsgu"], prm["wsd"], ys_pool, bp, tm_c)
    y_s = _combine(pos_s, x1_s, h2_s, mod_s[5], wt_s, prm["wsgu"], prm["wsd"], ys_pool, 1, min(128, ts))

    return (y_p.reshape(bp, sp, d), y_s.reshape(bs, ss, d),
            kp.reshape(1, bp, sp, FOX_HEADS, FOX_HEAD_DIM), vp.reshape(1, bp, sp, FOX_HEADS, FOX_HEAD_DIM),
            lfp.reshape(1, bp, sp, FOX_HEADS), sfin_p[None],
            ks.reshape(1, bs, ss, FOX_HEADS, FOX_HEAD_DIM), vs.reshape(1, bs, ss, FOX_HEADS, FOX_HEAD_DIM),
            lfs.reshape(1, bs, ss, FOX_HEADS), sfin_s[None])
```

```python
import functools

import jax
import jax.numpy as jnp
from jax import lax
from jax.experimental import pallas as pl
from jax.experimental.pallas import tpu as pltpu

F32 = jnp.float32
BF16 = jnp.bfloat16
I32 = jnp.int32

D_MODEL = 1024
FOX_HEADS = 8
FOX_HEAD_DIM = 64
FOX_WIDTH = FOX_HEADS * FOX_HEAD_DIM
HGRN_HEADS = 4
HGRN_DK = 128
HGRN_WIDTH = HGRN_HEADS * HGRN_DK
N_EXPERTS = 256
TOP_K = 8
N_GROUPS = 8
TOPK_GROUPS = 4
GROUP_SIZE = N_EXPERTS // N_GROUPS
D_EXPERT = 256
D_SHARED = 256
ROUTED_SCALE = 2.5
NORM_EPS = 1e-6
NEG_BIG = -1e30
LOG2E = 1.4426950408889634
QK_SCALE = FOX_HEAD_DIM ** -0.5 * LOG2E

LANES = 128
SUB = 16
MOE_BLK = 512
VMEM_LIMIT = 56 * 1024 * 1024

NT_DIMS = (((1,), (1,)), ((), ()))
TN_DIMS = (((0,), (0,)), ((), ()))


def _cparams(sem):
    return pltpu.CompilerParams(dimension_semantics=sem, vmem_limit_bytes=VMEM_LIMIT)


def _const_spec(shape):
    nd = len(shape)
    return pl.BlockSpec(shape, lambda *_: (0,) * nd, pipeline_mode=pl.Buffered(1))


def _sigmoid(z):
    return 1.0 / (1.0 + jnp.exp(-z))


def _log_sigmoid(z):
    return jnp.minimum(z, 0.0) - jnp.log(1.0 + jnp.exp(-jnp.abs(z)))


ROW_CHUNKS = D_MODEL // LANES


def _row_chunk(ref, c):
    return ref.at[pl.ds(c, ref.shape[0] // ROW_CHUNKS, stride=ROW_CHUNKS), :]


def _store_rows(ref, val):
    for c in range(ROW_CHUNKS):
        _row_chunk(ref, c)[...] = val[:, c * LANES:(c + 1) * LANES]


def _load_rows(ref):
    return jnp.concatenate([_row_chunk(ref, c)[...] for c in range(ROW_CHUNKS)], axis=1)


def _split3(a):
    hi = a.astype(BF16)
    r1 = a - hi.astype(F32)
    mid = r1.astype(BF16)
    lo = (r1 - mid.astype(F32)).astype(BF16)
    return hi, mid, lo


def _ada_kernel(c_ref, w_ref, b_ref, o_ref):
    c = c_ref[...]
    s = c * _sigmoid(c)
    o_ref[...] = jnp.dot(s.astype(BF16), w_ref[...].astype(BF16), preferred_element_type=F32) + b_ref[...]


def _ada(c_all, w_ada, b_ada):
    bc, d = c_all.shape
    n = w_ada.shape[1]
    tn = 1024
    return pl.pallas_call(
        _ada_kernel,
        grid=(n // tn,),
        in_specs=[pl.BlockSpec((bc, d), lambda j: (0, 0)),
                  pl.BlockSpec((d, tn), lambda j: (0, j)),
                  pl.BlockSpec((1, tn), lambda j: (0, j))],
        out_specs=pl.BlockSpec((bc, tn), lambda j: (0, j)),
        out_shape=jax.ShapeDtypeStruct((bc, n), F32),
        compiler_params=_cparams(("parallel",)),
        name="ada",
    )(c_all, w_ada, b_ada.reshape(1, n))


def _inproj_kernel(x_ref, sh_ref, sc_ref, g1_ref, wqkv_ref, wff_ref, wfft_ref, bf_ref, bft_ref,
                   gq_ref, gk_ref, bd_ref, lbp_ref, wh_ref, wg_ref,
                   q_ref, k_ref, kb_ref, v_ref, vb_ref, lf_ref, lft_ref,
                   hq_ref, hl_ref, hi_ref, og_ref, ga_ref, gb_ref):
    x = x_ref[...]
    ms = jnp.mean(x * x, axis=-1, keepdims=True)
    h = x * lax.rsqrt(ms + NORM_EPS) * g1_ref[...]
    h = h * (1.0 + sc_ref[0]) + sh_ref[0]
    hb = h.astype(BF16)

    def headnorm(a, g):
        ss = jnp.dot((a * a).astype(BF16), bd_ref[...], preferred_element_type=F32)
        return a * lax.rsqrt(ss + NORM_EPS) * g

    fq = jnp.dot(hb, wqkv_ref[:, 0:FOX_WIDTH], preferred_element_type=F32)
    q_ref[...] = (headnorm(fq, gq_ref[...]) * QK_SCALE).astype(BF16)
    fk = jnp.dot(hb, wqkv_ref[:, FOX_WIDTH:2 * FOX_WIDTH], preferred_element_type=F32)
    k = headnorm(fk, gk_ref[...])
    kb_ref[...] = k.astype(BF16)
    fv = jnp.dot(hb, wqkv_ref[:, 2 * FOX_WIDTH:3 * FOX_WIDTH], preferred_element_type=F32)
    vb_ref[...] = fv.astype(BF16)
    for hd in range(FOX_HEADS):
        cols = slice(hd * FOX_HEAD_DIM, (hd + 1) * FOX_HEAD_DIM)
        k_ref[:, hd, :] = k[:, cols]
        v_ref[:, hd, :] = fv[:, cols]

    ff = jnp.dot(hb, wff_ref[...], preferred_element_type=F32)
    lf_ref[...] = _log_sigmoid(ff[:, 0:FOX_HEADS] + bf_ref[...])
    fft = lax.dot_general(wfft_ref[...], hb, NT_DIMS, preferred_element_type=F32)
    lft_ref[0] = _log_sigmoid(fft[0:FOX_HEADS, :] + bft_ref[...])

    lbp = lbp_ref[...]
    e = jnp.exp(lbp - jnp.max(lbp, axis=0, keepdims=True))
    lb = e[0:1, :] / jnp.sum(e, axis=0, keepdims=True)

    w = HGRN_WIDTH
    hq = jnp.dot(hb, wh_ref[:, 0:w], preferred_element_type=F32)
    hq_ref[...] = hq * _sigmoid(hq)
    hf = jnp.dot(hb, wh_ref[:, w:2 * w], preferred_element_type=F32)
    hl_ref[...] = jnp.log(lb + (1.0 - lb) * _sigmoid(hf))
    hi_ref[...] = jnp.dot(hb, wh_ref[:, 2 * w:3 * w], preferred_element_type=F32)
    hg = jnp.dot(hb, wh_ref[:, 3 * w:4 * w], preferred_element_type=F32)
    og_ref[...] = _sigmoid(hg)
    ga = jnp.dot(hb, wg_ref[:, 0:D_MODEL], preferred_element_type=F32)
    ga_ref[...] = _sigmoid(ga).astype(BF16)
    gb = jnp.dot(hb, wg_ref[:, D_MODEL:2 * D_MODEL], preferred_element_type=F32)
    gb_ref[...] = _sigmoid(gb).astype(BF16)


def _mod_spec(arr, tm, ns):
    d = arr.shape[-1]
    if arr.shape[1] == 1:
        return pl.BlockSpec((1, 1, d), lambda b, i: (b, 0, 0))
    return pl.BlockSpec((1, tm, d), lambda b, i: (0, b * ns + i, 0))


def _inproj(x2, nb, tm, shift1, scale1, prm):
    t, d = x2.shape
    ns = t // nb // tm
    row = lambda w: pl.BlockSpec((tm, w), lambda b, i: (b * ns + i, 0))
    in_specs = [row(d), _mod_spec(shift1, tm, ns), _mod_spec(scale1, tm, ns),
                _const_spec((1, d)), _const_spec(prm["wqkv"].shape), _const_spec(prm["wff"].shape),
                _const_spec(prm["wfft"].shape), _const_spec((1, FOX_HEADS)), _const_spec((FOX_HEADS, 1)),
                _const_spec((1, FOX_WIDTH)), _const_spec((1, FOX_WIDTH)), _const_spec((FOX_WIDTH, FOX_WIDTH)),
                _const_spec(prm["lbp"].shape), _const_spec(prm["wh"].shape), _const_spec(prm["wg"].shape)]
    fw, hw = FOX_WIDTH, HGRN_WIDTH
    heads = pl.BlockSpec((tm, FOX_HEADS, FOX_HEAD_DIM), lambda b, i: (b * ns + i, 0, 0))
    out_shape = [jax.ShapeDtypeStruct((t, fw), BF16),
                 jax.ShapeDtypeStruct((t, FOX_HEADS, FOX_HEAD_DIM), F32),
                 jax.ShapeDtypeStruct((t, fw), BF16),
                 jax.ShapeDtypeStruct((t, FOX_HEADS, FOX_HEAD_DIM), F32),
                 jax.ShapeDtypeStruct((t, fw), BF16),
                 jax.ShapeDtypeStruct((t, FOX_HEADS), F32),
                 jax.ShapeDtypeStruct((nb * ns, FOX_HEADS, tm), F32),
                 jax.ShapeDtypeStruct((t, hw), F32),
                 jax.ShapeDtypeStruct((t, hw), F32),
                 jax.ShapeDtypeStruct((t, hw), F32),
                 jax.ShapeDtypeStruct((t, hw), F32),
                 jax.ShapeDtypeStruct((t, d), BF16),
                 jax.ShapeDtypeStruct((t, d), BF16)]
    out_specs = [row(fw), heads, row(fw), heads, row(fw), row(FOX_HEADS),
                 pl.BlockSpec((1, FOX_HEADS, tm), lambda b, i: (b * ns + i, 0, 0)),
                 row(hw), row(hw), row(hw), row(hw), row(d), row(d)]
    return pl.pallas_call(
        _inproj_kernel, grid=(nb, ns), in_specs=in_specs, out_specs=out_specs, out_shape=out_shape,
        compiler_params=_cparams(("parallel", "parallel")), name="inproj",
    )(x2, shift1, scale1, prm["g1"], prm["wqkv"], prm["wff"], prm["wfft"], prm["bf"], prm["bft"],
      prm["gq"], prm["gk"], prm["bd"], prm["lbp"], prm["wh"], prm["wg"])


def _cumsum_kernel(x_ref, o_ref):
    x = x_ref[...]
    n = x.shape[-1]
    lane = lax.broadcasted_iota(I32, x.shape, 1)
    s = 1
    while s < n:
        x = x + jnp.where(lane >= s, pltpu.roll(x, s, axis=1), 0.0)
        s *= 2
    o_ref[...] = x * LOG2E


def _cumsum_lanes(x):
    return pl.pallas_call(
        _cumsum_kernel, out_shape=jax.ShapeDtypeStruct(x.shape, F32),
        compiler_params=pltpu.CompilerParams(vmem_limit_bytes=VMEM_LIMIT), name="cumsum",
    )(x)


def _fox_prompt_kernel(qt_ref, kt_ref, q_ref, k_ref, v_ref, fk_ref, o_ref,
                       m0, m1, l0, l1, a0, a1, s0, s1, p0, p1, *, tile, rsub):
    step = pl.program_id(2)
    qi = qt_ref[step]
    ki = kt_ref[step]
    heads = ((m0, l0, a0, s0, p0), (m1, l1, a1, s1, p1))
    lane = lax.broadcasted_iota(I32, (1, LANES), 1)
    lo_half = lane < FOX_HEAD_DIM

    @pl.when(ki == 0)
    def _init():
        for m_sc, l_sc, a_sc, _, _ in heads:
            m_sc[...] = jnp.full(m_sc.shape, NEG_BIG, F32)
            l_sc[...] = jnp.zeros(l_sc.shape, F32)
            a_sc[...] = jnp.zeros(a_sc.shape, F32)

    def tile_update(diagonal):
        q = q_ref[...]
        k = k_ref[...]
        v = v_ref[...]
        fk = fk_ref[0, 0]
        nsub = tile // rsub
        if diagonal:
            ahead = (lax.broadcasted_iota(I32, (rsub, tile), 1)
                     - lax.broadcasted_iota(I32, (rsub, tile), 0))
        for hs, (_, _, _, s_sc, _) in enumerate(heads):
            sel = lo_half if hs == 0 else jnp.logical_not(lo_half)
            qh = jnp.where(sel, q, jnp.zeros_like(q))
            s_sc[...] = lax.dot_general(qh, k, NT_DIMS, preferred_element_type=F32)
        for hs, (m_sc, l_sc, a_sc, s_sc, p_sc) in enumerate(heads):

            def biased(r):
                rows = slice(r * rsub, (r + 1) * rsub)
                cw = min(tile, -(-((r + 1) * rsub) // LANES) * LANES) if diagonal else tile
                s = s_sc[rows, 0:cw] - fk[hs:hs + 1, 0:cw]
                if diagonal:
                    s = jnp.where(ahead[:, 0:cw] <= r * rsub, s, NEG_BIG)
                return s, rows, cw

            m_old = m_sc[...]
            m_parts = []
            for r in range(nsub):
                s, rows, _ = biased(r)
                m_parts.append(jnp.maximum(m_old[rows], jnp.max(s, axis=1, keepdims=True)))
            l_parts = []
            for r in range(nsub):
                s, rows, cw = biased(r)
                nc = cw // LANES
                p = jnp.exp2(s - jnp.concatenate([m_parts[r]] * nc, axis=1))
                part = p[:, 0:LANES]
                for c in range(1, nc):
                    part = part + p[:, c * LANES:(c + 1) * LANES]
                l_parts.append(part)
                p_sc[rows, 0:cw] = p.astype(BF16)
                if cw < tile:
                    p_sc[rows, cw:tile] = jnp.zeros((rsub, tile - cw), BF16)
            m_new = jnp.concatenate(m_parts, axis=0)
            alpha = jnp.exp2(m_old - m_new)
            m_sc[...] = m_new
            l_sc[...] = alpha * l_sc[...] + jnp.concatenate(l_parts, axis=0)
            a_sc[...] = alpha * a_sc[...] + jnp.dot(p_sc[...], v, preferred_element_type=F32)

    @pl.when(ki < qi)
    def _full():
        tile_update(False)

    @pl.when(ki == qi)
    def _diag():
        tile_update(True)
        o0 = a0[...] / jnp.sum(l0[...], axis=1, keepdims=True)
        o1 = a1[...] / jnp.sum(l1[...], axis=1, keepdims=True)
        o_ref[...] = jnp.where(lo_half, o0, o1).astype(o_ref.dtype)


def _fox_prompt(q, kb, vb, fcum, nb, seq, tile, rsub):
    nt = seq // tile
    hp = FOX_HEADS // 2
    qt = jnp.asarray([qi for qi in range(nt) for _ in range(qi + 1)], I32)
    kt = jnp.asarray([ki for qi in range(nt) for ki in range(qi + 1)], I32)
    qmap = lambda b, h, s, qt, kt: (b * nt + qt[s], h)
    kmap = lambda b, h, s, qt, kt: (b * nt + kt[s], h)
    fmap = lambda b, h, s, qt, kt: (b, h, 0, kt[s])
    stat = pltpu.VMEM((tile, LANES), F32)
    return pl.pallas_call(
        functools.partial(_fox_prompt_kernel, tile=tile, rsub=rsub),
        grid_spec=pltpu.PrefetchScalarGridSpec(
            num_scalar_prefetch=2, grid=(nb, hp, nt * (nt + 1) // 2),
            in_specs=[pl.BlockSpec((tile, LANES), qmap), pl.BlockSpec((tile, LANES), kmap),
                      pl.BlockSpec((tile, LANES), kmap), pl.BlockSpec((1, 1, 2, tile), fmap)],
            out_specs=pl.BlockSpec((tile, LANES), qmap),
            scratch_shapes=[stat, stat, stat, stat, stat, stat,
                            pltpu.VMEM((tile, tile), F32), pltpu.VMEM((tile, tile), F32),
                            pltpu.VMEM((tile, tile), BF16), pltpu.VMEM((tile, tile), BF16)]),
        out_shape=jax.ShapeDtypeStruct((nb * seq, FOX_WIDTH), BF16),
        compiler_params=_cparams(("parallel", "parallel", "arbitrary")),
        name="fox_prompt",
    )(qt, kt, q, kb, vb, fcum)


def _fox_sample_kernel(q_ref, kn_ref, vn_ref, kp_ref, vp_ref, fk_ref, o_ref, *, past, seq):
    lane = lax.broadcasted_iota(I32, (1, LANES), 1)
    lo_half = lane < FOX_HEAD_DIM
    q = q_ref[...]
    kn = kn_ref[...]
    vn = vn_ref[...]
    kp = kp_ref[0].astype(BF16)
    vp = vp_ref[0].astype(BF16)
    fk = fk_ref[0, 0]
    row = lax.broadcasted_iota(I32, (seq, seq), 0)
    col = lax.broadcasted_iota(I32, (seq, seq), 1)
    causal = col <= row
    outs = []
    for hs in range(2):
        sel = lo_half if hs == 0 else jnp.logical_not(lo_half)
        qh = jnp.where(sel, q, jnp.zeros_like(q))
        sp = lax.dot_general(qh, kp, NT_DIMS, preferred_element_type=F32) - fk[hs:hs + 1, 0:past]
        sn = lax.dot_general(qh, kn, NT_DIMS, preferred_element_type=F32) - fk[hs:hs + 1, past:past + seq]
        sn = jnp.where(causal, sn, NEG_BIG)
        m = jnp.maximum(jnp.max(sp, axis=1, keepdims=True), jnp.max(sn, axis=1, keepdims=True))
        pp = jnp.exp2(sp - m)
        pn = jnp.exp2(sn - m)
        l = jnp.sum(pp, axis=1, keepdims=True) + jnp.sum(pn, axis=1, keepdims=True)
        o = (jnp.dot(pp.astype(BF16), vp, preferred_element_type=F32)
             + jnp.dot(pn.astype(BF16), vn, preferred_element_type=F32))
        outs.append(o / l)
    o_ref[...] = jnp.where(lo_half, outs[0], outs[1]).astype(o_ref.dtype)


def _fox_sample(q, kb, vb, cache_k, cache_v, fcum, nb, seq, past):
    hp = FOX_HEADS // 2
    lpad = fcum.shape[-1]
    rmap = lambda b, h: (b, h)
    cmap = lambda b, h: (b, 0, h)
    return pl.pallas_call(
        functools.partial(_fox_sample_kernel, past=past, seq=seq),
        grid=(nb, hp),
        in_specs=[pl.BlockSpec((seq, LANES), rmap), pl.BlockSpec((seq, LANES), rmap),
                  pl.BlockSpec((seq, LANES), rmap),
                  pl.BlockSpec((1, past, LANES), cmap), pl.BlockSpec((1, past, LANES), cmap),
                  pl.BlockSpec((1, 1, 2, lpad), lambda b, h: (b, h, 0, 0))],
        out_specs=pl.BlockSpec((seq, LANES), rmap),
        out_shape=jax.ShapeDtypeStruct((nb * seq, FOX_WIDTH), BF16),
        compiler_params=_cparams(("parallel", "parallel")),
        name="fox_sample",
    )(q, kb, vb, cache_k, cache_v, fcum)


def _hgrn_kernel(*refs, tb, has_state):
    if has_state:
        q_ref, g_ref, i_ref, og_ref, gn_ref, tri_ref, s0_ref, y_ref, sfin_ref, st_sc, b_sc, o_sc = refs
    else:
        q_ref, g_ref, i_ref, og_ref, gn_ref, tri_ref, y_ref, sfin_ref, st_sc, b_sc, o_sc = refs
        s0_ref = None
    step = pl.program_id(1)
    nstep = pl.num_programs(1)

    @pl.when(step == 0)
    def _init():
        for h in range(HGRN_HEADS):
            if has_state:
                st_sc[h] = s0_ref[0, h].T
            else:
                st_sc[h] = jnp.zeros((HGRN_DK, HGRN_DK), F32)

    g = g_ref[...]
    tri = tri_ref[...]
    b = None
    for part in _split3(g):
        pb = jnp.dot(tri, part, preferred_element_type=F32)
        b = pb if b is None else b + pb
    b_sc[...] = b

    trow = lax.broadcasted_iota(I32, (SUB, 1), 0)

    def sub_chunk(c, carry):
        r0 = pl.multiple_of(c * SUB, SUB)
        for h in range(HGRN_HEADS):
            cs = slice(h * HGRN_DK, (h + 1) * HGRN_DK)
            q = q_ref[pl.ds(r0, SUB), cs]
            gg = g_ref[pl.ds(r0, SUB), cs]
            iv = i_ref[pl.ds(r0, SUB), cs]
            bb = b_sc[pl.ds(r0, SUB), cs]
            kk = 1.0 - jnp.exp(gg)
            st = st_sc[h]
            o = lax.dot_general((q * jnp.exp(bb)).astype(BF16), st.astype(BF16), NT_DIMS,
                                preferred_element_type=F32)
            for s in range(SUB):
                lo = 0 if s < 8 else 8
                ks = kk[s:s + 1, :]
                bs = bb[s:s + 1, :]
                ivs = iv[s:s + 1, :]
                e = jnp.exp(jnp.where(trow[lo:] >= s, bb[lo:] - bs, NEG_BIG))
                a = jnp.sum(q[lo:] * ks * e, axis=-1, keepdims=True)
                upd = a * ivs
                if lo:
                    upd = jnp.concatenate([jnp.zeros((8, HGRN_DK), F32), upd], axis=0)
                o = o + upd
            o_sc[pl.ds(r0, SUB), cs] = o
            bl = bb[SUB - 1:SUB, :]
            kd = kk * jnp.exp(bl - bb)
            u = lax.dot_general(iv.astype(BF16), kd.astype(BF16), TN_DIMS, preferred_element_type=F32)
            st_sc[h] = st * jnp.exp(bl) + u
        return carry

    lax.fori_loop(0, tb // SUB, sub_chunk, 0)

    for h in range(HGRN_HEADS):
        cs = slice(h * HGRN_DK, (h + 1) * HGRN_DK)
        o = o_sc[:, cs]
        ms = jnp.mean(o * o, axis=-1, keepdims=True)
        y_ref[:, cs] = (o * lax.rsqrt(ms + NORM_EPS) * gn_ref[...] * og_ref[:, cs]).astype(y_ref.dtype)

    @pl.when(step == nstep - 1)
    def _fin():
        for h in range(HGRN_HEADS):
            sfin_ref[0, h] = st_sc[h].T


def _hgrn(qs, gl, iv, og, gn, s0, nb, seq, tb):
    ns = seq // tb
    t = nb * seq
    row = pl.BlockSpec((tb, HGRN_WIDTH), lambda b, i: (b * ns + i, 0))
    r = jnp.arange(tb)
    tri = ((r[:, None] // SUB == r[None, :] // SUB) & (r[None, :] <= r[:, None])).astype(BF16)
    in_specs = [row, row, row, row, _const_spec((1, HGRN_DK)), _const_spec((tb, tb))]
    args = [qs, gl, iv, og, gn, tri]
    if s0 is not None:
        in_specs.append(pl.BlockSpec((1, HGRN_HEADS, HGRN_DK, HGRN_DK), lambda b, i: (b, 0, 0, 0)))
        args.append(s0)
    return pl.pallas_call(
        functools.partial(_hgrn_kernel, tb=tb, has_state=s0 is not None),
        grid=(nb, ns), in_specs=in_specs,
        out_specs=[row, pl.BlockSpec((1, HGRN_HEADS, HGRN_DK, HGRN_DK), lambda b, i: (b, 0, 0, 0))],
        out_shape=[jax.ShapeDtypeStruct((t, HGRN_WIDTH), BF16),
                   jax.ShapeDtypeStruct((nb, HGRN_HEADS, HGRN_DK, HGRN_DK), F32)],
        scratch_shapes=[pltpu.VMEM((HGRN_HEADS, HGRN_DK, HGRN_DK), F32),
                        pltpu.VMEM((tb, HGRN_WIDTH), F32), pltpu.VMEM((tb, HGRN_WIDTH), F32)],
        compiler_params=_cparams(("parallel", "arbitrary")), name="hgrn",
    )(*args)


def _post_kernel(x_ref, ya_ref, yb_ref, ga_ref, gb_ref, g1_ref, sh2_ref, sc2_ref, gn2_ref,
                 wpa_ref, wpb_ref, wo_ref, wrh_ref, wrl_ref, br_ref, tri_ref, cin_ref,
                 x1_ref, h2_ref, idx_ref, wt_ref, rank_ref, cout_ref, cnt_sc, *, tm):
    first = jnp.logical_and(pl.program_id(0) == 0, pl.program_id(1) == 0)

    @pl.when(first)
    def _init():
        cnt_sc[...] = cin_ref[...]

    merged = (ga_ref[...].astype(F32) * jnp.dot(ya_ref[...], wpa_ref[...], preferred_element_type=F32)
              + gb_ref[...].astype(F32) * jnp.dot(yb_ref[...], wpb_ref[...], preferred_element_type=F32))
    x1 = x_ref[...] + g1_ref[0] * jnp.dot(merged.astype(BF16), wo_ref[...], preferred_element_type=F32)
    x1_ref[...] = x1
    ms = jnp.mean(x1 * x1, axis=-1, keepdims=True)
    h2 = x1 * lax.rsqrt(ms + NORM_EPS) * gn2_ref[...]
    h2 = h2 * (1.0 + sc2_ref[0]) + sh2_ref[0]
    _store_rows(h2_ref, h2)

    hh = h2.astype(BF16)
    hl = (h2 - hh.astype(F32)).astype(BF16)
    wrh = wrh_ref[...]
    logits = (lax.dot_general(wrh, hh, NT_DIMS, preferred_element_type=F32)
              + lax.dot_general(wrh, hl, NT_DIMS, preferred_element_type=F32)
              + lax.dot_general(wrl_ref[...], hh, NT_DIMS, preferred_element_type=F32))
    scores = _sigmoid(logits)
    biased = scores + br_ref[...]

    b3 = biased.reshape(N_GROUPS, GROUP_SIZE, tm)
    it3 = lax.broadcasted_iota(I32, (N_GROUPS, GROUP_SIZE, tm), 1).astype(F32)
    m1 = jnp.max(b3, axis=1, keepdims=True)
    i1 = jnp.min(jnp.where(b3 == m1, it3, float(GROUP_SIZE)), axis=1, keepdims=True)
    m2 = jnp.max(jnp.where(it3 == i1, -jnp.inf, b3), axis=1, keepdims=True)
    gs = (m1 + m2).reshape(N_GROUPS, tm)

    gi = lax.broadcasted_iota(I32, (N_GROUPS, tm), 0)
    beat = jnp.zeros((N_GROUPS, tm), F32)
    for g in range(N_GROUPS):
        r = gs[g:g + 1, :]
        beat = beat + jnp.where((r > gs) | ((r == gs) & (g < gi)), 1.0, 0.0)
    gpen = jnp.where(beat < TOPK_GROUPS, 0.0, -jnp.inf)
    masked = (b3 + gpen.reshape(N_GROUPS, 1, tm)).reshape(N_EXPERTS, tm)

    ei = lax.broadcasted_iota(I32, (N_EXPERTS, tm), 0).astype(F32)
    idx_rows, w_rows = [], []
    chosen = jnp.zeros((N_EXPERTS, tm), F32)
    for _ in range(TOP_K):
        m = jnp.max(masked, axis=0, keepdims=True)
        ik = jnp.min(jnp.where(masked == m, ei, float(N_EXPERTS)), axis=0, keepdims=True)
        hit = ei == ik
        w_rows.append(jnp.sum(jnp.where(hit, scores, 0.0), axis=0, keepdims=True))
        idx_rows.append(ik)
        masked = jnp.where(hit, -jnp.inf, masked)
        chosen = jnp.where(hit, 1.0, chosen)
    idx_ref[...] = jnp.concatenate(idx_rows, axis=0).astype(I32)
    wts = jnp.concatenate(w_rows, axis=0)
    wt_ref[...] = wts / jnp.sum(wts, axis=0, keepdims=True) * ROUTED_SCALE

    before = cnt_sc[...] + jnp.dot(chosen.astype(BF16), tri_ref[...], preferred_element_type=F32)
    rank_rows = [jnp.sum(jnp.where(ei == idx_rows[kk], before, 0.0), axis=0, keepdims=True)
                 for kk in range(TOP_K)]
    rank_ref[...] = jnp.concatenate(rank_rows, axis=0).astype(I32)
    cnt_sc[...] = cnt_sc[...] + jnp.sum(chosen, axis=1, keepdims=True)
    cout_ref[...] = cnt_sc[...]


def _post(x2, ya, yb, ga, gb, gate1, shift2, scale2, prm, count_in, nb, tm):
    t, d = x2.shape
    ns = t // nb // tm
    row = lambda w: pl.BlockSpec((tm, w), lambda b, i: (b * ns + i, 0))
    colb = pl.BlockSpec((TOP_K, tm), lambda b, i: (0, b * ns + i))
    r = jnp.arange(tm)
    tri = (r[:, None] < r[None, :]).astype(BF16)
    in_specs = [row(d), row(FOX_WIDTH), row(HGRN_WIDTH), row(d), row(d),
                _mod_spec(gate1, tm, ns), _mod_spec(shift2, tm, ns), _mod_spec(scale2, tm, ns),
                _const_spec((1, d)), _const_spec((FOX_WIDTH, d)), _const_spec((HGRN_WIDTH, d)),
                _const_spec((d, d)), _const_spec((N_EXPERTS, d)), _const_spec((N_EXPERTS, d)),
                _const_spec((N_EXPERTS, 1)), _const_spec((tm, tm)), _const_spec((N_EXPERTS, 1))]
    out_shape = [jax.ShapeDtypeStruct((t, d), F32), jax.ShapeDtypeStruct((t * ROW_CHUNKS, LANES), F32),
                 jax.ShapeDtypeStruct((TOP_K, t), I32), jax.ShapeDtypeStruct((TOP_K, t), F32),
                 jax.ShapeDtypeStruct((TOP_K, t), I32), jax.ShapeDtypeStruct((N_EXPERTS, 1), F32)]
    out_specs = [row(d), pl.BlockSpec((tm * ROW_CHUNKS, LANES), lambda b, i: (b * ns + i, 0)),
                 colb, colb, colb, pl.BlockSpec((N_EXPERTS, 1), lambda b, i: (0, 0))]
    return pl.pallas_call(
        functools.partial(_post_kernel, tm=tm), grid=(nb, ns), in_specs=in_specs,
        out_specs=out_specs, out_shape=out_shape,
        scratch_shapes=[pltpu.VMEM((N_EXPERTS, 1), F32)],
        compiler_params=_cparams(("arbitrary", "arbitrary")), name="post",
    )(x2, ya, yb, ga, gb, gate1, shift2, scale2, prm["g2"], prm["wpa"], prm["wpb"], prm["wo"],
      prm["wrh"], prm["wrl"], prm["br"], tri, count_in)


def _pos_kernel(idx_ref, rank_ref, pst_ref, pos_ref, *, tm):
    ei = lax.broadcasted_iota(I32, (N_EXPERTS, tm), 0)
    pst = pst_ref[...]
    rows = [jnp.sum(jnp.where(ei == idx_ref[kk:kk + 1, :], pst, 0.0), axis=0, keepdims=True)
            for kk in range(TOP_K)]
    pos_ref[...] = jnp.concatenate(rows, axis=0).astype(I32) + rank_ref[...]


def _pos(idx, rank, pstart_col, tm):
    t = idx.shape[1]
    colb = pl.BlockSpec((TOP_K, tm), lambda i: (0, i))
    return pl.pallas_call(
        functools.partial(_pos_kernel, tm=tm), grid=(t // tm,),
        in_specs=[colb, colb, _const_spec((N_EXPERTS, 1))], out_specs=colb,
        out_shape=jax.ShapeDtypeStruct((TOP_K, t), I32),
        compiler_params=_cparams(("parallel",)), name="moe_pos",
    )(idx, rank, pstart_col)


def _padfill_kernel(cnt_ref, pad_ref, pst_ref, xs_ref, zero_sc, sem):
    zero_sc[...] = jnp.zeros(zero_sc.shape, zero_sc.dtype)
    sizes = [1 << b for b in reversed(range(MOE_BLK.bit_length() - 1))]

    def chunks(e, act):
        n = pad_ref[e] - cnt_ref[e]
        row = pst_ref[e] + cnt_ref[e]
        for size in sizes:
            @pl.when((n & size) != 0)
            def _():
                act(pltpu.make_async_copy(zero_sc.at[pl.ds(0, size)], xs_ref.at[pl.ds(row, size)], sem))

            row = row + (n & size)

    def issue(e, c):
        chunks(e, lambda cp: cp.start())
        return c

    def drain(e, c):
        chunks(e, lambda cp: cp.wait())
        return c

    lax.fori_loop(0, N_EXPERTS, issue, 0)
    lax.fori_loop(0, N_EXPERTS, drain, 0)


def _padfill(counts, padded, pstart, n_rows, d):
    row_tile = (d // LANES, LANES)
    return pl.pallas_call(
        _padfill_kernel,
        grid_spec=pltpu.PrefetchScalarGridSpec(
            num_scalar_prefetch=3, grid=(1,), in_specs=[],
            out_specs=pl.BlockSpec(memory_space=pl.ANY),
            scratch_shapes=[pltpu.VMEM((MOE_BLK // 2,) + row_tile, F32), pltpu.SemaphoreType.DMA(())]),
        out_shape=jax.ShapeDtypeStruct((n_rows,) + row_tile, F32),
        compiler_params=pltpu.CompilerParams(dimension_semantics=("arbitrary",), has_side_effects=True),
        name="moe_padfill",
    )(counts, padded, pstart)


def _dispatch_kernel(pos_ref, h2_ref, xs_in_ref, xs_ref, sem, *, tm):
    del xs_in_ref

    for t in range(tm):
        for kk in range(TOP_K):
            pltpu.make_async_copy(h2_ref.at[t], xs_ref.at[pos_ref[kk, t]], sem).start(priority=kk % 2)
    for kk in range(TOP_K):
        pltpu.make_async_copy(h2_ref, xs_ref.at[pl.ds(0, tm)], sem).wait()


def _dispatch(pos, h2_flat, xs, tm):
    h2 = h2_flat.reshape(-1, ROW_CHUNKS, LANES)
    t = h2.shape[0]
    return pl.pallas_call(
        functools.partial(_dispatch_kernel, tm=tm),
        grid=(t // tm,),
        in_specs=[pl.BlockSpec((TOP_K, tm), lambda i: (0, i), memory_space=pltpu.SMEM),
                  pl.BlockSpec((tm,) + h2.shape[1:], lambda i: (i, 0, 0)),
                  pl.BlockSpec(memory_space=pl.ANY)],
        out_specs=pl.BlockSpec(memory_space=pl.ANY),
        out_shape=jax.ShapeDtypeStruct(xs.shape, xs.dtype),
        scratch_shapes=[pltpu.SemaphoreType.DMA(())],
        input_output_aliases={2: 0},
        compiler_params=pltpu.CompilerParams(dimension_semantics=("arbitrary",), has_side_effects=True,
                                             vmem_limit_bytes=VMEM_LIMIT),
        name="moe_dispatch",
    )(pos, h2, xs)


def _moe_kernel(be_ref, nu_ref, first_ref, slot_ref, nxt_ref, x_ref, wg_ref, wu_ref, wd_ref, y_ref,
                wg_buf, wu_buf, wd_buf, wgu_sc, wd_sc, sem):
    i = pl.program_id(0)
    e = be_ref[i]

    def fetch(expert, slot):
        return (pltpu.make_async_copy(wg_ref.at[expert], wg_buf.at[slot], sem.at[slot]),
                pltpu.make_async_copy(wu_ref.at[expert], wu_buf.at[slot], sem.at[slot]),
                pltpu.make_async_copy(wd_ref.at[expert], wd_buf.at[slot], sem.at[slot]))

    @pl.when(first_ref[i] == 1)
    def _switch_expert():
        slot = slot_ref[i]

        @pl.when(i == 0)
        def _():
            for cp in fetch(e, slot):
                cp.start()

        for cp in fetch(e, slot):
            cp.wait()
        nxt = nxt_ref[e]

        @pl.when(nxt >= 0)
        def _():
            for cp in fetch(nxt, 1 - slot):
                cp.start()

        wgu_sc[:, 0:D_EXPERT] = wg_buf[slot].astype(BF16)
        wgu_sc[:, D_EXPERT:2 * D_EXPERT] = wu_buf[slot].astype(BF16)
        wd_sc[...] = wd_buf[slot].astype(BF16)

    @pl.when(i < nu_ref[0])
    def _compute():
        gu = jnp.dot(_load_rows(x_ref).astype(BF16), wgu_sc[...], preferred_element_type=F32)
        gt = gu[:, 0:D_EXPERT]
        h = gt * _sigmoid(gt) * gu[:, D_EXPERT:2 * D_EXPERT]
        _store_rows(y_ref, jnp.dot(h.astype(BF16), wd_sc[...], preferred_element_type=F32))


def _moe(blk_expert, n_used, first, slot, nxt, xs_pool, w_gate, w_up, w_down):
    n_rows = xs_pool.shape[0]
    xs = xs_pool.reshape(n_rows * ROW_CHUNKS, LANES)
    d = w_gate.shape[1]
    nblk = n_rows // MOE_BLK
    xmap = lambda i, be, nu, fi, sl, nx: (jnp.minimum(i, nu[0] - 1), 0)
    hbm = pl.BlockSpec(memory_space=pl.ANY)
    return pl.pallas_call(
        _moe_kernel,
        grid_spec=pltpu.PrefetchScalarGridSpec(
            num_scalar_prefetch=5, grid=(nblk,),
            in_specs=[pl.BlockSpec((MOE_BLK * ROW_CHUNKS, LANES), xmap), hbm, hbm, hbm],
            out_specs=pl.BlockSpec((MOE_BLK * ROW_CHUNKS, LANES), xmap),
            scratch_shapes=[pltpu.VMEM((2, d, D_EXPERT), F32), pltpu.VMEM((2, d, D_EXPERT), F32),
                            pltpu.VMEM((2, D_EXPERT, d), F32),
                            pltpu.VMEM((d, 2 * D_EXPERT), BF16), pltpu.VMEM((D_EXPERT, d), BF16),
                            pltpu.SemaphoreType.DMA((2,))]),
        out_shape=jax.ShapeDtypeStruct(xs.shape, F32),
        compiler_params=_cparams(("arbitrary",)), name="moe_experts",
    )(blk_expert, n_used, first, slot, nxt, xs, w_gate, w_up, w_down)


def _combine_kernel(pos_ref, posn_ref, x1_ref, h2_ref, g2_ref, wt_ref, wsgu_ref, wsd_ref, ys_ref, ysflat_ref,
                    o_ref, buf, sem, *, tm):
    lin = pl.program_id(0) * pl.num_programs(1) + pl.program_id(1)
    total = pl.num_programs(0) * pl.num_programs(1)
    slot = lin % 2

    def issue(p_ref, s):
        for t in range(tm):
            for kk in range(TOP_K):
                pltpu.make_async_copy(ys_ref.at[p_ref[kk, t]], buf.at[s, kk, pl.ds(t * ROW_CHUNKS, ROW_CHUNKS)],
                                      sem.at[s]).start(priority=kk % 2)

    @pl.when(lin == 0)
    def _():
        issue(pos_ref, 0)

    issue(posn_ref, 1 - slot)

    gu = jnp.dot(_load_rows(h2_ref).astype(BF16), wsgu_ref[...], preferred_element_type=F32)
    gt = gu[:, 0:D_SHARED]
    hs = gt * _sigmoid(gt) * gu[:, D_SHARED:2 * D_SHARED]
    shared = jnp.dot(hs.astype(BF16), wsd_ref[...], preferred_element_type=F32)

    for kk in range(TOP_K):
        pltpu.make_async_copy(ysflat_ref.at[pl.ds(0, tm * ROW_CHUNKS)], buf.at[slot, kk], sem.at[slot]).wait()

    nchunk = shared.shape[-1] // LANES
    acc = [shared[:, c * LANES:(c + 1) * LANES] for c in range(nchunk)]
    for kk in range(TOP_K):
        wrow = jnp.broadcast_to(wt_ref[kk:kk + 1, :], (LANES, tm))
        wcol = wrow.T
        for c in range(nchunk):
            acc[c] = acc[c] + _row_chunk(buf.at[slot, kk], c)[...] * wcol
    o_ref[...] = x1_ref[...] + g2_ref[0] * jnp.concatenate(acc, axis=1)

    @pl.when(lin + 1 == total)
    def _():
        for kk in range(TOP_K):
            pltpu.make_async_copy(ysflat_ref.at[pl.ds(0, tm * ROW_CHUNKS)], buf.at[1 - slot, kk],
                                  sem.at[1 - slot]).wait()


def _combine(pos, x1, h2, gate2, wts, wsgu, wsd, ys_flat, nb, tm):
    t, d = x1.shape
    ns = t // nb // tm
    row = pl.BlockSpec((tm, d), lambda b, i: (b * ns + i, 0))
    last = nb * ns - 1
    ys = ys_flat.reshape(-1, ROW_CHUNKS, LANES)
    return pl.pallas_call(
        functools.partial(_combine_kernel, tm=tm),
        grid=(nb, ns),
        in_specs=[pl.BlockSpec((TOP_K, tm), lambda b, i: (0, b * ns + i), memory_space=pltpu.SMEM),
                  pl.BlockSpec((TOP_K, tm), lambda b, i: (0, jnp.minimum(b * ns + i + 1, last)),
                               memory_space=pltpu.SMEM),
                  row, pl.BlockSpec((tm * ROW_CHUNKS, LANES), lambda b, i: (b * ns + i, 0)),
                  _mod_spec(gate2, tm, ns),
                  pl.BlockSpec((TOP_K, tm), lambda b, i: (0, b * ns + i)),
                  _const_spec(wsgu.shape), _const_spec(wsd.shape),
                  pl.BlockSpec(memory_space=pl.ANY), pl.BlockSpec(memory_space=pl.ANY)],
        out_specs=row,
        out_shape=jax.ShapeDtypeStruct((t, d), F32),
        scratch_shapes=[pltpu.VMEM((2, TOP_K, tm * ROW_CHUNKS, LANES), F32), pltpu.SemaphoreType.DMA((2,))],
        compiler_params=_cparams(("arbitrary", "arbitrary")), name="moe_combine",
    )(pos, pos, x1, h2, gate2, wts, wsgu, wsd, ys, ys_flat)


def _prepare_params(g_norm1, w_in, b_fox_f, g_q, g_k, hgrn_lb, g_hgrn_o, w_proj_a, w_proj_b, w_out,
                    g_norm2, w_router, b_router, w_sh_gate, w_sh_up, w_sh_down):
    d = D_MODEL
    fw, hw = FOX_WIDTH, HGRN_WIDTH
    c0 = 3 * fw
    c1 = c0 + FOX_HEADS
    c2 = c1 + 4 * hw
    wff = w_in[:, c0:c1]
    head = jnp.arange(fw) // FOX_HEAD_DIM
    wr_t = w_router.T
    wrh = wr_t.astype(BF16)
    return dict(
        g1=g_norm1.reshape(1, d),
        wqkv=w_in[:, :c0].astype(BF16),
        wff=jnp.pad(wff, ((0, 0), (0, LANES - FOX_HEADS))).astype(BF16),
        wfft=jnp.pad(wff.T, ((0, 16 - FOX_HEADS), (0, 0))).astype(BF16),
        bf=b_fox_f.reshape(1, FOX_HEADS), bft=b_fox_f.reshape(FOX_HEADS, 1),
        gq=jnp.tile(g_q, FOX_HEADS).reshape(1, fw), gk=jnp.tile(g_k, FOX_HEADS).reshape(1, fw),
        bd=(head[:, None] == head[None, :]).astype(BF16) * (1.0 / FOX_HEAD_DIM),
        lbp=hgrn_lb,
        wh=w_in[:, c1:c2].astype(BF16), wg=w_in[:, c2:].astype(BF16),
        gn=g_hgrn_o.reshape(1, HGRN_DK),
        g2=g_norm2.reshape(1, d),
        wpa=w_proj_a.astype(BF16), wpb=w_proj_b.astype(BF16), wo=w_out.astype(BF16),
        wrh=wrh, wrl=(wr_t - wrh.astype(F32)).astype(BF16), br=b_router.reshape(N_EXPERTS, 1),
        wsgu=jnp.concatenate([w_sh_gate, w_sh_up], axis=1).astype(BF16), wsd=w_sh_down.astype(BF16),
    )


def _mixers(x2, nb, seq, tm, shift1, scale1, prm, past, tile, tb):
    (q, k, kb, v, vb, lf, lft, hq, hl, hi, og, ga, gb) = _inproj(x2, nb if shift1.shape[1] == 1 else 1, tm,
                                                                 shift1, scale1, prm)
    t = nb * seq
    hp = FOX_HEADS // 2
    lft = lft.transpose(1, 0, 2).reshape(FOX_HEADS, nb, seq).transpose(1, 0, 2)
    if past is None:
        fcum = _cumsum_lanes(lft.reshape(nb * FOX_HEADS, seq)).reshape(nb, hp, 2, seq)
        ya = _fox_prompt(q, kb, vb, fcum, nb, seq, tile, min(32, tile))
        s0 = None
    else:
        cache_k, cache_v, cache_lf, s0 = past
        plen = cache_k.shape[1]
        ltot = plen + seq
        lpad = -(-ltot // LANES) * LANES
        lf_all = jnp.concatenate([cache_lf.transpose(0, 2, 1), lft,
                                  jnp.zeros((nb, FOX_HEADS, lpad - ltot), F32)], axis=-1)
        fcum = _cumsum_lanes(lf_all.reshape(nb * FOX_HEADS, lpad)).reshape(nb, hp, 2, lpad)
        ya = _fox_sample(q, kb, vb, cache_k.reshape(nb, plen, FOX_WIDTH), cache_v.reshape(nb, plen, FOX_WIDTH),
                         fcum, nb, seq, plen)
    yb, sfin = _hgrn(hq, hl, hi, og, prm["gn"], s0, nb, seq, tb)
    return k, v, lf, sfin, ya, yb, ga, gb


def kernel(x_prompt, x_sample, cache_fox_k, cache_fox_v, cache_fox_logf, state_hgrn, c_prompt, c_sample,
           w_ada, b_ada, g_norm1, w_in, b_fox_f, g_q, g_k, hgrn_lb, g_hgrn_o, w_proj_a, w_proj_b, w_out,
           g_norm2, w_router, b_router, w_exp_gate, w_exp_up, w_exp_down, w_sh_gate, w_sh_up, w_sh_down):
    assert w_ada.shape[0] == 1 and hgrn_lb.shape[0] == 2, "single-layer trunk"
    d = D_MODEL
    bp, sp, _ = x_prompt.shape
    bs, ss, _ = x_sample.shape
    tp, ts = bp * sp, bs * ss
    prm = _prepare_params(g_norm1[0], w_in[0], b_fox_f[0], g_q[0], g_k[0], hgrn_lb, g_hgrn_o[0],
                          w_proj_a[0], w_proj_b[0], w_out[0], g_norm2[0], w_router[0], b_router[0],
                          w_sh_gate[0], w_sh_up[0], w_sh_down[0])

    bc = bp + bs
    bc_pad = -(-bc // 8) * 8
    c_all = jnp.concatenate([c_prompt, c_sample, jnp.zeros((bc_pad - bc, d), F32)], axis=0)
    mod = _ada(c_all, w_ada[0], b_ada[0])
    mod_p = [mod[:bp, j * d:(j + 1) * d].reshape(bp, 1, d) for j in range(6)]
    mod_s = [jnp.repeat(mod[bp:bc, j * d:(j + 1) * d], ss, axis=0).reshape(1, ts, d) for j in range(6)]

    tm_p = min(512, sp)
    tm_s = min(256, ts)
    xp2 = x_prompt.reshape(tp, d)
    xs2 = x_sample.reshape(ts, d)
    tile = min(1024, sp)
    tb = min(256, sp)

    kp, vp, lfp, sfin_p, ya_p, yb_p, ga_p, gb_p = _mixers(
        xp2, bp, sp, tm_p, mod_p[0], mod_p[1], prm, None, tile, tb)
    past = (cache_fox_k[0], cache_fox_v[0], cache_fox_logf[0], state_hgrn[0])
    ks, vs, lfs, sfin_s, ya_s, yb_s, ga_s, gb_s = _mixers(
        xs2, bs, ss, tm_s, mod_s[0], mod_s[1], prm, past, None, min(256, ss))

    zero_cnt = jnp.zeros((N_EXPERTS, 1), F32)
    x1_p, h2_p, idx_p, wt_p, rank_p, cnt_p = _post(xp2, ya_p, yb_p, ga_p, gb_p, mod_p[2], mod_p[3], mod_p[4],
                                                    prm, zero_cnt, bp, tm_p)
    x1_s, h2_s, idx_s, wt_s, rank_s, cnt_all = _post(xs2, ya_s, yb_s, ga_s, gb_s, mod_s[2], mod_s[3], mod_s[4],
                                                     prm, cnt_p, 1, tm_s)

    counts = cnt_all.reshape(N_EXPERTS).astype(I32)
    padded = (counts + MOE_BLK - 1) // MOE_BLK * MOE_BLK
    pend = jnp.cumsum(padded)
    pstart = pend - padded
    nblk = -(-((tp + ts) * TOP_K + N_EXPERTS * (MOE_BLK - 1)) // MOE_BLK)
    n_used = (pend[-1] // MOE_BLK).reshape(1)
    blk_row0 = jnp.arange(nblk, dtype=I32) * MOE_BLK
    blk_expert = jnp.minimum(jnp.sum((pend[None, :] <= blk_row0[:, None]).astype(I32), axis=1), N_EXPERTS - 1)
    blk_used = jnp.arange(nblk) < n_used[0]
    blk_expert = jnp.where(blk_used, blk_expert, blk_expert[jnp.maximum(n_used[0] - 1, 0)])
    blk_first = blk_used & jnp.concatenate([jnp.ones((1,), bool), blk_expert[1:] != blk_expert[:-1]])
    blk_slot = ((jnp.cumsum(blk_first.astype(I32)) - 1) & 1).astype(I32)
    eids = jnp.arange(N_EXPERTS, dtype=I32)
    later = (eids[None, :] > eids[:, None]) & (padded[None, :] > 0)
    nxt_expert = jnp.min(jnp.where(later, eids[None, :], N_EXPERTS), axis=1)
    nxt_expert = jnp.where(nxt_expert < N_EXPERTS, nxt_expert, -1).astype(I32)
    pstart_col = pstart.astype(F32).reshape(N_EXPERTS, 1)
    pos_p = _pos(idx_p, rank_p, pstart_col, tm_p)
    pos_s = _pos(idx_s, rank_s, pstart_col, tm_s)

    xs_pool = _padfill(counts, padded, pstart, nblk * MOE_BLK, d)
    xs_pool = _dispatch(pos_p, h2_p, xs_pool, min(128, tm_p))
    xs_pool = _dispatch(pos_s, h2_s, xs_pool, min(128, tm_s))
    ys_pool = _moe(blk_expert, n_used, blk_first.astype(I32), blk_slot, nxt_expert, xs_pool,
                   w_exp_gate[0], w_exp_up[0], w_exp_down[0])

    tm_c = min(128, sp)
    y_p = _combine(pos_p, x1_p, h2_p, mod_p[5], wt_p, prm["wsgu"], prm["wsd"], ys_pool, bp, tm_c)
    y_s = _combine(pos_s, x1_s, h2_s, mod_s[5], wt_s, prm["wsgu"], prm["wsd"], ys_pool, 1, min(128, ts))

    return (y_p.reshape(bp, sp, d), y_s.reshape(bs, ss, d),
            kp.reshape(1, bp, sp, FOX_HEADS, FOX_HEAD_DIM), vp.reshape(1, bp, sp, FOX_HEADS, FOX_HEAD_DIM),
            lfp.reshape(1, bp, sp, FOX_HEADS), sfin_p[None],
            ks.reshape(1, bs, ss, FOX_HEADS, FOX_HEAD_DIM), vs.reshape(1, bs, ss, FOX_HEADS, FOX_HEAD_DIM),
            lfs.reshape(1, bs, ss, FOX_HEADS), sfin_s[None])
```

```python
import functools

import jax
import jax.numpy as jnp
from jax import lax
from jax.experimental import pallas as pl
from jax.experimental.pallas import tpu as pltpu

F32 = jnp.float32
BF16 = jnp.bfloat16
I32 = jnp.int32

D_MODEL = 1024
FOX_HEADS = 8
FOX_HEAD_DIM = 64
FOX_WIDTH = FOX_HEADS * FOX_HEAD_DIM
HGRN_HEADS = 4
HGRN_DK = 128
HGRN_WIDTH = HGRN_HEADS * HGRN_DK
N_EXPERTS = 256
TOP_K = 8
N_GROUPS = 8
TOPK_GROUPS = 4
GROUP_SIZE = N_EXPERTS // N_GROUPS
D_EXPERT = 256
D_SHARED = 256
ROUTED_SCALE = 2.5
NORM_EPS = 1e-6
NEG_BIG = -1e30
LOG2E = 1.4426950408889634
QK_SCALE = FOX_HEAD_DIM ** -0.5 * LOG2E

LANES = 128
SUB = 16
MOE_BLK = 512
VMEM_LIMIT = 56 * 1024 * 1024

NT_DIMS = (((1,), (1,)), ((), ()))
TN_DIMS = (((0,), (0,)), ((), ()))


def _cparams(sem):
    return pltpu.CompilerParams(dimension_semantics=sem, vmem_limit_bytes=VMEM_LIMIT)


def _const_spec(shape):
    nd = len(shape)
    return pl.BlockSpec(shape, lambda *_: (0,) * nd, pipeline_mode=pl.Buffered(1))


def _sigmoid(z):
    return 1.0 / (1.0 + jnp.exp(-z))


def _log_sigmoid(z):
    return jnp.minimum(z, 0.0) - jnp.log(1.0 + jnp.exp(-jnp.abs(z)))


ROW_CHUNKS = D_MODEL // LANES


def _row_chunk(ref, c):
    return ref.at[pl.ds(c, ref.shape[0] // ROW_CHUNKS, stride=ROW_CHUNKS), :]


def _store_rows(ref, val):
    for c in range(ROW_CHUNKS):
        _row_chunk(ref, c)[...] = val[:, c * LANES:(c + 1) * LANES]


def _load_rows(ref):
    return jnp.concatenate([_row_chunk(ref, c)[...] for c in range(ROW_CHUNKS)], axis=1)


def _split3(a):
    hi = a.astype(BF16)
    r1 = a - hi.astype(F32)
    mid = r1.astype(BF16)
    lo = (r1 - mid.astype(F32)).astype(BF16)
    return hi, mid, lo


def _ada_kernel(c_ref, w_ref, b_ref, o_ref):
    c = c_ref[...]
    s = c * _sigmoid(c)
    o_ref[...] = jnp.dot(s.astype(BF16), w_ref[...].astype(BF16), preferred_element_type=F32) + b_ref[...]


def _ada(c_all, w_ada, b_ada):
    bc, d = c_all.shape
    n = w_ada.shape[1]
    tn = 1024
    return pl.pallas_call(
        _ada_kernel,
        grid=(n // tn,),
        in_specs=[pl.BlockSpec((bc, d), lambda j: (0, 0)),
                  pl.BlockSpec((d, tn), lambda j: (0, j)),
                  pl.BlockSpec((1, tn), lambda j: (0, j))],
        out_specs=pl.BlockSpec((bc, tn), lambda j: (0, j)),
        out_shape=jax.ShapeDtypeStruct((bc, n), F32),
        compiler_params=_cparams(("parallel",)),
        name="ada",
    )(c_all, w_ada, b_ada.reshape(1, n))


def _inproj_kernel(x_ref, sh_ref, sc_ref, g1_ref, wqkv_ref, wff_ref, wfft_ref, bf_ref, bft_ref,
                   gq_ref, gk_ref, bd_ref, lbp_ref, wh_ref, wg_ref,
                   q_ref, k_ref, kb_ref, v_ref, vb_ref, lf_ref, lft_ref,
                   hq_ref, hl_ref, hi_ref, og_ref, ga_ref, gb_ref):
    x = x_ref[...]
    ms = jnp.mean(x * x, axis=-1, keepdims=True)
    h = x * lax.rsqrt(ms + NORM_EPS) * g1_ref[...]
    h = h * (1.0 + sc_ref[0]) + sh_ref[0]
    hb = h.astype(BF16)

    def headnorm(a, g):
        ss = jnp.dot((a * a).astype(BF16), bd_ref[...], preferred_element_type=F32)
        return a * lax.rsqrt(ss + NORM_EPS) * g

    fq = jnp.dot(hb, wqkv_ref[:, 0:FOX_WIDTH], preferred_element_type=F32)
    q_ref[...] = (headnorm(fq, gq_ref[...]) * QK_SCALE).astype(BF16)
    fk = jnp.dot(hb, wqkv_ref[:, FOX_WIDTH:2 * FOX_WIDTH], preferred_element_type=F32)
    k = headnorm(fk, gk_ref[...])
    kb_ref[...] = k.astype(BF16)
    fv = jnp.dot(hb, wqkv_ref[:, 2 * FOX_WIDTH:3 * FOX_WIDTH], preferred_element_type=F32)
    vb_ref[...] = fv.astype(BF16)
    for hd in range(FOX_HEADS):
        cols = slice(hd * FOX_HEAD_DIM, (hd + 1) * FOX_HEAD_DIM)
        k_ref[:, hd, :] = k[:, cols]
        v_ref[:, hd, :] = fv[:, cols]

    ff = jnp.dot(hb, wff_ref[...], preferred_element_type=F32)
    lf_ref[...] = _log_sigmoid(ff[:, 0:FOX_HEADS] + bf_ref[...])
    fft = lax.dot_general(wfft_ref[...], hb, NT_DIMS, preferred_element_type=F32)
    lft_ref[0] = _log_sigmoid(fft[0:FOX_HEADS, :] + bft_ref[...])

    lbp = lbp_ref[...]
    e = jnp.exp(lbp - jnp.max(lbp, axis=0, keepdims=True))
    lb = e[0:1, :] / jnp.sum(e, axis=0, keepdims=True)

    w = HGRN_WIDTH
    hq = jnp.dot(hb, wh_ref[:, 0:w], preferred_element_type=F32)
    hq_ref[...] = hq * _sigmoid(hq)
    hf = jnp.dot(hb, wh_ref[:, w:2 * w], preferred_element_type=F32)
    hl_ref[...] = jnp.log(lb + (1.0 - lb) * _sigmoid(hf))
    hi_ref[...] = jnp.dot(hb, wh_ref[:, 2 * w:3 * w], preferred_element_type=F32)
    hg = jnp.dot(hb, wh_ref[:, 3 * w:4 * w], preferred_element_type=F32)
    og_ref[...] = _sigmoid(hg)
    ga = jnp.dot(hb, wg_ref[:, 0:D_MODEL], preferred_element_type=F32)
    ga_ref[...] = _sigmoid(ga).astype(BF16)
    gb = jnp.dot(hb, wg_ref[:, D_MODEL:2 * D_MODEL], preferred_element_type=F32)
    gb_ref[...] = _sigmoid(gb).astype(BF16)


def _mod_spec(arr, tm, ns):
    d = arr.shape[-1]
    if arr.shape[1] == 1:
        return pl.BlockSpec((1, 1, d), lambda b, i: (b, 0, 0))
    return pl.BlockSpec((1, tm, d), lambda b, i: (0, b * ns + i, 0))


def _inproj(x2, nb, tm, shift1, scale1, prm):
    t, d = x2.shape
    ns = t // nb // tm
    row = lambda w: pl.BlockSpec((tm, w), lambda b, i: (b * ns + i, 0))
    in_specs = [row(d), _mod_spec(shift1, tm, ns), _mod_spec(scale1, tm, ns),
                _const_spec((1, d)), _const_spec(prm["wqkv"].shape), _const_spec(prm["wff"].shape),
                _const_spec(prm["wfft"].shape), _const_spec((1, FOX_HEADS)), _const_spec((FOX_HEADS, 1)),
                _const_spec((1, FOX_WIDTH)), _const_spec((1, FOX_WIDTH)), _const_spec((FOX_WIDTH, FOX_WIDTH)),
                _const_spec(prm["lbp"].shape), _const_spec(prm["wh"].shape), _const_spec(prm["wg"].shape)]
    fw, hw = FOX_WIDTH, HGRN_WIDTH
    heads = pl.BlockSpec((tm, FOX_HEADS, FOX_HEAD_DIM), lambda b, i: (b * ns + i, 0, 0))
    out_shape = [jax.ShapeDtypeStruct((t, fw), BF16),
                 jax.ShapeDtypeStruct((t, FOX_HEADS, FOX_HEAD_DIM), F32),
                 jax.ShapeDtypeStruct((t, fw), BF16),
                 jax.ShapeDtypeStruct((t, FOX_HEADS, FOX_HEAD_DIM), F32),
                 jax.ShapeDtypeStruct((t, fw), BF16),
                 jax.ShapeDtypeStruct((t, FOX_HEADS), F32),
                 jax.ShapeDtypeStruct((nb * ns, FOX_HEADS, tm), F32),
                 jax.ShapeDtypeStruct((t, hw), F32),
                 jax.ShapeDtypeStruct((t, hw), F32),
                 jax.ShapeDtypeStruct((t, hw), F32),
                 jax.ShapeDtypeStruct((t, hw), F32),
                 jax.ShapeDtypeStruct((t, d), BF16),
                 jax.ShapeDtypeStruct((t, d), BF16)]
    out_specs = [row(fw), heads, row(fw), heads, row(fw), row(FOX_HEADS),
                 pl.BlockSpec((1, FOX_HEADS, tm), lambda b, i: (b * ns + i, 0, 0)),
                 row(hw), row(hw), row(hw), row(hw), row(d), row(d)]
    return pl.pallas_call(
        _inproj_kernel, grid=(nb, ns), in_specs=in_specs, out_specs=out_specs, out_shape=out_shape,
        compiler_params=_cparams(("parallel", "parallel")), name="inproj",
    )(x2, shift1, scale1, prm["g1"], prm["wqkv"], prm["wff"], prm["wfft"], prm["bf"], prm["bft"],
      prm["gq"], prm["gk"], prm["bd"], prm["lbp"], prm["wh"], prm["wg"])


def _cumsum_kernel(x_ref, o_ref):
    x = x_ref[...]
    n = x.shape[-1]
    lane = lax.broadcasted_iota(I32, x.shape, 1)
    s = 1
    while s < n:
        x = x + jnp.where(lane >= s, pltpu.roll(x, s, axis=1), 0.0)
        s *= 2
    o_ref[...] = x * LOG2E


def _cumsum_lanes(x):
    return pl.pallas_call(
        _cumsum_kernel, out_shape=jax.ShapeDtypeStruct(x.shape, F32),
        compiler_params=pltpu.CompilerParams(vmem_limit_bytes=VMEM_LIMIT), name="cumsum",
    )(x)


def _fox_prompt_kernel(qt_ref, kt_ref, q_ref, k_ref, v_ref, fk_ref, o_ref,
                       m0, m1, a0, a1, s0, s1, p0, p1, *, tile, rsub):
    step = pl.program_id(2)
    qi = qt_ref[step]
    ki = kt_ref[step]
    heads = ((m0, a0, s0, p0), (m1, a1, s1, p1))
    lane = lax.broadcasted_iota(I32, (1, LANES), 1)
    lo_half = lane < FOX_HEAD_DIM

    @pl.when(ki == 0)
    def _init():
        for m_sc, a_sc, _, _ in heads:
            m_sc[...] = jnp.full(m_sc.shape, NEG_BIG, F32)
            a_sc[...] = jnp.zeros(a_sc.shape, F32)

    def tile_update(diagonal):
        q = q_ref[...]
        k = k_ref[...]
        v = v_ref[...]
        fk = fk_ref[0, 0]
        nsub = tile // rsub
        if diagonal:
            ahead = (lax.broadcasted_iota(I32, (rsub, tile), 1)
                     - lax.broadcasted_iota(I32, (rsub, tile), 0))
        for hs, (_, _, s_sc, _) in enumerate(heads):
            sel = lo_half if hs == 0 else jnp.logical_not(lo_half)
            qh = jnp.where(sel, q, jnp.zeros_like(q))
            s_sc[...] = lax.dot_general(qh, k, NT_DIMS, preferred_element_type=F32)
        for hs, (m_sc, a_sc, s_sc, p_sc) in enumerate(heads):
            sel = lo_half if hs == 0 else jnp.logical_not(lo_half)
            v_ones = jnp.where(sel, v, jnp.ones_like(v))

            def biased(r):
                rows = slice(r * rsub, (r + 1) * rsub)
                cw = min(tile, -(-((r + 1) * rsub) // LANES) * LANES) if diagonal else tile
                s = s_sc[rows, 0:cw] - fk[hs:hs + 1, 0:cw]
                if diagonal:
                    s = jnp.where(ahead[:, 0:cw] <= r * rsub, s, NEG_BIG)
                return s, rows, cw

            m_old = m_sc[...]
            m_parts = []
            for r in range(nsub):
                s, rows, _ = biased(r)
                m_parts.append(jnp.maximum(m_old[rows], jnp.max(s, axis=1, keepdims=True)))
            for r in range(nsub):
                s, rows, cw = biased(r)
                nc = cw // LANES
                p = jnp.exp2(s - jnp.concatenate([m_parts[r]] * nc, axis=1))
                p_sc[rows, 0:cw] = p.astype(BF16)
                if cw < tile:
                    p_sc[rows, cw:tile] = jnp.zeros((rsub, tile - cw), BF16)
            m_new = jnp.concatenate(m_parts, axis=0)
            alpha = jnp.exp2(m_old - m_new)
            m_sc[...] = m_new
            a_sc[...] = alpha * a_sc[...] + jnp.dot(p_sc[...], v_ones, preferred_element_type=F32)

    @pl.when(ki < qi)
    def _full():
        tile_update(False)

    @pl.when(ki == qi)
    def _diag():
        tile_update(True)
        acc0 = a0[...]
        acc1 = a1[...]
        o0 = acc0 / pltpu.roll(acc0, FOX_HEAD_DIM, axis=1)
        o1 = acc1 / pltpu.roll(acc1, FOX_HEAD_DIM, axis=1)
        o_ref[...] = jnp.where(lo_half, o0, o1).astype(o_ref.dtype)


def _fox_prompt(q, kb, vb, fcum, nb, seq, tile, rsub):
    nt = seq // tile
    hp = FOX_HEADS // 2
    qt = jnp.asarray([qi for qi in range(nt) for _ in range(qi + 1)], I32)
    kt = jnp.asarray([ki for qi in range(nt) for ki in range(qi + 1)], I32)
    qmap = lambda b, h, s, qt, kt: (b * nt + qt[s], h)
    kmap = lambda b, h, s, qt, kt: (b * nt + kt[s], h)
    fmap = lambda b, h, s, qt, kt: (b, h, 0, kt[s])
    stat = pltpu.VMEM((tile, LANES), F32)
    return pl.pallas_call(
        functools.partial(_fox_prompt_kernel, tile=tile, rsub=rsub),
        grid_spec=pltpu.PrefetchScalarGridSpec(
            num_scalar_prefetch=2, grid=(nb, hp, nt * (nt + 1) // 2),
            in_specs=[pl.BlockSpec((tile, LANES), qmap), pl.BlockSpec((tile, LANES), kmap),
                      pl.BlockSpec((tile, LANES), kmap), pl.BlockSpec((1, 1, 2, tile), fmap)],
            out_specs=pl.BlockSpec((tile, LANES), qmap),
            scratch_shapes=[stat, stat, stat, stat,
                            pltpu.VMEM((tile, tile), F32), pltpu.VMEM((tile, tile), F32),
                            pltpu.VMEM((tile, tile), BF16), pltpu.VMEM((tile, tile), BF16)]),
        out_shape=jax.ShapeDtypeStruct((nb * seq, FOX_WIDTH), BF16),
        compiler_params=_cparams(("parallel", "parallel", "arbitrary")),
        name="fox_prompt",
    )(qt, kt, q, kb, vb, fcum)


def _fox_sample_kernel(q_ref, kn_ref, vn_ref, kp_ref, vp_ref, fk_ref, o_ref, *, past, seq):
    lane = lax.broadcasted_iota(I32, (1, LANES), 1)
    lo_half = lane < FOX_HEAD_DIM
    q = q_ref[...]
    kn = kn_ref[...]
    vn = vn_ref[...]
    kp = kp_ref[0].astype(BF16)
    vp = vp_ref[0].astype(BF16)
    fk = fk_ref[0, 0]
    row = lax.broadcasted_iota(I32, (seq, seq), 0)
    col = lax.broadcasted_iota(I32, (seq, seq), 1)
    causal = col <= row
    outs = []
    for hs in range(2):
        sel = lo_half if hs == 0 else jnp.logical_not(lo_half)
        qh = jnp.where(sel, q, jnp.zeros_like(q))
        sp = lax.dot_general(qh, kp, NT_DIMS, preferred_element_type=F32) - fk[hs:hs + 1, 0:past]
        sn = lax.dot_general(qh, kn, NT_DIMS, preferred_element_type=F32) - fk[hs:hs + 1, past:past + seq]
        sn = jnp.where(causal, sn, NEG_BIG)
        m = jnp.maximum(jnp.max(sp, axis=1, keepdims=True), jnp.max(sn, axis=1, keepdims=True))
        pp = jnp.exp2(sp - m)
        pn = jnp.exp2(sn - m)
        l = jnp.sum(pp, axis=1, keepdims=True) + jnp.sum(pn, axis=1, keepdims=True)
        o = (jnp.dot(pp.astype(BF16), vp, preferred_element_type=F32)
             + jnp.dot(pn.astype(BF16), vn, preferred_element_type=F32))
        outs.append(o / l)
    o_ref[...] = jnp.where(lo_half, outs[0], outs[1]).astype(o_ref.dtype)


def _fox_sample(q, kb, vb, cache_k, cache_v, fcum, nb, seq, past):
    hp = FOX_HEADS // 2
    lpad = fcum.shape[-1]
    rmap = lambda b, h: (b, h)
    cmap = lambda b, h: (b, 0, h)
    return pl.pallas_call(
        functools.partial(_fox_sample_kernel, past=past, seq=seq),
        grid=(nb, hp),
        in_specs=[pl.BlockSpec((seq, LANES), rmap), pl.BlockSpec((seq, LANES), rmap),
                  pl.BlockSpec((seq, LANES), rmap),
                  pl.BlockSpec((1, past, LANES), cmap), pl.BlockSpec((1, past, LANES), cmap),
                  pl.BlockSpec((1, 1, 2, lpad), lambda b, h: (b, h, 0, 0))],
        out_specs=pl.BlockSpec((seq, LANES), rmap),
        out_shape=jax.ShapeDtypeStruct((nb * seq, FOX_WIDTH), BF16),
        compiler_params=_cparams(("parallel", "parallel")),
        name="fox_sample",
    )(q, kb, vb, cache_k, cache_v, fcum)


def _hgrn_kernel(*refs, tb, has_state):
    if has_state:
        q_ref, g_ref, i_ref, og_ref, gn_ref, tri_ref, s0_ref, y_ref, sfin_ref, st_sc, b_sc, o_sc = refs
    else:
        q_ref, g_ref, i_ref, og_ref, gn_ref, tri_ref, y_ref, sfin_ref, st_sc, b_sc, o_sc = refs
        s0_ref = None
    step = pl.program_id(1)
    nstep = pl.num_programs(1)

    @pl.when(step == 0)
    def _init():
        for h in range(HGRN_HEADS):
            if has_state:
                st_sc[h] = s0_ref[0, h].T
            else:
                st_sc[h] = jnp.zeros((HGRN_DK, HGRN_DK), F32)

    g = g_ref[...]
    tri = tri_ref[...]
    b = None
    for part in _split3(g):
        pb = jnp.dot(tri, part, preferred_element_type=F32)
        b = pb if b is None else b + pb
    b_sc[...] = b

    trow = lax.broadcasted_iota(I32, (SUB, 1), 0)

    def sub_chunk(c, carry):
        r0 = pl.multiple_of(c * SUB, SUB)
        for h in range(HGRN_HEADS):
            cs = slice(h * HGRN_DK, (h + 1) * HGRN_DK)
            q = q_ref[pl.ds(r0, SUB), cs]
            gg = g_ref[pl.ds(r0, SUB), cs]
            iv = i_ref[pl.ds(r0, SUB), cs]
            bb = b_sc[pl.ds(r0, SUB), cs]
            kk = 1.0 - jnp.exp(gg)
            st = st_sc[h]
            o = lax.dot_general((q * jnp.exp(bb)).astype(BF16), st.astype(BF16), NT_DIMS,
                                preferred_element_type=F32)
            for s in range(SUB):
                lo = 0 if s < 8 else 8
                ks = kk[s:s + 1, :]
                bs = bb[s:s + 1, :]
                ivs = iv[s:s + 1, :]
                e = jnp.exp(jnp.where(trow[lo:] >= s, bb[lo:] - bs, NEG_BIG))
                a = jnp.sum(q[lo:] * ks * e, axis=-1, keepdims=True)
                upd = a * ivs
                if lo:
                    upd = jnp.concatenate([jnp.zeros((8, HGRN_DK), F32), upd], axis=0)
                o = o + upd
            o_sc[pl.ds(r0, SUB), cs] = o
            bl = bb[SUB - 1:SUB, :]
            kd = kk * jnp.exp(bl - bb)
            u = lax.dot_general(iv.astype(BF16), kd.astype(BF16), TN_DIMS, preferred_element_type=F32)
            st_sc[h] = st * jnp.exp(bl) + u
        return carry

    lax.fori_loop(0, tb // SUB, sub_chunk, 0)

    for h in range(HGRN_HEADS):
        cs = slice(h * HGRN_DK, (h + 1) * HGRN_DK)
        o = o_sc[:, cs]
        ms = jnp.mean(o * o, axis=-1, keepdims=True)
        y_ref[:, cs] = (o * lax.rsqrt(ms + NORM_EPS) * gn_ref[...] * og_ref[:, cs]).astype(y_ref.dtype)

    @pl.when(step == nstep - 1)
    def _fin():
        for h in range(HGRN_HEADS):
            sfin_ref[0, h] = st_sc[h].T


def _hgrn(qs, gl, iv, og, gn, s0, nb, seq, tb):
    ns = seq // tb
    t = nb * seq
    row = pl.BlockSpec((tb, HGRN_WIDTH), lambda b, i: (b * ns + i, 0))
    r = jnp.arange(tb)
    tri = ((r[:, None] // SUB == r[None, :] // SUB) & (r[None, :] <= r[:, None])).astype(BF16)
    in_specs = [row, row, row, row, _const_spec((1, HGRN_DK)), _const_spec((tb, tb))]
    args = [qs, gl, iv, og, gn, tri]
    if s0 is not None:
        in_specs.append(pl.BlockSpec((1, HGRN_HEADS, HGRN_DK, HGRN_DK), lambda b, i: (b, 0, 0, 0)))
        args.append(s0)
    return pl.pallas_call(
        functools.partial(_hgrn_kernel, tb=tb, has_state=s0 is not None),
        grid=(nb, ns), in_specs=in_specs,
        out_specs=[row, pl.BlockSpec((1, HGRN_HEADS, HGRN_DK, HGRN_DK), lambda b, i: (b, 0, 0, 0))],
        out_shape=[jax.ShapeDtypeStruct((t, HGRN_WIDTH), BF16),
                   jax.ShapeDtypeStruct((nb, HGRN_HEADS, HGRN_DK, HGRN_DK), F32)],
        scratch_shapes=[pltpu.VMEM((HGRN_HEADS, HGRN_DK, HGRN_DK), F32),
                        pltpu.VMEM((tb, HGRN_WIDTH), F32), pltpu.VMEM((tb, HGRN_WIDTH), F32)],
        compiler_params=_cparams(("parallel", "arbitrary")), name="hgrn",
    )(*args)


def _post_kernel(x_ref, ya_ref, yb_ref, ga_ref, gb_ref, g1_ref, sh2_ref, sc2_ref, gn2_ref,
                 wpa_ref, wpb_ref, wo_ref, wrh_ref, wrl_ref, br_ref, tri_ref, cin_ref,
                 x1_ref, h2_ref, idx_ref, wt_ref, rank_ref, cout_ref, cnt_sc, *, tm):
    first = jnp.logical_and(pl.program_id(0) == 0, pl.program_id(1) == 0)

    @pl.when(first)
    def _init():
        cnt_sc[...] = cin_ref[...]

    merged = (ga_ref[...].astype(F32) * jnp.dot(ya_ref[...], wpa_ref[...], preferred_element_type=F32)
              + gb_ref[...].astype(F32) * jnp.dot(yb_ref[...], wpb_ref[...], preferred_element_type=F32))
    x1 = x_ref[...] + g1_ref[0] * jnp.dot(merged.astype(BF16), wo_ref[...], preferred_element_type=F32)
    x1_ref[...] = x1
    ms = jnp.mean(x1 * x1, axis=-1, keepdims=True)
    h2 = x1 * lax.rsqrt(ms + NORM_EPS) * gn2_ref[...]
    h2 = h2 * (1.0 + sc2_ref[0]) + sh2_ref[0]
    _store_rows(h2_ref, h2)

    hh = h2.astype(BF16)
    hl = (h2 - hh.astype(F32)).astype(BF16)
    wrh = wrh_ref[...]
    logits = (lax.dot_general(wrh, hh, NT_DIMS, preferred_element_type=F32)
              + lax.dot_general(wrh, hl, NT_DIMS, preferred_element_type=F32)
              + lax.dot_general(wrl_ref[...], hh, NT_DIMS, preferred_element_type=F32))
    scores = _sigmoid(logits)
    biased = scores + br_ref[...]

    b3 = biased.reshape(N_GROUPS, GROUP_SIZE, tm)
    it3 = lax.broadcasted_iota(I32, (N_GROUPS, GROUP_SIZE, tm), 1).astype(F32)
    m1 = jnp.max(b3, axis=1, keepdims=True)
    i1 = jnp.min(jnp.where(b3 == m1, it3, float(GROUP_SIZE)), axis=1, keepdims=True)
    m2 = jnp.max(jnp.where(it3 == i1, -jnp.inf, b3), axis=1, keepdims=True)
    gs = (m1 + m2).reshape(N_GROUPS, tm)

    gi = lax.broadcasted_iota(I32, (N_GROUPS, tm), 0)
    beat = jnp.zeros((N_GROUPS, tm), F32)
    for g in range(N_GROUPS):
        r = gs[g:g + 1, :]
        beat = beat + jnp.where((r > gs) | ((r == gs) & (g < gi)), 1.0, 0.0)
    gpen = jnp.where(beat < TOPK_GROUPS, 0.0, -jnp.inf)
    masked = (b3 + gpen.reshape(N_GROUPS, 1, tm)).reshape(N_EXPERTS, tm)

    ei = lax.broadcasted_iota(I32, (N_EXPERTS, tm), 0).astype(F32)
    idx_rows, w_rows = [], []
    chosen = jnp.zeros((N_EXPERTS, tm), F32)
    for _ in range(TOP_K):
        m = jnp.max(masked, axis=0, keepdims=True)
        ik = jnp.min(jnp.where(masked == m, ei, float(N_EXPERTS)), axis=0, keepdims=True)
        hit = ei == ik
        w_rows.append(jnp.sum(jnp.where(hit, scores, 0.0), axis=0, keepdims=True))
        idx_rows.append(ik)
        masked = jnp.where(hit, -jnp.inf, masked)
        chosen = jnp.where(hit, 1.0, chosen)
    idx_ref[...] = jnp.concatenate(idx_rows, axis=0).astype(I32)
    wts = jnp.concatenate(w_rows, axis=0)
    wt_ref[...] = wts / jnp.sum(wts, axis=0, keepdims=True) * ROUTED_SCALE

    before = cnt_sc[...] + jnp.dot(chosen.astype(BF16), tri_ref[...], preferred_element_type=F32)
    rank_rows = [jnp.sum(jnp.where(ei == idx_rows[kk], before, 0.0), axis=0, keepdims=True)
                 for kk in range(TOP_K)]
    rank_ref[...] = jnp.concatenate(rank_rows, axis=0).astype(I32)
    cnt_sc[...] = cnt_sc[...] + jnp.sum(chosen, axis=1, keepdims=True)
    cout_ref[...] = cnt_sc[...]


def _post(x2, ya, yb, ga, gb, gate1, shift2, scale2, prm, count_in, nb, tm):
    t, d = x2.shape
    ns = t // nb // tm
    row = lambda w: pl.BlockSpec((tm, w), lambda b, i: (b * ns + i, 0))
    colb = pl.BlockSpec((TOP_K, tm), lambda b, i: (0, b * ns + i))
    r = jnp.arange(tm)
    tri = (r[:, None] < r[None, :]).astype(BF16)
    in_specs = [row(d), row(FOX_WIDTH), row(HGRN_WIDTH), row(d), row(d),
                _mod_spec(gate1, tm, ns), _mod_spec(shift2, tm, ns), _mod_spec(scale2, tm, ns),
                _const_spec((1, d)), _const_spec((FOX_WIDTH, d)), _const_spec((HGRN_WIDTH, d)),
                _const_spec((d, d)), _const_spec((N_EXPERTS, d)), _const_spec((N_EXPERTS, d)),
                _const_spec((N_EXPERTS, 1)), _const_spec((tm, tm)), _const_spec((N_EXPERTS, 1))]
    out_shape = [jax.ShapeDtypeStruct((t, d), F32), jax.ShapeDtypeStruct((t * ROW_CHUNKS, LANES), F32),
                 jax.ShapeDtypeStruct((TOP_K, t), I32), jax.ShapeDtypeStruct((TOP_K, t), F32),
                 jax.ShapeDtypeStruct((TOP_K, t), I32), jax.ShapeDtypeStruct((N_EXPERTS, 1), F32)]
    out_specs = [row(d), pl.BlockSpec((tm * ROW_CHUNKS, LANES), lambda b, i: (b * ns + i, 0)),
                 colb, colb, colb, pl.BlockSpec((N_EXPERTS, 1), lambda b, i: (0, 0))]
    return pl.pallas_call(
        functools.partial(_post_kernel, tm=tm), grid=(nb, ns), in_specs=in_specs,
        out_specs=out_specs, out_shape=out_shape,
        scratch_shapes=[pltpu.VMEM((N_EXPERTS, 1), F32)],
        compiler_params=_cparams(("arbitrary", "arbitrary")), name="post",
    )(x2, ya, yb, ga, gb, gate1, shift2, scale2, prm["g2"], prm["wpa"], prm["wpb"], prm["wo"],
      prm["wrh"], prm["wrl"], prm["br"], tri, count_in)


def _pos_kernel(idx_ref, rank_ref, pst_ref, pos_ref, *, tm):
    ei = lax.broadcasted_iota(I32, (N_EXPERTS, tm), 0)
    pst = pst_ref[...]
    rows = [jnp.sum(jnp.where(ei == idx_ref[kk:kk + 1, :], pst, 0.0), axis=0, keepdims=True)
            for kk in range(TOP_K)]
    pos_ref[...] = jnp.concatenate(rows, axis=0).astype(I32) + rank_ref[...]


def _pos(idx, rank, pstart_col, tm):
    t = idx.shape[1]
    colb = pl.BlockSpec((TOP_K, tm), lambda i: (0, i))
    return pl.pallas_call(
        functools.partial(_pos_kernel, tm=tm), grid=(t // tm,),
        in_specs=[colb, colb, _const_spec((N_EXPERTS, 1))], out_specs=colb,
        out_shape=jax.ShapeDtypeStruct((TOP_K, t), I32),
        compiler_params=_cparams(("parallel",)), name="moe_pos",
    )(idx, rank, pstart_col)


def _padfill_kernel(cnt_ref, pad_ref, pst_ref, xs_ref, zero_sc, sem):
    zero_sc[...] = jnp.zeros(zero_sc.shape, zero_sc.dtype)
    sizes = [1 << b for b in reversed(range(MOE_BLK.bit_length() - 1))]

    def chunks(e, act):
        n = pad_ref[e] - cnt_ref[e]
        row = pst_ref[e] + cnt_ref[e]
        for size in sizes:
            @pl.when((n & size) != 0)
            def _():
                act(pltpu.make_async_copy(zero_sc.at[pl.ds(0, size)], xs_ref.at[pl.ds(row, size)], sem))

            row = row + (n & size)

    def issue(e, c):
        chunks(e, lambda cp: cp.start())
        return c

    def drain(e, c):
        chunks(e, lambda cp: cp.wait())
        return c

    lax.fori_loop(0, N_EXPERTS, issue, 0)
    lax.fori_loop(0, N_EXPERTS, drain, 0)


def _padfill(counts, padded, pstart, n_rows, d):
    row_tile = (d // LANES, LANES)
    return pl.pallas_call(
        _padfill_kernel,
        grid_spec=pltpu.PrefetchScalarGridSpec(
            num_scalar_prefetch=3, grid=(1,), in_specs=[],
            out_specs=pl.BlockSpec(memory_space=pl.ANY),
            scratch_shapes=[pltpu.VMEM((MOE_BLK // 2,) + row_tile, F32), pltpu.SemaphoreType.DMA(())]),
        out_shape=jax.ShapeDtypeStruct((n_rows,) + row_tile, F32),
        compiler_params=pltpu.CompilerParams(dimension_semantics=("arbitrary",), has_side_effects=True),
        name="moe_padfill",
    )(counts, padded, pstart)


def _dispatch_kernel(pos_ref, h2_ref, xs_in_ref, xs_ref, sem, *, tm):
    del xs_in_ref

    for t in range(tm):
        for kk in range(TOP_K):
            pltpu.make_async_copy(h2_ref.at[t], xs_ref.at[pos_ref[kk, t]], sem).start(priority=kk % 2)
    for kk in range(TOP_K):
        pltpu.make_async_copy(h2_ref, xs_ref.at[pl.ds(0, tm)], sem).wait()


def _dispatch(pos, h2_flat, xs, tm):
    h2 = h2_flat.reshape(-1, ROW_CHUNKS, LANES)
    t = h2.shape[0]
    return pl.pallas_call(
        functools.partial(_dispatch_kernel, tm=tm),
        grid=(t // tm,),
        in_specs=[pl.BlockSpec((TOP_K, tm), lambda i: (0, i), memory_space=pltpu.SMEM),
                  pl.BlockSpec((tm,) + h2.shape[1:], lambda i: (i, 0, 0)),
                  pl.BlockSpec(memory_space=pl.ANY)],
        out_specs=pl.BlockSpec(memory_space=pl.ANY),
        out_shape=jax.ShapeDtypeStruct(xs.shape, xs.dtype),
        scratch_shapes=[pltpu.SemaphoreType.DMA(())],
        input_output_aliases={2: 0},
        compiler_params=pltpu.CompilerParams(dimension_semantics=("arbitrary",), has_side_effects=True,
                                             vmem_limit_bytes=VMEM_LIMIT),
        name="moe_dispatch",
    )(pos, h2, xs)


def _moe_kernel(be_ref, nu_ref, first_ref, slot_ref, nxt_ref, x_ref, wg_ref, wu_ref, wd_ref, y_ref,
                wg_buf, wu_buf, wd_buf, wgu_sc, wd_sc, sem):
    i = pl.program_id(0)
    e = be_ref[i]

    def fetch(expert, slot):
        return (pltpu.make_async_copy(wg_ref.at[expert], wg_buf.at[slot], sem.at[slot]),
                pltpu.make_async_copy(wu_ref.at[expert], wu_buf.at[slot], sem.at[slot]),
                pltpu.make_async_copy(wd_ref.at[expert], wd_buf.at[slot], sem.at[slot]))

    @pl.when(first_ref[i] == 1)
    def _switch_expert():
        slot = slot_ref[i]

        @pl.when(i == 0)
        def _():
            for cp in fetch(e, slot):
                cp.start()

        for cp in fetch(e, slot):
            cp.wait()
        nxt = nxt_ref[e]

        @pl.when(nxt >= 0)
        def _():
            for cp in fetch(nxt, 1 - slot):
                cp.start()

        wgu_sc[:, 0:D_EXPERT] = wg_buf[slot].astype(BF16)
        wgu_sc[:, D_EXPERT:2 * D_EXPERT] = wu_buf[slot].astype(BF16)
        wd_sc[...] = wd_buf[slot].astype(BF16)

    @pl.when(i < nu_ref[0])
    def _compute():
        gu = jnp.dot(_load_rows(x_ref).astype(BF16), wgu_sc[...], preferred_element_type=F32)
        gt = gu[:, 0:D_EXPERT]
        h = gt * _sigmoid(gt) * gu[:, D_EXPERT:2 * D_EXPERT]
        _store_rows(y_ref, jnp.dot(h.astype(BF16), wd_sc[...], preferred_element_type=F32))


def _moe(blk_expert, n_used, first, slot, nxt, xs_pool, w_gate, w_up, w_down):
    n_rows = xs_pool.shape[0]
    xs = xs_pool.reshape(n_rows * ROW_CHUNKS, LANES)
    d = w_gate.shape[1]
    nblk = n_rows // MOE_BLK
    xmap = lambda i, be, nu, fi, sl, nx: (jnp.minimum(i, nu[0] - 1), 0)
    hbm = pl.BlockSpec(memory_space=pl.ANY)
    return pl.pallas_call(
        _moe_kernel,
        grid_spec=pltpu.PrefetchScalarGridSpec(
            num_scalar_prefetch=5, grid=(nblk,),
            in_specs=[pl.BlockSpec((MOE_BLK * ROW_CHUNKS, LANES), xmap), hbm, hbm, hbm],
            out_specs=pl.BlockSpec((MOE_BLK * ROW_CHUNKS, LANES), xmap),
            scratch_shapes=[pltpu.VMEM((2, d, D_EXPERT), F32), pltpu.VMEM((2, d, D_EXPERT), F32),
                            pltpu.VMEM((2, D_EXPERT, d), F32),
                            pltpu.VMEM((d, 2 * D_EXPERT), BF16), pltpu.VMEM((D_EXPERT, d), BF16),
                            pltpu.SemaphoreType.DMA((2,))]),
        out_shape=jax.ShapeDtypeStruct(xs.shape, F32),
        compiler_params=_cparams(("arbitrary",)), name="moe_experts",
    )(blk_expert, n_used, first, slot, nxt, xs, w_gate, w_up, w_down)


def _combine_kernel(pos_ref, posn_ref, x1_ref, h2_ref, g2_ref, wt_ref, wsgu_ref, wsd_ref, ys_ref, ysflat_ref,
                    o_ref, buf, sem, *, tm):
    lin = pl.program_id(0) * pl.num_programs(1) + pl.program_id(1)
    total = pl.num_programs(0) * pl.num_programs(1)
    slot = lin % 2

    def issue(p_ref, s):
        for t in range(tm):
            for kk in range(TOP_K):
                pltpu.make_async_copy(ys_ref.at[p_ref[kk, t]], buf.at[s, kk, pl.ds(t * ROW_CHUNKS, ROW_CHUNKS)],
                                      sem.at[s]).start(priority=kk % 2)

    @pl.when(lin == 0)
    def _():
        issue(pos_ref, 0)

    issue(posn_ref, 1 - slot)

    gu = jnp.dot(_load_rows(h2_ref).astype(BF16), wsgu_ref[...], preferred_element_type=F32)
    gt = gu[:, 0:D_SHARED]
    hs = gt * _sigmoid(gt) * gu[:, D_SHARED:2 * D_SHARED]
    shared = jnp.dot(hs.astype(BF16), wsd_ref[...], preferred_element_type=F32)

    for kk in range(TOP_K):
        pltpu.make_async_copy(ysflat_ref.at[pl.ds(0, tm * ROW_CHUNKS)], buf.at[slot, kk], sem.at[slot]).wait()

    nchunk = shared.shape[-1] // LANES
    acc = [shared[:, c * LANES:(c + 1) * LANES] for c in range(nchunk)]
    for kk in range(TOP_K):
        wrow = jnp.broadcast_to(wt_ref[kk:kk + 1, :], (LANES, tm))
        wcol = wrow.T
        for c in range(nchunk):
            acc[c] = acc[c] + _row_chunk(buf.at[slot, kk], c)[...] * wcol
    o_ref[...] = x1_ref[...] + g2_ref[0] * jnp.concatenate(acc, axis=1)

    @pl.when(lin + 1 == total)
    def _():
        for kk in range(TOP_K):
            pltpu.make_async_copy(ysflat_ref.at[pl.ds(0, tm * ROW_CHUNKS)], buf.at[1 - slot, kk],
                                  sem.at[1 - slot]).wait()


def _combine(pos, x1, h2, gate2, wts, wsgu, wsd, ys_flat, nb, tm):
    t, d = x1.shape
    ns = t // nb // tm
    row = pl.BlockSpec((tm, d), lambda b, i: (b * ns + i, 0))
    last = nb * ns - 1
    ys = ys_flat.reshape(-1, ROW_CHUNKS, LANES)
    return pl.pallas_call(
        functools.partial(_combine_kernel, tm=tm),
        grid=(nb, ns),
        in_specs=[pl.BlockSpec((TOP_K, tm), lambda b, i: (0, b * ns + i), memory_space=pltpu.SMEM),
                  pl.BlockSpec((TOP_K, tm), lambda b, i: (0, jnp.minimum(b * ns + i + 1, last)),
                               memory_space=pltpu.SMEM),
                  row, pl.BlockSpec((tm * ROW_CHUNKS, LANES), lambda b, i: (b * ns + i, 0)),
                  _mod_spec(gate2, tm, ns),
                  pl.BlockSpec((TOP_K, tm), lambda b, i: (0, b * ns + i)),
                  _const_spec(wsgu.shape), _const_spec(wsd.shape),
                  pl.BlockSpec(memory_space=pl.ANY), pl.BlockSpec(memory_space=pl.ANY)],
        out_specs=row,
        out_shape=jax.ShapeDtypeStruct((t, d), F32),
        scratch_shapes=[pltpu.VMEM((2, TOP_K, tm * ROW_CHUNKS, LANES), F32), pltpu.SemaphoreType.DMA((2,))],
        compiler_params=_cparams(("arbitrary", "arbitrary")), name="moe_combine",
    )(pos, pos, x1, h2, gate2, wts, wsgu, wsd, ys, ys_flat)


def _prepare_params(g_norm1, w_in, b_fox_f, g_q, g_k, hgrn_lb, g_hgrn_o, w_proj_a, w_proj_b, w_out,
                    g_norm2, w_router, b_router, w_sh_gate, w_sh_up, w_sh_down):
    d = D_MODEL
    fw, hw = FOX_WIDTH, HGRN_WIDTH
    c0 = 3 * fw
    c1 = c0 + FOX_HEADS
    c2 = c1 + 4 * hw
    wff = w_in[:, c0:c1]
    head = jnp.arange(fw) // FOX_HEAD_DIM
    wr_t = w_router.T
    wrh = wr_t.astype(BF16)
    return dict(
        g1=g_norm1.reshape(1, d),
        wqkv=w_in[:, :c0].astype(BF16),
        wff=jnp.pad(wff, ((0, 0), (0, LANES - FOX_HEADS))).astype(BF16),
        wfft=jnp.pad(wff.T, ((0, 16 - FOX_HEADS), (0, 0))).astype(BF16),
        bf=b_fox_f.reshape(1, FOX_HEADS), bft=b_fox_f.reshape(FOX_HEADS, 1),
        gq=jnp.tile(g_q, FOX_HEADS).reshape(1, fw), gk=jnp.tile(g_k, FOX_HEADS).reshape(1, fw),
        bd=(head[:, None] == head[None, :]).astype(BF16) * (1.0 / FOX_HEAD_DIM),
        lbp=hgrn_lb,
        wh=w_in[:, c1:c2].astype(BF16), wg=w_in[:, c2:].astype(BF16),
        gn=g_hgrn_o.reshape(1, HGRN_DK),
        g2=g_norm2.reshape(1, d),
        wpa=w_proj_a.astype(BF16), wpb=w_proj_b.astype(BF16), wo=w_out.astype(BF16),
        wrh=wrh, wrl=(wr_t - wrh.astype(F32)).astype(BF16), br=b_router.reshape(N_EXPERTS, 1),
        wsgu=jnp.concatenate([w_sh_gate, w_sh_up], axis=1).astype(BF16), wsd=w_sh_down.astype(BF16),
    )


def _mixers(x2, nb, seq, tm, shift1, scale1, prm, past, tile, tb):
    (q, k, kb, v, vb, lf, lft, hq, hl, hi, og, ga, gb) = _inproj(x2, nb if shift1.shape[1] == 1 else 1, tm,
                                                                 shift1, scale1, prm)
    t = nb * seq
    hp = FOX_HEADS // 2
    lft = lft.transpose(1, 0, 2).reshape(FOX_HEADS, nb, seq).transpose(1, 0, 2)
    if past is None:
        fcum = _cumsum_lanes(lft.reshape(nb * FOX_HEADS, seq)).reshape(nb, hp, 2, seq)
        ya = _fox_prompt(q, kb, vb, fcum, nb, seq, tile, min(32, tile))
        s0 = None
    else:
        cache_k, cache_v, cache_lf, s0 = past
        plen = cache_k.shape[1]
        ltot = plen + seq
        lpad = -(-ltot // LANES) * LANES
        lf_all = jnp.concatenate([cache_lf.transpose(0, 2, 1), lft,
                                  jnp.zeros((nb, FOX_HEADS, lpad - ltot), F32)], axis=-1)
        fcum = _cumsum_lanes(lf_all.reshape(nb * FOX_HEADS, lpad)).reshape(nb, hp, 2, lpad)
        ya = _fox_sample(q, kb, vb, cache_k.reshape(nb, plen, FOX_WIDTH), cache_v.reshape(nb, plen, FOX_WIDTH),
                         fcum, nb, seq, plen)
    yb, sfin = _hgrn(hq, hl, hi, og, prm["gn"], s0, nb, seq, tb)
    return k, v, lf, sfin, ya, yb, ga, gb


def kernel(x_prompt, x_sample, cache_fox_k, cache_fox_v, cache_fox_logf, state_hgrn, c_prompt, c_sample,
           w_ada, b_ada, g_norm1, w_in, b_fox_f, g_q, g_k, hgrn_lb, g_hgrn_o, w_proj_a, w_proj_b, w_out,
           g_norm2, w_router, b_router, w_exp_gate, w_exp_up, w_exp_down, w_sh_gate, w_sh_up, w_sh_down):
    assert w_ada.shape[0] == 1 and hgrn_lb.shape[0] == 2, "single-layer trunk"
    d = D_MODEL
    bp, sp, _ = x_prompt.shape
    bs, ss, _ = x_sample.shape
    tp, ts = bp * sp, bs * ss
    prm = _prepare_params(g_norm1[0], w_in[0], b_fox_f[0], g_q[0], g_k[0], hgrn_lb, g_hgrn_o[0],
                          w_proj_a[0], w_proj_b[0], w_out[0], g_norm2[0], w_router[0], b_router[0],
                          w_sh_gate[0], w_sh_up[0], w_sh_down[0])

    bc = bp + bs
    bc_pad = -(-bc // 8) * 8
    c_all = jnp.concatenate([c_prompt, c_sample, jnp.zeros((bc_pad - bc, d), F32)], axis=0)
    mod = _ada(c_all, w_ada[0], b_ada[0])
    mod_p = [mod[:bp, j * d:(j + 1) * d].reshape(bp, 1, d) for j in range(6)]
    mod_s = [jnp.repeat(mod[bp:bc, j * d:(j + 1) * d], ss, axis=0).reshape(1, ts, d) for j in range(6)]

    tm_p = min(512, sp)
    tm_s = min(256, ts)
    xp2 = x_prompt.reshape(tp, d)
    xs2 = x_sample.reshape(ts, d)
    tile = min(1024, sp)
    tb = min(256, sp)

    kp, vp, lfp, sfin_p, ya_p, yb_p, ga_p, gb_p = _mixers(
        xp2, bp, sp, tm_p, mod_p[0], mod_p[1], prm, None, tile, tb)
    past = (cache_fox_k[0], cache_fox_v[0], cache_fox_logf[0], state_hgrn[0])
    ks, vs, lfs, sfin_s, ya_s, yb_s, ga_s, gb_s = _mixers(
        xs2, bs, ss, tm_s, mod_s[0], mod_s[1], prm, past, None, min(256, ss))

    zero_cnt = jnp.zeros((N_EXPERTS, 1), F32)
    x1_p, h2_p, idx_p, wt_p, rank_p, cnt_p = _post(xp2, ya_p, yb_p, ga_p, gb_p, mod_p[2], mod_p[3], mod_p[4],
                                                    prm, zero_cnt, bp, tm_p)
    x1_s, h2_s, idx_s, wt_s, rank_s, cnt_all = _post(xs2, ya_s, yb_s, ga_s, gb_s, mod_s[2], mod_s[3], mod_s[4],
                                                     prm, cnt_p, 1, tm_s)

    counts = cnt_all.reshape(N_EXPERTS).astype(I32)
    padded = (counts + MOE_BLK - 1) // MOE_BLK * MOE_BLK
    pend = jnp.cumsum(padded)
    pstart = pend - padded
    nblk = -(-((tp + ts) * TOP_K + N_EXPERTS * (MOE_BLK - 1)) // MOE_BLK)
    n_used = (pend[-1] // MOE_BLK).reshape(1)
    blk_row0 = jnp.arange(nblk, dtype=I32) * MOE_BLK
    blk_expert = jnp.minimum(jnp.sum((pend[None, :] <= blk_row0[:, None]).astype(I32), axis=1), N_EXPERTS - 1)
    blk_used = jnp.arange(nblk) < n_used[0]
    blk_expert = jnp.where(blk_used, blk_expert, blk_expert[jnp.maximum(n_used[0] - 1, 0)])
    blk_first = blk_used & jnp.concatenate([jnp.ones((1,), bool), blk_expert[1:] != blk_expert[:-1]])
    blk_slot = ((jnp.cumsum(blk_first.astype(I32)) - 1) & 1).astype(I32)
    eids = jnp.arange(N_EXPERTS, dtype=I32)
    later = (eids[None, :] > eids[:, None]) & (padded[None, :] > 0)
    nxt_expert = jnp.min(jnp.where(later, eids[None, :], N_EXPERTS), axis=1)
    nxt_expert = jnp.where(nxt_expert < N_EXPERTS, nxt_expert, -1).astype(I32)
    pstart_col = pstart.astype(F32).reshape(N_EXPERTS, 1)
    pos_p = _pos(idx_p, rank_p, pstart_col, tm_p)
    pos_s = _pos(idx_s, rank_s, pstart_col, tm_s)

    xs_pool = _padfill(counts, padded, pstart, nblk * MOE_BLK, d)
    xs_pool = _dispatch(pos_p, h2_p, xs_pool, min(128, tm_p))
    xs_pool = _dispatch(pos_s, h2_s, xs_pool, min(128, tm_s))
    ys_pool = _moe(blk_expert, n_used, blk_first.astype(I32), blk_slot, nxt_expert, xs_pool,
                   w_exp_gate[0], w_exp_up[0], w_exp_down[0])

    tm_c = min(128, sp)
    y_p = _combine(pos_p, x1_p, h2_p, mod_p[5], wt_p, prm["wsgu"], prm["wsd"], ys_pool, bp, tm_c)
    y_s = _combine(pos_s, x1_s, h2_s, mod_s[5], wt_s, prm["wsgu"], prm["wsd"], ys_pool, 1, min(128, ts))

    return (y_p.reshape(bp, sp, d), y_s.reshape(bs, ss, d),
            kp.reshape(1, bp, sp, FOX_HEADS, FOX_HEAD_DIM), vp.reshape(1, bp, sp, FOX_HEADS, FOX_HEAD_DIM),
            lfp.reshape(1, bp, sp, FOX_HEADS), sfin_p[None],
            ks.reshape(1, bs, ss, FOX_HEADS, FOX_HEAD_DIM), vs.reshape(1, bs, ss, FOX_HEADS, FOX_HEAD_DIM),
            lfs.reshape(1, bs, ss, FOX_HEADS), sfin_s[None])
```

```python
import functools

import jax
import jax.numpy as jnp
from jax import lax
from jax.experimental import pallas as pl
from jax.experimental.pallas import tpu as pltpu

F32 = jnp.float32
BF16 = jnp.bfloat16
I32 = jnp.int32

D_MODEL = 1024
FOX_HEADS = 8
FOX_HEAD_DIM = 64
FOX_WIDTH = FOX_HEADS * FOX_HEAD_DIM
HGRN_HEADS = 4
HGRN_DK = 128
HGRN_WIDTH = HGRN_HEADS * HGRN_DK
N_EXPERTS = 256
TOP_K = 8
N_GROUPS = 8
TOPK_GROUPS = 4
GROUP_SIZE = N_EXPERTS // N_GROUPS
D_EXPERT = 256
D_SHARED = 256
ROUTED_SCALE = 2.5
NORM_EPS = 1e-6
NEG_BIG = -1e30
LOG2E = 1.4426950408889634
QK_SCALE = FOX_HEAD_DIM ** -0.5 * LOG2E

LANES = 128
SUB = 16
MOE_BLK = 512
VMEM_LIMIT = 56 * 1024 * 1024

NT_DIMS = (((1,), (1,)), ((), ()))
TN_DIMS = (((0,), (0,)), ((), ()))


def _cparams(sem):
    return pltpu.CompilerParams(dimension_semantics=sem, vmem_limit_bytes=VMEM_LIMIT)


def _const_spec(shape):
    nd = len(shape)
    return pl.BlockSpec(shape, lambda *_: (0,) * nd, pipeline_mode=pl.Buffered(1))


def _sigmoid(z):
    return 1.0 / (1.0 + jnp.exp(-z))


def _log_sigmoid(z):
    return jnp.minimum(z, 0.0) - jnp.log(1.0 + jnp.exp(-jnp.abs(z)))


ROW_CHUNKS = D_MODEL // LANES


def _row_chunk(ref, c):
    return ref.at[pl.ds(c, ref.shape[0] // ROW_CHUNKS, stride=ROW_CHUNKS), :]


def _store_rows(ref, val):
    for c in range(ROW_CHUNKS):
        _row_chunk(ref, c)[...] = val[:, c * LANES:(c + 1) * LANES]


def _load_rows(ref):
    return jnp.concatenate([_row_chunk(ref, c)[...] for c in range(ROW_CHUNKS)], axis=1)


def _split3(a):
    hi = a.astype(BF16)
    r1 = a - hi.astype(F32)
    mid = r1.astype(BF16)
    lo = (r1 - mid.astype(F32)).astype(BF16)
    return hi, mid, lo


def _ada_kernel(c_ref, w_ref, b_ref, o_ref):
    c = c_ref[...]
    s = c * _sigmoid(c)
    o_ref[...] = jnp.dot(s.astype(BF16), w_ref[...].astype(BF16), preferred_element_type=F32) + b_ref[...]


def _ada(c_all, w_ada, b_ada):
    bc, d = c_all.shape
    n = w_ada.shape[1]
    tn = 1024
    return pl.pallas_call(
        _ada_kernel,
        grid=(n // tn,),
        in_specs=[pl.BlockSpec((bc, d), lambda j: (0, 0)),
                  pl.BlockSpec((d, tn), lambda j: (0, j)),
                  pl.BlockSpec((1, tn), lambda j: (0, j))],
        out_specs=pl.BlockSpec((bc, tn), lambda j: (0, j)),
        out_shape=jax.ShapeDtypeStruct((bc, n), F32),
        compiler_params=_cparams(("parallel",)),
        name="ada",
    )(c_all, w_ada, b_ada.reshape(1, n))


def _inproj_kernel(x_ref, sh_ref, sc_ref, g1_ref, wqkv_ref, wff_ref, wfft_ref, bf_ref, bft_ref,
                   gq_ref, gk_ref, bd_ref, lbp_ref, wh_ref, wg_ref,
                   q_ref, k_ref, kb_ref, v_ref, vb_ref, lf_ref, lft_ref,
                   hq_ref, hl_ref, hi_ref, og_ref, ga_ref, gb_ref):
    x = x_ref[...]
    ms = jnp.mean(x * x, axis=-1, keepdims=True)
    h = x * lax.rsqrt(ms + NORM_EPS) * g1_ref[...]
    h = h * (1.0 + sc_ref[0]) + sh_ref[0]
    hb = h.astype(BF16)

    def headnorm(a, g):
        ss = jnp.dot((a * a).astype(BF16), bd_ref[...], preferred_element_type=F32)
        return a * lax.rsqrt(ss + NORM_EPS) * g

    fq = jnp.dot(hb, wqkv_ref[:, 0:FOX_WIDTH], preferred_element_type=F32)
    q_ref[...] = (headnorm(fq, gq_ref[...]) * QK_SCALE).astype(BF16)
    fk = jnp.dot(hb, wqkv_ref[:, FOX_WIDTH:2 * FOX_WIDTH], preferred_element_type=F32)
    k = headnorm(fk, gk_ref[...])
    kb_ref[...] = k.astype(BF16)
    fv = jnp.dot(hb, wqkv_ref[:, 2 * FOX_WIDTH:3 * FOX_WIDTH], preferred_element_type=F32)
    vb_ref[...] = fv.astype(BF16)
    for hd in range(FOX_HEADS):
        cols = slice(hd * FOX_HEAD_DIM, (hd + 1) * FOX_HEAD_DIM)
        k_ref[:, hd, :] = k[:, cols]
        v_ref[:, hd, :] = fv[:, cols]

    ff = jnp.dot(hb, wff_ref[...], preferred_element_type=F32)
    lf_ref[...] = _log_sigmoid(ff[:, 0:FOX_HEADS] + bf_ref[...])
    fft = lax.dot_general(wfft_ref[...], hb, NT_DIMS, preferred_element_type=F32)
    lft_ref[0] = _log_sigmoid(fft[0:FOX_HEADS, :] + bft_ref[...])

    lbp = lbp_ref[...]
    e = jnp.exp(lbp - jnp.max(lbp, axis=0, keepdims=True))
    lb = e[0:1, :] / jnp.sum(e, axis=0, keepdims=True)

    w = HGRN_WIDTH
    hq = jnp.dot(hb, wh_ref[:, 0:w], preferred_element_type=F32)
    hq_ref[...] = hq * _sigmoid(hq)
    hf = jnp.dot(hb, wh_ref[:, w:2 * w], preferred_element_type=F32)
    hl_ref[...] = jnp.log(lb + (1.0 - lb) * _sigmoid(hf))
    hi_ref[...] = jnp.dot(hb, wh_ref[:, 2 * w:3 * w], preferred_element_type=F32)
    hg = jnp.dot(hb, wh_ref[:, 3 * w:4 * w], preferred_element_type=F32)
    og_ref[...] = _sigmoid(hg)
    ga = jnp.dot(hb, wg_ref[:, 0:D_MODEL], preferred_element_type=F32)
    ga_ref[...] = _sigmoid(ga).astype(BF16)
    gb = jnp.dot(hb, wg_ref[:, D_MODEL:2 * D_MODEL], preferred_element_type=F32)
    gb_ref[...] = _sigmoid(gb).astype(BF16)


def _mod_spec(arr, tm, ns):
    d = arr.shape[-1]
    if arr.shape[1] == 1:
        return pl.BlockSpec((1, 1, d), lambda b, i: (b, 0, 0))
    return pl.BlockSpec((1, tm, d), lambda b, i: (0, b * ns + i, 0))


def _inproj(x2, nb, tm, shift1, scale1, prm):
    t, d = x2.shape
    ns = t // nb // tm
    row = lambda w: pl.BlockSpec((tm, w), lambda b, i: (b * ns + i, 0))
    in_specs = [row(d), _mod_spec(shift1, tm, ns), _mod_spec(scale1, tm, ns),
                _const_spec((1, d)), _const_spec(prm["wqkv"].shape), _const_spec(prm["wff"].shape),
                _const_spec(prm["wfft"].shape), _const_spec((1, FOX_HEADS)), _const_spec((FOX_HEADS, 1)),
                _const_spec((1, FOX_WIDTH)), _const_spec((1, FOX_WIDTH)), _const_spec((FOX_WIDTH, FOX_WIDTH)),
                _const_spec(prm["lbp"].shape), _const_spec(prm["wh"].shape), _const_spec(prm["wg"].shape)]
    fw, hw = FOX_WIDTH, HGRN_WIDTH
    heads = pl.BlockSpec((tm, FOX_HEADS, FOX_HEAD_DIM), lambda b, i: (b * ns + i, 0, 0))
    out_shape = [jax.ShapeDtypeStruct((t, fw), BF16),
                 jax.ShapeDtypeStruct((t, FOX_HEADS, FOX_HEAD_DIM), F32),
                 jax.ShapeDtypeStruct((t, fw), BF16),
                 jax.ShapeDtypeStruct((t, FOX_HEADS, FOX_HEAD_DIM), F32),
                 jax.ShapeDtypeStruct((t, fw), BF16),
                 jax.ShapeDtypeStruct((t, FOX_HEADS), F32),
                 jax.ShapeDtypeStruct((nb * ns, FOX_HEADS, tm), F32),
                 jax.ShapeDtypeStruct((t, hw), F32),
                 jax.ShapeDtypeStruct((t, hw), F32),
                 jax.ShapeDtypeStruct((t, hw), F32),
                 jax.ShapeDtypeStruct((t, hw), F32),
                 jax.ShapeDtypeStruct((t, d), BF16),
                 jax.ShapeDtypeStruct((t, d), BF16)]
    out_specs = [row(fw), heads, row(fw), heads, row(fw), row(FOX_HEADS),
                 pl.BlockSpec((1, FOX_HEADS, tm), lambda b, i: (b * ns + i, 0, 0)),
                 row(hw), row(hw), row(hw), row(hw), row(d), row(d)]
    return pl.pallas_call(
        _inproj_kernel, grid=(nb, ns), in_specs=in_specs, out_specs=out_specs, out_shape=out_shape,
        compiler_params=_cparams(("parallel", "parallel")), name="inproj",
    )(x2, shift1, scale1, prm["g1"], prm["wqkv"], prm["wff"], prm["wfft"], prm["bf"], prm["bft"],
      prm["gq"], prm["gk"], prm["bd"], prm["lbp"], prm["wh"], prm["wg"])


def _cumsum_kernel(x_ref, o_ref):
    x = x_ref[...]
    n = x.shape[-1]
    lane = lax.broadcasted_iota(I32, x.shape, 1)
    s = 1
    while s < n:
        x = x + jnp.where(lane >= s, pltpu.roll(x, s, axis=1), 0.0)
        s *= 2
    o_ref[...] = x * LOG2E


def _cumsum_lanes(x):
    return pl.pallas_call(
        _cumsum_kernel, out_shape=jax.ShapeDtypeStruct(x.shape, F32),
        compiler_params=pltpu.CompilerParams(vmem_limit_bytes=VMEM_LIMIT), name="cumsum",
    )(x)


def _fox_prompt_kernel(qt_ref, kt_ref, q_ref, k_ref, v_ref, fk_ref, o_ref,
                       m0, m1, a0, a1, s0, s1, p0, p1, *, tile, rsub):
    step = pl.program_id(2)
    qi = qt_ref[step]
    ki = kt_ref[step]
    heads = ((m0, a0, s0, p0), (m1, a1, s1, p1))
    lane = lax.broadcasted_iota(I32, (1, LANES), 1)
    lo_half = lane < FOX_HEAD_DIM

    @pl.when(ki == 0)
    def _init():
        for m_sc, a_sc, _, _ in heads:
            m_sc[...] = jnp.full(m_sc.shape, NEG_BIG, F32)
            a_sc[...] = jnp.zeros(a_sc.shape, F32)

    def tile_update(diagonal):
        q = q_ref[...]
        k = k_ref[...]
        v = v_ref[...]
        fk = fk_ref[0, 0]
        nsub = tile // rsub
        if diagonal:
            ahead = (lax.broadcasted_iota(I32, (rsub, tile), 1)
                     - lax.broadcasted_iota(I32, (rsub, tile), 0))
        for hs, (_, _, s_sc, _) in enumerate(heads):
            sel = lo_half if hs == 0 else jnp.logical_not(lo_half)
            qh = jnp.where(sel, q, jnp.zeros_like(q))
            s_sc[...] = lax.dot_general(qh, k, NT_DIMS, preferred_element_type=F32)
        for hs, (m_sc, a_sc, s_sc, p_sc) in enumerate(heads):
            sel = lo_half if hs == 0 else jnp.logical_not(lo_half)
            v_ones = jnp.where(sel, v, jnp.ones_like(v))

            def biased(r):
                rows = slice(r * rsub, (r + 1) * rsub)
                cw = min(tile, -(-((r + 1) * rsub) // LANES) * LANES) if diagonal else tile
                s = s_sc[rows, 0:cw] - fk[hs:hs + 1, 0:cw]
                if diagonal:
                    s = jnp.where(ahead[:, 0:cw] <= r * rsub, s, NEG_BIG)
                return s, rows, cw

            m_old = m_sc[...]
            m_parts = []
            for r in range(nsub):
                s, rows, _ = biased(r)
                m_parts.append(jnp.maximum(m_old[rows], jnp.max(s, axis=1, keepdims=True)))
            for r in range(nsub):
                s, rows, cw = biased(r)
                nc = cw // LANES
                p = jnp.exp2(s - jnp.concatenate([m_parts[r]] * nc, axis=1))
                p_sc[rows, 0:cw] = p.astype(BF16)
                if cw < tile:
                    p_sc[rows, cw:tile] = jnp.zeros((rsub, tile - cw), BF16)
            m_new = jnp.concatenate(m_parts, axis=0)
            alpha = jnp.exp2(m_old - m_new)
            m_sc[...] = m_new
            a_sc[...] = alpha * a_sc[...] + jnp.dot(p_sc[...], v_ones, preferred_element_type=F32)

    @pl.when(ki < qi)
    def _full():
        tile_update(False)

    @pl.when(ki == qi)
    def _diag():
        tile_update(True)
        acc0 = a0[...]
        acc1 = a1[...]
        o0 = acc0 / pltpu.roll(acc0, FOX_HEAD_DIM, axis=1)
        o1 = acc1 / pltpu.roll(acc1, FOX_HEAD_DIM, axis=1)
        o_ref[...] = jnp.where(lo_half, o0, o1).astype(o_ref.dtype)


def _fox_prompt(q, kb, vb, fcum, nb, seq, tile, rsub):
    nt = seq // tile
    hp = FOX_HEADS // 2
    qt = jnp.asarray([qi for qi in range(nt) for _ in range(qi + 1)], I32)
    kt = jnp.asarray([ki for qi in range(nt) for ki in range(qi + 1)], I32)
    qmap = lambda b, h, s, qt, kt: (b * nt + qt[s], h)
    kmap = lambda b, h, s, qt, kt: (b * nt + kt[s], h)
    fmap = lambda b, h, s, qt, kt: (b, h, 0, kt[s])
    stat = pltpu.VMEM((tile, LANES), F32)
    return pl.pallas_call(
        functools.partial(_fox_prompt_kernel, tile=tile, rsub=rsub),
        grid_spec=pltpu.PrefetchScalarGridSpec(
            num_scalar_prefetch=2, grid=(nb, hp, nt * (nt + 1) // 2),
            in_specs=[pl.BlockSpec((tile, LANES), qmap), pl.BlockSpec((tile, LANES), kmap),
                      pl.BlockSpec((tile, LANES), kmap), pl.BlockSpec((1, 1, 2, tile), fmap)],
            out_specs=pl.BlockSpec((tile, LANES), qmap),
            scratch_shapes=[stat, stat, stat, stat,
                            pltpu.VMEM((tile, tile), F32), pltpu.VMEM((tile, tile), F32),
                            pltpu.VMEM((tile, tile), BF16), pltpu.VMEM((tile, tile), BF16)]),
        out_shape=jax.ShapeDtypeStruct((nb * seq, FOX_WIDTH), BF16),
        compiler_params=_cparams(("parallel", "parallel", "arbitrary")),
        name="fox_prompt",
    )(qt, kt, q, kb, vb, fcum)


def _fox_sample_kernel(q_ref, kn_ref, vn_ref, kp_ref, vp_ref, fk_ref, o_ref, *, past, seq):
    lane = lax.broadcasted_iota(I32, (1, LANES), 1)
    lo_half = lane < FOX_HEAD_DIM
    q = q_ref[...]
    kn = kn_ref[...]
    vn = vn_ref[...]
    kp = kp_ref[0].astype(BF16)
    vp = vp_ref[0].astype(BF16)
    fk = fk_ref[0, 0]
    row = lax.broadcasted_iota(I32, (seq, seq), 0)
    col = lax.broadcasted_iota(I32, (seq, seq), 1)
    causal = col <= row
    outs = []
    for hs in range(2):
        sel = lo_half if hs == 0 else jnp.logical_not(lo_half)
        qh = jnp.where(sel, q, jnp.zeros_like(q))
        sp = lax.dot_general(qh, kp, NT_DIMS, preferred_element_type=F32) - fk[hs:hs + 1, 0:past]
        sn = lax.dot_general(qh, kn, NT_DIMS, preferred_element_type=F32) - fk[hs:hs + 1, past:past + seq]
        sn = jnp.where(causal, sn, NEG_BIG)
        m = jnp.maximum(jnp.max(sp, axis=1, keepdims=True), jnp.max(sn, axis=1, keepdims=True))
        pp = jnp.exp2(sp - m)
        pn = jnp.exp2(sn - m)
        l = jnp.sum(pp, axis=1, keepdims=True) + jnp.sum(pn, axis=1, keepdims=True)
        o = (jnp.dot(pp.astype(BF16), vp, preferred_element_type=F32)
             + jnp.dot(pn.astype(BF16), vn, preferred_element_type=F32))
        outs.append(o / l)
    o_ref[...] = jnp.where(lo_half, outs[0], outs[1]).astype(o_ref.dtype)


def _fox_sample(q, kb, vb, cache_k, cache_v, fcum, nb, seq, past):
    hp = FOX_HEADS // 2
    lpad = fcum.shape[-1]
    rmap = lambda b, h: (b, h)
    cmap = lambda b, h: (b, 0, h)
    return pl.pallas_call(
        functools.partial(_fox_sample_kernel, past=past, seq=seq),
        grid=(nb, hp),
        in_specs=[pl.BlockSpec((seq, LANES), rmap), pl.BlockSpec((seq, LANES), rmap),
                  pl.BlockSpec((seq, LANES), rmap),
                  pl.BlockSpec((1, past, LANES), cmap), pl.BlockSpec((1, past, LANES), cmap),
                  pl.BlockSpec((1, 1, 2, lpad), lambda b, h: (b, h, 0, 0))],
        out_specs=pl.BlockSpec((seq, LANES), rmap),
        out_shape=jax.ShapeDtypeStruct((nb * seq, FOX_WIDTH), BF16),
        compiler_params=_cparams(("parallel", "parallel")),
        name="fox_sample",
    )(q, kb, vb, cache_k, cache_v, fcum)


def _hgrn_kernel(*refs, tb, has_state):
    if has_state:
        q_ref, g_ref, i_ref, og_ref, gn_ref, tri_ref, s0_ref, y_ref, sfin_ref, *st_sc, b_sc, o_sc = refs
    else:
        q_ref, g_ref, i_ref, og_ref, gn_ref, tri_ref, y_ref, sfin_ref, *st_sc, b_sc, o_sc = refs
        s0_ref = None
    step = pl.program_id(1)
    nstep = pl.num_programs(1)

    @pl.when(step == 0)
    def _init():
        for h in range(HGRN_HEADS):
            if has_state:
                st_sc[h][...] = s0_ref[0, h].T
            else:
                st_sc[h][...] = jnp.zeros((HGRN_DK, HGRN_DK), F32)

    g = g_ref[...]
    tri = tri_ref[...]
    b = None
    for part in _split3(g):
        pb = jnp.dot(tri, part, preferred_element_type=F32)
        b = pb if b is None else b + pb
    b_sc[...] = b

    trow = lax.broadcasted_iota(I32, (SUB, 1), 0)

    def sub_chunk(c, carry):
        r0 = pl.multiple_of(c * SUB, SUB)
        for h in range(HGRN_HEADS):
            cs = slice(h * HGRN_DK, (h + 1) * HGRN_DK)
            q = q_ref[pl.ds(r0, SUB), cs]
            gg = g_ref[pl.ds(r0, SUB), cs]
            iv = i_ref[pl.ds(r0, SUB), cs]
            bb = b_sc[pl.ds(r0, SUB), cs]
            kk = 1.0 - jnp.exp(gg)
            st = st_sc[h][...]
            o = lax.dot_general((q * jnp.exp(bb)).astype(BF16), st.astype(BF16), NT_DIMS,
                                preferred_element_type=F32)
            for s in range(SUB):
                lo = 0 if s < 8 else 8
                ks = kk[s:s + 1, :]
                bs = bb[s:s + 1, :]
                ivs = iv[s:s + 1, :]
                e = jnp.exp(jnp.where(trow[lo:] >= s, bb[lo:] - bs, NEG_BIG))
                a = jnp.sum(q[lo:] * ks * e, axis=-1, keepdims=True)
                upd = a * ivs
                if lo:
                    upd = jnp.concatenate([jnp.zeros((8, HGRN_DK), F32), upd], axis=0)
                o = o + upd
            o_sc[pl.ds(r0, SUB), cs] = o
            bl = bb[SUB - 1:SUB, :]
            kd = kk * jnp.exp(bl - bb)
            u = lax.dot_general(iv.astype(BF16), kd.astype(BF16), TN_DIMS, preferred_element_type=F32)
            st_sc[h][...] = st * jnp.exp(bl) + u
        return carry

    lax.fori_loop(0, tb // SUB, sub_chunk, 0, unroll=min(4, tb // SUB))

    for h in range(HGRN_HEADS):
        cs = slice(h * HGRN_DK, (h + 1) * HGRN_DK)
        o = o_sc[:, cs]
        ms = jnp.mean(o * o, axis=-1, keepdims=True)
        y_ref[:, cs] = (o * lax.rsqrt(ms + NORM_EPS) * gn_ref[...] * og_ref[:, cs]).astype(y_ref.dtype)

    @pl.when(step == nstep - 1)
    def _fin():
        for h in range(HGRN_HEADS):
            sfin_ref[0, h] = st_sc[h][...].T


def _hgrn(qs, gl, iv, og, gn, s0, nb, seq, tb):
    ns = seq // tb
    t = nb * seq
    row = pl.BlockSpec((tb, HGRN_WIDTH), lambda b, i: (b * ns + i, 0))
    r = jnp.arange(tb)
    tri = ((r[:, None] // SUB == r[None, :] // SUB) & (r[None, :] <= r[:, None])).astype(BF16)
    in_specs = [row, row, row, row, _const_spec((1, HGRN_DK)), _const_spec((tb, tb))]
    args = [qs, gl, iv, og, gn, tri]
    if s0 is not None:
        in_specs.append(pl.BlockSpec((1, HGRN_HEADS, HGRN_DK, HGRN_DK), lambda b, i: (b, 0, 0, 0)))
        args.append(s0)
    return pl.pallas_call(
        functools.partial(_hgrn_kernel, tb=tb, has_state=s0 is not None),
        grid=(nb, ns), in_specs=in_specs,
        out_specs=[row, pl.BlockSpec((1, HGRN_HEADS, HGRN_DK, HGRN_DK), lambda b, i: (b, 0, 0, 0))],
        out_shape=[jax.ShapeDtypeStruct((t, HGRN_WIDTH), BF16),
                   jax.ShapeDtypeStruct((nb, HGRN_HEADS, HGRN_DK, HGRN_DK), F32)],
        scratch_shapes=[pltpu.VMEM((HGRN_DK, HGRN_DK), F32)] * HGRN_HEADS
                       + [pltpu.VMEM((tb, HGRN_WIDTH), F32), pltpu.VMEM((tb, HGRN_WIDTH), F32)],
        compiler_params=_cparams(("parallel", "arbitrary")), name="hgrn",
    )(*args)


def _post_kernel(x_ref, ya_ref, yb_ref, ga_ref, gb_ref, g1_ref, sh2_ref, sc2_ref, gn2_ref,
                 wpa_ref, wpb_ref, wo_ref, wrh_ref, wrl_ref, br_ref, tri_ref, cin_ref,
                 x1_ref, h2_ref, idx_ref, wt_ref, rank_ref, cout_ref, cnt_sc, *, tm):
    first = jnp.logical_and(pl.program_id(0) == 0, pl.program_id(1) == 0)

    @pl.when(first)
    def _init():
        cnt_sc[...] = cin_ref[...]

    merged = (ga_ref[...].astype(F32) * jnp.dot(ya_ref[...], wpa_ref[...], preferred_element_type=F32)
              + gb_ref[...].astype(F32) * jnp.dot(yb_ref[...], wpb_ref[...], preferred_element_type=F32))
    x1 = x_ref[...] + g1_ref[0] * jnp.dot(merged.astype(BF16), wo_ref[...], preferred_element_type=F32)
    x1_ref[...] = x1
    ms = jnp.mean(x1 * x1, axis=-1, keepdims=True)
    h2 = x1 * lax.rsqrt(ms + NORM_EPS) * gn2_ref[...]
    h2 = h2 * (1.0 + sc2_ref[0]) + sh2_ref[0]
    _store_rows(h2_ref, h2)

    hh = h2.astype(BF16)
    hl = (h2 - hh.astype(F32)).astype(BF16)
    wrh = wrh_ref[...]
    logits = (lax.dot_general(wrh, hh, NT_DIMS, preferred_element_type=F32)
              + lax.dot_general(wrh, hl, NT_DIMS, preferred_element_type=F32)
              + lax.dot_general(wrl_ref[...], hh, NT_DIMS, preferred_element_type=F32))
    scores = _sigmoid(logits)
    biased = scores + br_ref[...]

    b3 = biased.reshape(N_GROUPS, GROUP_SIZE, tm)
    it3 = lax.broadcasted_iota(I32, (N_GROUPS, GROUP_SIZE, tm), 1).astype(F32)
    m1 = jnp.max(b3, axis=1, keepdims=True)
    i1 = jnp.min(jnp.where(b3 == m1, it3, float(GROUP_SIZE)), axis=1, keepdims=True)
    m2 = jnp.max(jnp.where(it3 == i1, -jnp.inf, b3), axis=1, keepdims=True)
    gs = (m1 + m2).reshape(N_GROUPS, tm)

    gi = lax.broadcasted_iota(I32, (N_GROUPS, tm), 0)
    beat = jnp.zeros((N_GROUPS, tm), F32)
    for g in range(N_GROUPS):
        r = gs[g:g + 1, :]
        beat = beat + jnp.where((r > gs) | ((r == gs) & (g < gi)), 1.0, 0.0)
    gpen = jnp.where(beat < TOPK_GROUPS, 0.0, -jnp.inf)
    masked = (b3 + gpen.reshape(N_GROUPS, 1, tm)).reshape(N_EXPERTS, tm)

    ei = lax.broadcasted_iota(I32, (N_EXPERTS, tm), 0).astype(F32)
    idx_rows, w_rows = [], []
    chosen = jnp.zeros((N_EXPERTS, tm), F32)
    for _ in range(TOP_K):
        m = jnp.max(masked, axis=0, keepdims=True)
        ik = jnp.min(jnp.where(masked == m, ei, float(N_EXPERTS)), axis=0, keepdims=True)
        hit = ei == ik
        w_rows.append(jnp.sum(jnp.where(hit, scores, 0.0), axis=0, keepdims=True))
        idx_rows.append(ik)
        masked = jnp.where(hit, -jnp.inf, masked)
        chosen = jnp.where(hit, 1.0, chosen)
    idx_ref[...] = jnp.concatenate(idx_rows, axis=0).astype(I32)
    wts = jnp.concatenate(w_rows, axis=0)
    wt_ref[...] = wts / jnp.sum(wts, axis=0, keepdims=True) * ROUTED_SCALE

    before = cnt_sc[...] + jnp.dot(chosen.astype(BF16), tri_ref[...], preferred_element_type=F32)
    rank_rows = [jnp.sum(jnp.where(ei == idx_rows[kk], before, 0.0), axis=0, keepdims=True)
                 for kk in range(TOP_K)]
    rank_ref[...] = jnp.concatenate(rank_rows, axis=0).astype(I32)
    cnt_sc[...] = cnt_sc[...] + jnp.sum(chosen, axis=1, keepdims=True)
    cout_ref[...] = cnt_sc[...]


def _post(x2, ya, yb, ga, gb, gate1, shift2, scale2, prm, count_in, nb, tm):
    t, d = x2.shape
    ns = t // nb // tm
    row = lambda w: pl.BlockSpec((tm, w), lambda b, i: (b * ns + i, 0))
    colb = pl.BlockSpec((TOP_K, tm), lambda b, i: (0, b * ns + i))
    r = jnp.arange(tm)
    tri = (r[:, None] < r[None, :]).astype(BF16)
    in_specs = [row(d), row(FOX_WIDTH), row(HGRN_WIDTH), row(d), row(d),
                _mod_spec(gate1, tm, ns), _mod_spec(shift2, tm, ns), _mod_spec(scale2, tm, ns),
                _const_spec((1, d)), _const_spec((FOX_WIDTH, d)), _const_spec((HGRN_WIDTH, d)),
                _const_spec((d, d)), _const_spec((N_EXPERTS, d)), _const_spec((N_EXPERTS, d)),
                _const_spec((N_EXPERTS, 1)), _const_spec((tm, tm)), _const_spec((N_EXPERTS, 1))]
    out_shape = [jax.ShapeDtypeStruct((t, d), F32), jax.ShapeDtypeStruct((t * ROW_CHUNKS, LANES), F32),
                 jax.ShapeDtypeStruct((TOP_K, t), I32), jax.ShapeDtypeStruct((TOP_K, t), F32),
                 jax.ShapeDtypeStruct((TOP_K, t), I32), jax.ShapeDtypeStruct((N_EXPERTS, 1), F32)]
    out_specs = [row(d), pl.BlockSpec((tm * ROW_CHUNKS, LANES), lambda b, i: (b * ns + i, 0)),
                 colb, colb, colb, pl.BlockSpec((N_EXPERTS, 1), lambda b, i: (0, 0))]
    return pl.pallas_call(
        functools.partial(_post_kernel, tm=tm), grid=(nb, ns), in_specs=in_specs,
        out_specs=out_specs, out_shape=out_shape,
        scratch_shapes=[pltpu.VMEM((N_EXPERTS, 1), F32)],
        compiler_params=_cparams(("arbitrary", "arbitrary")), name="post",
    )(x2, ya, yb, ga, gb, gate1, shift2, scale2, prm["g2"], prm["wpa"], prm["wpb"], prm["wo"],
      prm["wrh"], prm["wrl"], prm["br"], tri, count_in)


def _pos_kernel(idx_ref, rank_ref, pst_ref, pos_ref, *, tm):
    ei = lax.broadcasted_iota(I32, (N_EXPERTS, tm), 0)
    pst = pst_ref[...]
    rows = [jnp.sum(jnp.where(ei == idx_ref[kk:kk + 1, :], pst, 0.0), axis=0, keepdims=True)
            for kk in range(TOP_K)]
    pos_ref[...] = jnp.concatenate(rows, axis=0).astype(I32) + rank_ref[...]


def _pos(idx, rank, pstart_col, tm):
    t = idx.shape[1]
    colb = pl.BlockSpec((TOP_K, tm), lambda i: (0, i))
    return pl.pallas_call(
        functools.partial(_pos_kernel, tm=tm), grid=(t // tm,),
        in_specs=[colb, colb, _const_spec((N_EXPERTS, 1))], out_specs=colb,
        out_shape=jax.ShapeDtypeStruct((TOP_K, t), I32),
        compiler_params=_cparams(("parallel",)), name="moe_pos",
    )(idx, rank, pstart_col)


def _padfill_kernel(cnt_ref, pad_ref, pst_ref, xs_ref, zero_sc, sem):
    zero_sc[...] = jnp.zeros(zero_sc.shape, zero_sc.dtype)
    sizes = [1 << b for b in reversed(range(MOE_BLK.bit_length() - 1))]

    def chunks(e, act):
        n = pad_ref[e] - cnt_ref[e]
        row = pst_ref[e] + cnt_ref[e]
        for size in sizes:
            @pl.when((n & size) != 0)
            def _():
                act(pltpu.make_async_copy(zero_sc.at[pl.ds(0, size)], xs_ref.at[pl.ds(row, size)], sem))

            row = row + (n & size)

    def issue(e, c):
        chunks(e, lambda cp: cp.start())
        return c

    def drain(e, c):
        chunks(e, lambda cp: cp.wait())
        return c

    lax.fori_loop(0, N_EXPERTS, issue, 0)
    lax.fori_loop(0, N_EXPERTS, drain, 0)


def _padfill(counts, padded, pstart, n_rows, d):
    row_tile = (d // LANES, LANES)
    return pl.pallas_call(
        _padfill_kernel,
        grid_spec=pltpu.PrefetchScalarGridSpec(
            num_scalar_prefetch=3, grid=(1,), in_specs=[],
            out_specs=pl.BlockSpec(memory_space=pl.ANY),
            scratch_shapes=[pltpu.VMEM((MOE_BLK // 2,) + row_tile, F32), pltpu.SemaphoreType.DMA(())]),
        out_shape=jax.ShapeDtypeStruct((n_rows,) + row_tile, F32),
        compiler_params=pltpu.CompilerParams(dimension_semantics=("arbitrary",), has_side_effects=True),
        name="moe_padfill",
    )(counts, padded, pstart)


def _dispatch_kernel(pos_ref, h2_ref, xs_in_ref, xs_ref, sem, *, tm):
    del xs_in_ref

    for t in range(tm):
        for kk in range(TOP_K):
            pltpu.make_async_copy(h2_ref.at[t], xs_ref.at[pos_ref[kk, t]], sem).start(priority=kk % 2)
    for kk in range(TOP_K):
        pltpu.make_async_copy(h2_ref, xs_ref.at[pl.ds(0, tm)], sem).wait()


def _dispatch(pos, h2_flat, xs, tm):
    h2 = h2_flat.reshape(-1, ROW_CHUNKS, LANES)
    t = h2.shape[0]
    return pl.pallas_call(
        functools.partial(_dispatch_kernel, tm=tm),
        grid=(t // tm,),
        in_specs=[pl.BlockSpec((TOP_K, tm), lambda i: (0, i), memory_space=pltpu.SMEM),
                  pl.BlockSpec((tm,) + h2.shape[1:], lambda i: (i, 0, 0)),
                  pl.BlockSpec(memory_space=pl.ANY)],
        out_specs=pl.BlockSpec(memory_space=pl.ANY),
        out_shape=jax.ShapeDtypeStruct(xs.shape, xs.dtype),
        scratch_shapes=[pltpu.SemaphoreType.DMA(())],
        input_output_aliases={2: 0},
        compiler_params=pltpu.CompilerParams(dimension_semantics=("arbitrary",), has_side_effects=True,
                                             vmem_limit_bytes=VMEM_LIMIT),
        name="moe_dispatch",
    )(pos, h2, xs)


def _moe_kernel(be_ref, nu_ref, first_ref, slot_ref, nxt_ref, x_ref, wg_ref, wu_ref, wd_ref, y_ref,
                wg_buf, wu_buf, wd_buf, wgu_sc, wd_sc, sem):
    i = pl.program_id(0)
    e = be_ref[i]

    def fetch(expert, slot):
        return (pltpu.make_async_copy(wg_ref.at[expert], wg_buf.at[slot], sem.at[slot]),
                pltpu.make_async_copy(wu_ref.at[expert], wu_buf.at[slot], sem.at[slot]),
                pltpu.make_async_copy(wd_ref.at[expert], wd_buf.at[slot], sem.at[slot]))

    @pl.when(first_ref[i] == 1)
    def _switch_expert():
        slot = slot_ref[i]

        @pl.when(i == 0)
        def _():
            for cp in fetch(e, slot):
                cp.start()

        for cp in fetch(e, slot):
            cp.wait()
        nxt = nxt_ref[e]

        @pl.when(nxt >= 0)
        def _():
            for cp in fetch(nxt, 1 - slot):
                cp.start()

        wgu_sc[:, 0:D_EXPERT] = wg_buf[slot].astype(BF16)
        wgu_sc[:, D_EXPERT:2 * D_EXPERT] = wu_buf[slot].astype(BF16)
        wd_sc[...] = wd_buf[slot].astype(BF16)

    @pl.when(i < nu_ref[0])
    def _compute():
        gu = jnp.dot(_load_rows(x_ref).astype(BF16), wgu_sc[...], preferred_element_type=F32)
        gt = gu[:, 0:D_EXPERT]
        h = gt * _sigmoid(gt) * gu[:, D_EXPERT:2 * D_EXPERT]
        _store_rows(y_ref, jnp.dot(h.astype(BF16), wd_sc[...], preferred_element_type=F32))


def _moe(blk_expert, n_used, first, slot, nxt, xs_pool, w_gate, w_up, w_down):
    n_rows = xs_pool.shape[0]
    xs = xs_pool.reshape(n_rows * ROW_CHUNKS, LANES)
    d = w_gate.shape[1]
    nblk = n_rows // MOE_BLK
    xmap = lambda i, be, nu, fi, sl, nx: (jnp.minimum(i, nu[0] - 1), 0)
    hbm = pl.BlockSpec(memory_space=pl.ANY)
    return pl.pallas_call(
        _moe_kernel,
        grid_spec=pltpu.PrefetchScalarGridSpec(
            num_scalar_prefetch=5, grid=(nblk,),
            in_specs=[pl.BlockSpec((MOE_BLK * ROW_CHUNKS, LANES), xmap), hbm, hbm, hbm],
            out_specs=pl.BlockSpec((MOE_BLK * ROW_CHUNKS, LANES), xmap),
            scratch_shapes=[pltpu.VMEM((2, d, D_EXPERT), F32), pltpu.VMEM((2, d, D_EXPERT), F32),
                            pltpu.VMEM((2, D_EXPERT, d), F32),
                            pltpu.VMEM((d, 2 * D_EXPERT), BF16), pltpu.VMEM((D_EXPERT, d), BF16),
                            pltpu.SemaphoreType.DMA((2,))]),
        out_shape=jax.ShapeDtypeStruct(xs.shape, F32),
        compiler_params=_cparams(("arbitrary",)), name="moe_experts",
    )(blk_expert, n_used, first, slot, nxt, xs, w_gate, w_up, w_down)


def _combine_kernel(pos_ref, posn_ref, x1_ref, h2_ref, g2_ref, wt_ref, wsgu_ref, wsd_ref, ys_ref, ysflat_ref,
                    o_ref, buf, sem, *, tm):
    lin = pl.program_id(0) * pl.num_programs(1) + pl.program_id(1)
    total = pl.num_programs(0) * pl.num_programs(1)
    slot = lin % 2

    def issue(p_ref, s):
        for t in range(tm):
            for kk in range(TOP_K):
                pltpu.make_async_copy(ys_ref.at[p_ref[kk, t]], buf.at[s, kk, pl.ds(t * ROW_CHUNKS, ROW_CHUNKS)],
                                      sem.at[s]).start(priority=kk % 2)

    @pl.when(lin == 0)
    def _():
        issue(pos_ref, 0)

    issue(posn_ref, 1 - slot)

    gu = jnp.dot(_load_rows(h2_ref).astype(BF16), wsgu_ref[...], preferred_element_type=F32)
    gt = gu[:, 0:D_SHARED]
    hs = gt * _sigmoid(gt) * gu[:, D_SHARED:2 * D_SHARED]
    shared = jnp.dot(hs.astype(BF16), wsd_ref[...], preferred_element_type=F32)

    for kk in range(TOP_K):
        pltpu.make_async_copy(ysflat_ref.at[pl.ds(0, tm * ROW_CHUNKS)], buf.at[slot, kk], sem.at[slot]).wait()

    nchunk = shared.shape[-1] // LANES
    acc = [shared[:, c * LANES:(c + 1) * LANES] for c in range(nchunk)]
    for kk in range(TOP_K):
        wrow = jnp.broadcast_to(wt_ref[kk:kk + 1, :], (LANES, tm))
        wcol = wrow.T
        for c in range(nchunk):
            acc[c] = acc[c] + _row_chunk(buf.at[slot, kk], c)[...] * wcol
    o_ref[...] = x1_ref[...] + g2_ref[0] * jnp.concatenate(acc, axis=1)

    @pl.when(lin + 1 == total)
    def _():
        for kk in range(TOP_K):
            pltpu.make_async_copy(ysflat_ref.at[pl.ds(0, tm * ROW_CHUNKS)], buf.at[1 - slot, kk],
                                  sem.at[1 - slot]).wait()


def _combine(pos, x1, h2, gate2, wts, wsgu, wsd, ys_flat, nb, tm):
    t, d = x1.shape
    ns = t // nb // tm
    row = pl.BlockSpec((tm, d), lambda b, i: (b * ns + i, 0))
    last = nb * ns - 1
    ys = ys_flat.reshape(-1, ROW_CHUNKS, LANES)
    return pl.pallas_call(
        functools.partial(_combine_kernel, tm=tm),
        grid=(nb, ns),
        in_specs=[pl.BlockSpec((TOP_K, tm), lambda b, i: (0, b * ns + i), memory_space=pltpu.SMEM),
                  pl.BlockSpec((TOP_K, tm), lambda b, i: (0, jnp.minimum(b * ns + i + 1, last)),
                               memory_space=pltpu.SMEM),
                  row, pl.BlockSpec((tm * ROW_CHUNKS, LANES), lambda b, i: (b * ns + i, 0)),
                  _mod_spec(gate2, tm, ns),
                  pl.BlockSpec((TOP_K, tm), lambda b, i: (0, b * ns + i)),
                  _const_spec(wsgu.shape), _const_spec(wsd.shape),
                  pl.BlockSpec(memory_space=pl.ANY), pl.BlockSpec(memory_space=pl.ANY)],
        out_specs=row,
        out_shape=jax.ShapeDtypeStruct((t, d), F32),
        scratch_shapes=[pltpu.VMEM((2, TOP_K, tm * ROW_CHUNKS, LANES), F32), pltpu.SemaphoreType.DMA((2,))],
        compiler_params=_cparams(("arbitrary", "arbitrary")), name="moe_combine",
    )(pos, pos, x1, h2, gate2, wts, wsgu, wsd, ys, ys_flat)


def _prepare_params(g_norm1, w_in, b_fox_f, g_q, g_k, hgrn_lb, g_hgrn_o, w_proj_a, w_proj_b, w_out,
                    g_norm2, w_router, b_router, w_sh_gate, w_sh_up, w_sh_down):
    d = D_MODEL
    fw, hw = FOX_WIDTH, HGRN_WIDTH
    c0 = 3 * fw
    c1 = c0 + FOX_HEADS
    c2 = c1 + 4 * hw
    wff = w_in[:, c0:c1]
    head = jnp.arange(fw) // FOX_HEAD_DIM
    wr_t = w_router.T
    wrh = wr_t.astype(BF16)
    return dict(
        g1=g_norm1.reshape(1, d),
        wqkv=w_in[:, :c0].astype(BF16),
        wff=jnp.pad(wff, ((0, 0), (0, LANES - FOX_HEADS))).astype(BF16),
        wfft=jnp.pad(wff.T, ((0, 16 - FOX_HEADS), (0, 0))).astype(BF16),
        bf=b_fox_f.reshape(1, FOX_HEADS), bft=b_fox_f.reshape(FOX_HEADS, 1),
        gq=jnp.tile(g_q, FOX_HEADS).reshape(1, fw), gk=jnp.tile(g_k, FOX_HEADS).reshape(1, fw),
        bd=(head[:, None] == head[None, :]).astype(BF16) * (1.0 / FOX_HEAD_DIM),
        lbp=hgrn_lb,
        wh=w_in[:, c1:c2].astype(BF16), wg=w_in[:, c2:].astype(BF16),
        gn=g_hgrn_o.reshape(1, HGRN_DK),
        g2=g_norm2.reshape(1, d),
        wpa=w_proj_a.astype(BF16), wpb=w_proj_b.astype(BF16), wo=w_out.astype(BF16),
        wrh=wrh, wrl=(wr_t - wrh.astype(F32)).astype(BF16), br=b_router.reshape(N_EXPERTS, 1),
        wsgu=jnp.concatenate([w_sh_gate, w_sh_up], axis=1).astype(BF16), wsd=w_sh_down.astype(BF16),
    )


def _mixers(x2, nb, seq, tm, shift1, scale1, prm, past, tile, tb):
    (q, k, kb, v, vb, lf, lft, hq, hl, hi, og, ga, gb) = _inproj(x2, nb if shift1.shape[1] == 1 else 1, tm,
                                                                 shift1, scale1, prm)
    t = nb * seq
    hp = FOX_HEADS // 2
    lft = lft.transpose(1, 0, 2).reshape(FOX_HEADS, nb, seq).transpose(1, 0, 2)
    if past is None:
        fcum = _cumsum_lanes(lft.reshape(nb * FOX_HEADS, seq)).reshape(nb, hp, 2, seq)
        ya = _fox_prompt(q, kb, vb, fcum, nb, seq, tile, min(32, tile))
        s0 = None
    else:
        cache_k, cache_v, cache_lf, s0 = past
        plen = cache_k.shape[1]
        ltot = plen + seq
        lpad = -(-ltot // LANES) * LANES
        lf_all = jnp.concatenate([cache_lf.transpose(0, 2, 1), lft,
                                  jnp.zeros((nb, FOX_HEADS, lpad - ltot), F32)], axis=-1)
        fcum = _cumsum_lanes(lf_all.reshape(nb * FOX_HEADS, lpad)).reshape(nb, hp, 2, lpad)
        ya = _fox_sample(q, kb, vb, cache_k.reshape(nb, plen, FOX_WIDTH), cache_v.reshape(nb, plen, FOX_WIDTH),
                         fcum, nb, seq, plen)
    yb, sfin = _hgrn(hq, hl, hi, og, prm["gn"], s0, nb, seq, tb)
    return k, v, lf, sfin, ya, yb, ga, gb


def kernel(x_prompt, x_sample, cache_fox_k, cache_fox_v, cache_fox_logf, state_hgrn, c_prompt, c_sample,
           w_ada, b_ada, g_norm1, w_in, b_fox_f, g_q, g_k, hgrn_lb, g_hgrn_o, w_proj_a, w_proj_b, w_out,
           g_norm2, w_router, b_router, w_exp_gate, w_exp_up, w_exp_down, w_sh_gate, w_sh_up, w_sh_down):
    assert w_ada.shape[0] == 1 and hgrn_lb.shape[0] == 2, "single-layer trunk"
    d = D_MODEL
    bp, sp, _ = x_prompt.shape
    bs, ss, _ = x_sample.shape
    tp, ts = bp * sp, bs * ss
    prm = _prepare_params(g_norm1[0], w_in[0], b_fox_f[0], g_q[0], g_k[0], hgrn_lb, g_hgrn_o[0],
                          w_proj_a[0], w_proj_b[0], w_out[0], g_norm2[0], w_router[0], b_router[0],
                          w_sh_gate[0], w_sh_up[0], w_sh_down[0])

    bc = bp + bs
    bc_pad = -(-bc // 8) * 8
    c_all = jnp.concatenate([c_prompt, c_sample, jnp.zeros((bc_pad - bc, d), F32)], axis=0)
    mod = _ada(c_all, w_ada[0], b_ada[0])
    mod_p = [mod[:bp, j * d:(j + 1) * d].reshape(bp, 1, d) for j in range(6)]
    mod_s = [jnp.repeat(mod[bp:bc, j * d:(j + 1) * d], ss, axis=0).reshape(1, ts, d) for j in range(6)]

    tm_p = min(512, sp)
    tm_s = min(256, ts)
    xp2 = x_prompt.reshape(tp, d)
    xs2 = x_sample.reshape(ts, d)
    tile = min(1024, sp)
    tb = min(256, sp)

    kp, vp, lfp, sfin_p, ya_p, yb_p, ga_p, gb_p = _mixers(
        xp2, bp, sp, tm_p, mod_p[0], mod_p[1], prm, None, tile, tb)
    past = (cache_fox_k[0], cache_fox_v[0], cache_fox_logf[0], state_hgrn[0])
    ks, vs, lfs, sfin_s, ya_s, yb_s, ga_s, gb_s = _mixers(
        xs2, bs, ss, tm_s, mod_s[0], mod_s[1], prm, past, None, min(256, ss))

    zero_cnt = jnp.zeros((N_EXPERTS, 1), F32)
    x1_p, h2_p, idx_p, wt_p, rank_p, cnt_p = _post(xp2, ya_p, yb_p, ga_p, gb_p, mod_p[2], mod_p[3], mod_p[4],
                                                    prm, zero_cnt, bp, tm_p)
    x1_s, h2_s, idx_s, wt_s, rank_s, cnt_all = _post(xs2, ya_s, yb_s, ga_s, gb_s, mod_s[2], mod_s[3], mod_s[4],
                                                     prm, cnt_p, 1, tm_s)

    counts = cnt_all.reshape(N_EXPERTS).astype(I32)
    padded = (counts + MOE_BLK - 1) // MOE_BLK * MOE_BLK
    pend = jnp.cumsum(padded)
    pstart = pend - padded
    nblk = -(-((tp + ts) * TOP_K + N_EXPERTS * (MOE_BLK - 1)) // MOE_BLK)
    n_used = (pend[-1] // MOE_BLK).reshape(1)
    blk_row0 = jnp.arange(nblk, dtype=I32) * MOE_BLK
    blk_expert = jnp.minimum(jnp.sum((pend[None, :] <= blk_row0[:, None]).astype(I32), axis=1), N_EXPERTS - 1)
    blk_used = jnp.arange(nblk) < n_used[0]
    blk_expert = jnp.where(blk_used, blk_expert, blk_expert[jnp.maximum(n_used[0] - 1, 0)])
    blk_first = blk_used & jnp.concatenate([jnp.ones((1,), bool), blk_expert[1:] != blk_expert[:-1]])
    blk_slot = ((jnp.cumsum(blk_first.astype(I32)) - 1) & 1).astype(I32)
    eids = jnp.arange(N_EXPERTS, dtype=I32)
    later = (eids[None, :] > eids[:, None]) & (padded[None, :] > 0)
    nxt_expert = jnp.min(jnp.where(later, eids[None, :], N_EXPERTS), axis=1)
    nxt_expert = jnp.where(nxt_expert < N_EXPERTS, nxt_expert, -1).astype(I32)
    pstart_col = pstart.astype(F32).reshape(N_EXPERTS, 1)
    pos_p = _pos(idx_p, rank_p, pstart_col, tm_p)
    pos_s = _pos(idx_s, rank_s, pstart_col, tm_s)

    xs_pool = _padfill(counts, padded, pstart, nblk * MOE_BLK, d)
    xs_pool = _dispatch(pos_p, h2_p, xs_pool, min(128, tm_p))
    xs_pool = _dispatch(pos_s, h2_s, xs_pool, min(128, tm_s))
    ys_pool = _moe(blk_expert, n_used, blk_first.astype(I32), blk_slot, nxt_expert, xs_pool,
                   w_exp_gate[0], w_exp_up[0], w_exp_down[0])

    tm_c = min(128, sp)
    y_p = _combine(pos_p, x1_p, h2_p, mod_p[5], wt_p, prm["wsgu"], prm["wsd"], ys_pool, bp, tm_c)
    y_s = _combine(pos_s, x1_s, h2_s, mod_s[5], wt_s, prm["wsgu"], prm["wsd"], ys_pool, 1, min(128, ts))

    return (y_p.reshape(bp, sp, d), y_s.reshape(bs, ss, d),
            kp.reshape(1, bp, sp, FOX_HEADS, FOX_HEAD_DIM), vp.reshape(1, bp, sp, FOX_HEADS, FOX_HEAD_DIM),
            lfp.reshape(1, bp, sp, FOX_HEADS), sfin_p[None],
            ks.reshape(1, bs, ss, FOX_HEADS, FOX_HEAD_DIM), vs.reshape(1, bs, ss, FOX_HEADS, FOX_HEAD_DIM),
            lfs.reshape(1, bs, ss, FOX_HEADS), sfin_s[None])
```

```python
import functools

import jax
import jax.numpy as jnp
from jax import lax
from jax.experimental import pallas as pl
from jax.experimental.pallas import tpu as pltpu

F32 = jnp.float32
BF16 = jnp.bfloat16
I32 = jnp.int32

D_MODEL = 1024
FOX_HEADS = 8
FOX_HEAD_DIM = 64
FOX_WIDTH = FOX_HEADS * FOX_HEAD_DIM
HGRN_HEADS = 4
HGRN_DK = 128
HGRN_WIDTH = HGRN_HEADS * HGRN_DK
N_EXPERTS = 256
TOP_K = 8
N_GROUPS = 8
TOPK_GROUPS = 4
GROUP_SIZE = N_EXPERTS // N_GROUPS
D_EXPERT = 256
D_SHARED = 256
ROUTED_SCALE = 2.5
NORM_EPS = 1e-6
NEG_BIG = -1e30
LOG2E = 1.4426950408889634
QK_SCALE = FOX_HEAD_DIM ** -0.5 * LOG2E

LANES = 128
SUB = 16
MOE_BLK = 512
VMEM_LIMIT = 56 * 1024 * 1024

NT_DIMS = (((1,), (1,)), ((), ()))
TN_DIMS = (((0,), (0,)), ((), ()))


def _cparams(sem):
    return pltpu.CompilerParams(dimension_semantics=sem, vmem_limit_bytes=VMEM_LIMIT)


def _const_spec(shape):
    nd = len(shape)
    return pl.BlockSpec(shape, lambda *_: (0,) * nd, pipeline_mode=pl.Buffered(1))


def _sigmoid(z):
    return 1.0 / (1.0 + jnp.exp(-z))


def _log_sigmoid(z):
    return jnp.minimum(z, 0.0) - jnp.log(1.0 + jnp.exp(-jnp.abs(z)))


ROW_CHUNKS = D_MODEL // LANES


def _row_chunk(ref, c):
    return ref.at[pl.ds(c, ref.shape[0] // ROW_CHUNKS, stride=ROW_CHUNKS), :]


def _store_rows(ref, val):
    for c in range(ROW_CHUNKS):
        _row_chunk(ref, c)[...] = val[:, c * LANES:(c + 1) * LANES]


def _load_rows(ref):
    return jnp.concatenate([_row_chunk(ref, c)[...] for c in range(ROW_CHUNKS)], axis=1)


def _split3(a):
    hi = a.astype(BF16)
    r1 = a - hi.astype(F32)
    mid = r1.astype(BF16)
    lo = (r1 - mid.astype(F32)).astype(BF16)
    return hi, mid, lo


def _ada_kernel(c_ref, w_ref, b_ref, o_ref):
    c = c_ref[...]
    s = c * _sigmoid(c)
    o_ref[...] = jnp.dot(s.astype(BF16), w_ref[...].astype(BF16), preferred_element_type=F32) + b_ref[...]


def _ada(c_all, w_ada, b_ada):
    bc, d = c_all.shape
    n = w_ada.shape[1]
    tn = 1024
    return pl.pallas_call(
        _ada_kernel,
        grid=(n // tn,),
        in_specs=[pl.BlockSpec((bc, d), lambda j: (0, 0)),
                  pl.BlockSpec((d, tn), lambda j: (0, j)),
                  pl.BlockSpec((1, tn), lambda j: (0, j))],
        out_specs=pl.BlockSpec((bc, tn), lambda j: (0, j)),
        out_shape=jax.ShapeDtypeStruct((bc, n), F32),
        compiler_params=_cparams(("parallel",)),
        name="ada",
    )(c_all, w_ada, b_ada.reshape(1, n))


def _inproj_kernel(x_ref, sh_ref, sc_ref, g1_ref, wqkv_ref, wff_ref, wfft_ref, bf_ref, bft_ref,
                   gq_ref, gk_ref, bd_ref, lbp_ref, wh_ref, wg_ref,
                   q_ref, k_ref, kb_ref, v_ref, vb_ref, lf_ref, lft_ref,
                   hq_ref, hl_ref, hi_ref, og_ref, ga_ref, gb_ref):
    x = x_ref[...]
    ms = jnp.mean(x * x, axis=-1, keepdims=True)
    h = x * lax.rsqrt(ms + NORM_EPS) * g1_ref[...]
    h = h * (1.0 + sc_ref[0]) + sh_ref[0]
    hb = h.astype(BF16)

    def headnorm(a, g):
        ss = jnp.dot((a * a).astype(BF16), bd_ref[...], preferred_element_type=F32)
        return a * lax.rsqrt(ss + NORM_EPS) * g

    fq = jnp.dot(hb, wqkv_ref[:, 0:FOX_WIDTH], preferred_element_type=F32)
    q_ref[...] = (headnorm(fq, gq_ref[...]) * QK_SCALE).astype(BF16)
    fk = jnp.dot(hb, wqkv_ref[:, FOX_WIDTH:2 * FOX_WIDTH], preferred_element_type=F32)
    k = headnorm(fk, gk_ref[...])
    kb_ref[...] = k.astype(BF16)
    fv = jnp.dot(hb, wqkv_ref[:, 2 * FOX_WIDTH:3 * FOX_WIDTH], preferred_element_type=F32)
    vb_ref[...] = fv.astype(BF16)
    for hd in range(FOX_HEADS):
        cols = slice(hd * FOX_HEAD_DIM, (hd + 1) * FOX_HEAD_DIM)
        k_ref[:, hd, :] = k[:, cols]
        v_ref[:, hd, :] = fv[:, cols]

    ff = jnp.dot(hb, wff_ref[...], preferred_element_type=F32)
    lf_ref[...] = _log_sigmoid(ff[:, 0:FOX_HEADS] + bf_ref[...])
    fft = lax.dot_general(wfft_ref[...], hb, NT_DIMS, preferred_element_type=F32)
    lft_ref[0] = _log_sigmoid(fft[0:FOX_HEADS, :] + bft_ref[...])

    lbp = lbp_ref[...]
    e = jnp.exp(lbp - jnp.max(lbp, axis=0, keepdims=True))
    lb = e[0:1, :] / jnp.sum(e, axis=0, keepdims=True)

    w = HGRN_WIDTH
    hq = jnp.dot(hb, wh_ref[:, 0:w], preferred_element_type=F32)
    hq_ref[...] = hq * _sigmoid(hq)
    hf = jnp.dot(hb, wh_ref[:, w:2 * w], preferred_element_type=F32)
    hl_ref[...] = jnp.log(lb + (1.0 - lb) * _sigmoid(hf))
    hi_ref[...] = jnp.dot(hb, wh_ref[:, 2 * w:3 * w], preferred_element_type=F32)
    hg = jnp.dot(hb, wh_ref[:, 3 * w:4 * w], preferred_element_type=F32)
    og_ref[...] = _sigmoid(hg)
    ga = jnp.dot(hb, wg_ref[:, 0:D_MODEL], preferred_element_type=F32)
    ga_ref[...] = _sigmoid(ga).astype(BF16)
    gb = jnp.dot(hb, wg_ref[:, D_MODEL:2 * D_MODEL], preferred_element_type=F32)
    gb_ref[...] = _sigmoid(gb).astype(BF16)


def _mod_spec(arr, tm, ns):
    d = arr.shape[-1]
    if arr.shape[1] == 1:
        return pl.BlockSpec((1, 1, d), lambda b, i: (b, 0, 0))
    return pl.BlockSpec((1, tm, d), lambda b, i: (0, b * ns + i, 0))


def _inproj(x2, nb, tm, shift1, scale1, prm):
    t, d = x2.shape
    ns = t // nb // tm
    row = lambda w: pl.BlockSpec((tm, w), lambda b, i: (b * ns + i, 0))
    in_specs = [row(d), _mod_spec(shift1, tm, ns), _mod_spec(scale1, tm, ns),
                _const_spec((1, d)), _const_spec(prm["wqkv"].shape), _const_spec(prm["wff"].shape),
                _const_spec(prm["wfft"].shape), _const_spec((1, FOX_HEADS)), _const_spec((FOX_HEADS, 1)),
                _const_spec((1, FOX_WIDTH)), _const_spec((1, FOX_WIDTH)), _const_spec((FOX_WIDTH, FOX_WIDTH)),
                _const_spec(prm["lbp"].shape), _const_spec(prm["wh"].shape), _const_spec(prm["wg"].shape)]
    fw, hw = FOX_WIDTH, HGRN_WIDTH
    heads = pl.BlockSpec((tm, FOX_HEADS, FOX_HEAD_DIM), lambda b, i: (b * ns + i, 0, 0))
    out_shape = [jax.ShapeDtypeStruct((t, fw), BF16),
                 jax.ShapeDtypeStruct((t, FOX_HEADS, FOX_HEAD_DIM), F32),
                 jax.ShapeDtypeStruct((t, fw), BF16),
                 jax.ShapeDtypeStruct((t, FOX_HEADS, FOX_HEAD_DIM), F32),
                 jax.ShapeDtypeStruct((t, fw), BF16),
                 jax.ShapeDtypeStruct((t, FOX_HEADS), F32),
                 jax.ShapeDtypeStruct((nb * ns, FOX_HEADS, tm), F32),
                 jax.ShapeDtypeStruct((t, hw), F32),
                 jax.ShapeDtypeStruct((t, hw), F32),
                 jax.ShapeDtypeStruct((t, hw), F32),
                 jax.ShapeDtypeStruct((t, hw), F32),
                 jax.ShapeDtypeStruct((t, d), BF16),
                 jax.ShapeDtypeStruct((t, d), BF16)]
    out_specs = [row(fw), heads, row(fw), heads, row(fw), row(FOX_HEADS),
                 pl.BlockSpec((1, FOX_HEADS, tm), lambda b, i: (b * ns + i, 0, 0)),
                 row(hw), row(hw), row(hw), row(hw), row(d), row(d)]
    return pl.pallas_call(
        _inproj_kernel, grid=(nb, ns), in_specs=in_specs, out_specs=out_specs, out_shape=out_shape,
        compiler_params=_cparams(("parallel", "parallel")), name="inproj",
    )(x2, shift1, scale1, prm["g1"], prm["wqkv"], prm["wff"], prm["wfft"], prm["bf"], prm["bft"],
      prm["gq"], prm["gk"], prm["bd"], prm["lbp"], prm["wh"], prm["wg"])


def _cumsum_kernel(x_ref, o_ref):
    x = x_ref[...]
    n = x.shape[-1]
    lane = lax.broadcasted_iota(I32, x.shape, 1)
    s = 1
    while s < n:
        x = x + jnp.where(lane >= s, pltpu.roll(x, s, axis=1), 0.0)
        s *= 2
    o_ref[...] = x * LOG2E


def _cumsum_lanes(x):
    return pl.pallas_call(
        _cumsum_kernel, out_shape=jax.ShapeDtypeStruct(x.shape, F32),
        compiler_params=pltpu.CompilerParams(vmem_limit_bytes=VMEM_LIMIT), name="cumsum",
    )(x)


FOX_PAIRS_PER_STEP = 2


def _fox_prompt_kernel(qt_ref, kt_ref, q_ref, k_ref, v_ref, fk_ref, o_ref, *scratch, tile, rsub):
    step = pl.program_id(2)
    qi = qt_ref[step]
    ki = kt_ref[step]
    nheads = 2 * FOX_PAIRS_PER_STEP
    heads = [scratch[4 * h:4 * h + 4] for h in range(nheads)]
    lane = lax.broadcasted_iota(I32, (1, LANES), 1)
    lo_half = lane < FOX_HEAD_DIM

    @pl.when(ki == 0)
    def _init():
        for m_sc, a_sc, _, _ in heads:
            m_sc[...] = jnp.full(m_sc.shape, NEG_BIG, F32)
            a_sc[...] = jnp.zeros(a_sc.shape, F32)

    def tile_update(diagonal):
        nsub = tile // rsub
        if diagonal:
            ahead = (lax.broadcasted_iota(I32, (rsub, tile), 1)
                     - lax.broadcasted_iota(I32, (rsub, tile), 0))
        for h, (_, _, s_sc, _) in enumerate(heads):
            lanes = slice((h // 2) * LANES, (h // 2 + 1) * LANES)
            q = q_ref[:, lanes]
            sel = lo_half if h % 2 == 0 else jnp.logical_not(lo_half)
            qh = jnp.where(sel, q, jnp.zeros_like(q))
            s_sc[...] = lax.dot_general(qh, k_ref[:, lanes], NT_DIMS, preferred_element_type=F32)
        for h, (m_sc, a_sc, s_sc, p_sc) in enumerate(heads):
            lanes = slice((h // 2) * LANES, (h // 2 + 1) * LANES)
            v = v_ref[:, lanes]
            fk = fk_ref[0, h // 2]
            hs = h % 2
            sel = lo_half if hs == 0 else jnp.logical_not(lo_half)
            v_ones = jnp.where(sel, v, jnp.ones_like(v))

            def biased(r):
                rows = slice(r * rsub, (r + 1) * rsub)
                cw = min(tile, -(-((r + 1) * rsub) // LANES) * LANES) if diagonal else tile
                s = s_sc[rows, 0:cw] - fk[hs:hs + 1, 0:cw]
                if diagonal:
                    s = jnp.where(ahead[:, 0:cw] <= r * rsub, s, NEG_BIG)
                return s, rows, cw

            m_old = m_sc[...]
            m_parts = []
            for r in range(nsub):
                s, rows, _ = biased(r)
                m_parts.append(jnp.maximum(m_old[rows], jnp.max(s, axis=1, keepdims=True)))
            for r in range(nsub):
                s, rows, cw = biased(r)
                nc = cw // LANES
                p = jnp.exp2(s - jnp.concatenate([m_parts[r]] * nc, axis=1))
                p_sc[rows, 0:cw] = p.astype(BF16)
                if cw < tile:
                    p_sc[rows, cw:tile] = jnp.zeros((rsub, tile - cw), BF16)
            m_new = jnp.concatenate(m_parts, axis=0)
            alpha = jnp.exp2(m_old - m_new)
            m_sc[...] = m_new
            a_sc[...] = alpha * a_sc[...] + jnp.dot(p_sc[...], v_ones, preferred_element_type=F32)

    @pl.when(ki < qi)
    def _full():
        tile_update(False)

    @pl.when(ki == qi)
    def _diag():
        tile_update(True)
        for pair in range(FOX_PAIRS_PER_STEP):
            acc0 = heads[2 * pair][1][...]
            acc1 = heads[2 * pair + 1][1][...]
            o0 = acc0 / pltpu.roll(acc0, FOX_HEAD_DIM, axis=1)
            o1 = acc1 / pltpu.roll(acc1, FOX_HEAD_DIM, axis=1)
            o_ref[:, pair * LANES:(pair + 1) * LANES] = jnp.where(lo_half, o0, o1).astype(o_ref.dtype)


def _fox_prompt(q, kb, vb, fcum, nb, seq, tile, rsub):
    nt = seq // tile
    pps = FOX_PAIRS_PER_STEP
    hp = FOX_HEADS // 2 // pps
    width = pps * LANES
    qt = jnp.asarray([qi for qi in range(nt) for _ in range(qi + 1)], I32)
    kt = jnp.asarray([ki for qi in range(nt) for ki in range(qi + 1)], I32)
    qmap = lambda b, h, s, qt, kt: (b * nt + qt[s], h)
    kmap = lambda b, h, s, qt, kt: (b * nt + kt[s], h)
    fmap = lambda b, h, s, qt, kt: (b, h, 0, kt[s])
    per_head = [pltpu.VMEM((tile, LANES), F32), pltpu.VMEM((tile, LANES), F32),
                pltpu.VMEM((tile, tile), F32), pltpu.VMEM((tile, tile), BF16)]
    return pl.pallas_call(
        functools.partial(_fox_prompt_kernel, tile=tile, rsub=rsub),
        grid_spec=pltpu.PrefetchScalarGridSpec(
            num_scalar_prefetch=2, grid=(nb, hp, nt * (nt + 1) // 2),
            in_specs=[pl.BlockSpec((tile, width), qmap), pl.BlockSpec((tile, width), kmap),
                      pl.BlockSpec((tile, width), kmap), pl.BlockSpec((1, pps, 2, tile), fmap)],
            out_specs=pl.BlockSpec((tile, width), qmap),
            scratch_shapes=per_head * (2 * pps)),
        out_shape=jax.ShapeDtypeStruct((nb * seq, FOX_WIDTH), BF16),
        compiler_params=_cparams(("parallel", "parallel", "arbitrary")),
        name="fox_prompt",
    )(qt, kt, q, kb, vb, fcum)


def _fox_sample_kernel(q_ref, kn_ref, vn_ref, kp_ref, vp_ref, fk_ref, o_ref, *, past, seq):
    lane = lax.broadcasted_iota(I32, (1, LANES), 1)
    lo_half = lane < FOX_HEAD_DIM
    q = q_ref[...]
    kn = kn_ref[...]
    vn = vn_ref[...]
    kp = kp_ref[0].astype(BF16)
    vp = vp_ref[0].astype(BF16)
    fk = fk_ref[0, 0]
    row = lax.broadcasted_iota(I32, (seq, seq), 0)
    col = lax.broadcasted_iota(I32, (seq, seq), 1)
    causal = col <= row
    outs = []
    for hs in range(2):
        sel = lo_half if hs == 0 else jnp.logical_not(lo_half)
        qh = jnp.where(sel, q, jnp.zeros_like(q))
        sp = lax.dot_general(qh, kp, NT_DIMS, preferred_element_type=F32) - fk[hs:hs + 1, 0:past]
        sn = lax.dot_general(qh, kn, NT_DIMS, preferred_element_type=F32) - fk[hs:hs + 1, past:past + seq]
        sn = jnp.where(causal, sn, NEG_BIG)
        m = jnp.maximum(jnp.max(sp, axis=1, keepdims=True), jnp.max(sn, axis=1, keepdims=True))
        pp = jnp.exp2(sp - m)
        pn = jnp.exp2(sn - m)
        l = jnp.sum(pp, axis=1, keepdims=True) + jnp.sum(pn, axis=1, keepdims=True)
        o = (jnp.dot(pp.astype(BF16), vp, preferred_element_type=F32)
             + jnp.dot(pn.astype(BF16), vn, preferred_element_type=F32))
        outs.append(o / l)
    o_ref[...] = jnp.where(lo_half, outs[0], outs[1]).astype(o_ref.dtype)


def _fox_sample(q, kb, vb, cache_k, cache_v, fcum, nb, seq, past):
    hp = FOX_HEADS // 2
    lpad = fcum.shape[-1]
    rmap = lambda b, h: (b, h)
    cmap = lambda b, h: (b, 0, h)
    return pl.pallas_call(
        functools.partial(_fox_sample_kernel, past=past, seq=seq),
        grid=(nb, hp),
        in_specs=[pl.BlockSpec((seq, LANES), rmap), pl.BlockSpec((seq, LANES), rmap),
                  pl.BlockSpec((seq, LANES), rmap),
                  pl.BlockSpec((1, past, LANES), cmap), pl.BlockSpec((1, past, LANES), cmap),
                  pl.BlockSpec((1, 1, 2, lpad), lambda b, h: (b, h, 0, 0))],
        out_specs=pl.BlockSpec((seq, LANES), rmap),
        out_shape=jax.ShapeDtypeStruct((nb * seq, FOX_WIDTH), BF16),
        compiler_params=_cparams(("parallel", "parallel")),
        name="fox_sample",
    )(q, kb, vb, cache_k, cache_v, fcum)


def _hgrn_kernel(*refs, tb, has_state):
    if has_state:
        q_ref, g_ref, i_ref, og_ref, gn_ref, tri_ref, s0_ref, y_ref, sfin_ref, *st_sc, b_sc, o_sc = refs
    else:
        q_ref, g_ref, i_ref, og_ref, gn_ref, tri_ref, y_ref, sfin_ref, *st_sc, b_sc, o_sc = refs
        s0_ref = None
    step = pl.program_id(1)
    nstep = pl.num_programs(1)

    @pl.when(step == 0)
    def _init():
        for h in range(HGRN_HEADS):
            if has_state:
                st_sc[h][...] = s0_ref[0, h].T
            else:
                st_sc[h][...] = jnp.zeros((HGRN_DK, HGRN_DK), F32)

    g = g_ref[...]
    tri = tri_ref[...]
    b = None
    for part in _split3(g):
        pb = jnp.dot(tri, part, preferred_element_type=F32)
        b = pb if b is None else b + pb
    b_sc[...] = b

    trow = lax.broadcasted_iota(I32, (SUB, 1), 0)

    def sub_chunk(c, carry):
        r0 = pl.multiple_of(c * SUB, SUB)
        for h in range(HGRN_HEADS):
            cs = slice(h * HGRN_DK, (h + 1) * HGRN_DK)
            q = q_ref[pl.ds(r0, SUB), cs]
            gg = g_ref[pl.ds(r0, SUB), cs]
            iv = i_ref[pl.ds(r0, SUB), cs]
            bb = b_sc[pl.ds(r0, SUB), cs]
            kk = 1.0 - jnp.exp(gg)
            st = st_sc[h][...]
            o = lax.dot_general((q * jnp.exp(bb)).astype(BF16), st.astype(BF16), NT_DIMS,
                                preferred_element_type=F32)
            for s in range(SUB):
                lo = 0 if s < 8 else 8
                ks = kk[s:s + 1, :]
                bs = bb[s:s + 1, :]
                ivs = iv[s:s + 1, :]
                e = jnp.exp(jnp.where(trow[lo:] >= s, bb[lo:] - bs, NEG_BIG))
                a = jnp.sum(q[lo:] * ks * e, axis=-1, keepdims=True)
                upd = a * ivs
                if lo:
                    upd = jnp.concatenate([jnp.zeros((8, HGRN_DK), F32), upd], axis=0)
                o = o + upd
            o_sc[pl.ds(r0, SUB), cs] = o
            bl = bb[SUB - 1:SUB, :]
            kd = kk * jnp.exp(bl - bb)
            u = lax.dot_general(iv.astype(BF16), kd.astype(BF16), TN_DIMS, preferred_element_type=F32)
            st_sc[h][...] = st * jnp.exp(bl) + u
        return carry

    lax.fori_loop(0, tb // SUB, sub_chunk, 0, unroll=min(4, tb // SUB))

    for h in range(HGRN_HEADS):
        cs = slice(h * HGRN_DK, (h + 1) * HGRN_DK)
        o = o_sc[:, cs]
        ms = jnp.mean(o * o, axis=-1, keepdims=True)
        y_ref[:, cs] = (o * lax.rsqrt(ms + NORM_EPS) * gn_ref[...] * og_ref[:, cs]).astype(y_ref.dtype)

    @pl.when(step == nstep - 1)
    def _fin():
        for h in range(HGRN_HEADS):
            sfin_ref[0, h] = st_sc[h][...].T


def _hgrn(qs, gl, iv, og, gn, s0, nb, seq, tb):
    ns = seq // tb
    t = nb * seq
    row = pl.BlockSpec((tb, HGRN_WIDTH), lambda b, i: (b * ns + i, 0))
    r = jnp.arange(tb)
    tri = ((r[:, None] // SUB == r[None, :] // SUB) & (r[None, :] <= r[:, None])).astype(BF16)
    in_specs = [row, row, row, row, _const_spec((1, HGRN_DK)), _const_spec((tb, tb))]
    args = [qs, gl, iv, og, gn, tri]
    if s0 is not None:
        in_specs.append(pl.BlockSpec((1, HGRN_HEADS, HGRN_DK, HGRN_DK), lambda b, i: (b, 0, 0, 0)))
        args.append(s0)
    return pl.pallas_call(
        functools.partial(_hgrn_kernel, tb=tb, has_state=s0 is not None),
        grid=(nb, ns), in_specs=in_specs,
        out_specs=[row, pl.BlockSpec((1, HGRN_HEADS, HGRN_DK, HGRN_DK), lambda b, i: (b, 0, 0, 0))],
        out_shape=[jax.ShapeDtypeStruct((t, HGRN_WIDTH), BF16),
                   jax.ShapeDtypeStruct((nb, HGRN_HEADS, HGRN_DK, HGRN_DK), F32)],
        scratch_shapes=[pltpu.VMEM((HGRN_DK, HGRN_DK), F32)] * HGRN_HEADS
                       + [pltpu.VMEM((tb, HGRN_WIDTH), F32), pltpu.VMEM((tb, HGRN_WIDTH), F32)],
        compiler_params=_cparams(("parallel", "arbitrary")), name="hgrn",
    )(*args)


def _post_kernel(x_ref, ya_ref, yb_ref, ga_ref, gb_ref, g1_ref, sh2_ref, sc2_ref, gn2_ref,
                 wpa_ref, wpb_ref, wo_ref, wrh_ref, wrl_ref, br_ref, tri_ref, cin_ref,
                 x1_ref, h2_ref, idx_ref, wt_ref, rank_ref, cout_ref, cnt_sc, *, tm):
    first = jnp.logical_and(pl.program_id(0) == 0, pl.program_id(1) == 0)

    @pl.when(first)
    def _init():
        cnt_sc[...] = cin_ref[...]

    merged = (ga_ref[...].astype(F32) * jnp.dot(ya_ref[...], wpa_ref[...], preferred_element_type=F32)
              + gb_ref[...].astype(F32) * jnp.dot(yb_ref[...], wpb_ref[...], preferred_element_type=F32))
    x1 = x_ref[...] + g1_ref[0] * jnp.dot(merged.astype(BF16), wo_ref[...], preferred_element_type=F32)
    x1_ref[...] = x1
    ms = jnp.mean(x1 * x1, axis=-1, keepdims=True)
    h2 = x1 * lax.rsqrt(ms + NORM_EPS) * gn2_ref[...]
    h2 = h2 * (1.0 + sc2_ref[0]) + sh2_ref[0]
    _store_rows(h2_ref, h2)

    hh = h2.astype(BF16)
    hl = (h2 - hh.astype(F32)).astype(BF16)
    wrh = wrh_ref[...]
    logits = (lax.dot_general(wrh, hh, NT_DIMS, preferred_element_type=F32)
              + lax.dot_general(wrh, hl, NT_DIMS, preferred_element_type=F32)
              + lax.dot_general(wrl_ref[...], hh, NT_DIMS, preferred_element_type=F32))
    scores = _sigmoid(logits)
    biased = scores + br_ref[...]

    b3 = biased.reshape(N_GROUPS, GROUP_SIZE, tm)
    it3 = lax.broadcasted_iota(I32, (N_GROUPS, GROUP_SIZE, tm), 1).astype(F32)
    m1 = jnp.max(b3, axis=1, keepdims=True)
    i1 = jnp.min(jnp.where(b3 == m1, it3, float(GROUP_SIZE)), axis=1, keepdims=True)
    m2 = jnp.max(jnp.where(it3 == i1, -jnp.inf, b3), axis=1, keepdims=True)
    gs = (m1 + m2).reshape(N_GROUPS, tm)

    gi = lax.broadcasted_iota(I32, (N_GROUPS, tm), 0)
    beat = jnp.zeros((N_GROUPS, tm), F32)
    for g in range(N_GROUPS):
        r = gs[g:g + 1, :]
        beat = beat + jnp.where((r > gs) | ((r == gs) & (g < gi)), 1.0, 0.0)
    gpen = jnp.where(beat < TOPK_GROUPS, 0.0, -jnp.inf)
    masked = (b3 + gpen.reshape(N_GROUPS, 1, tm)).reshape(N_EXPERTS, tm)

    ei = lax.broadcasted_iota(I32, (N_EXPERTS, tm), 0).astype(F32)
    idx_rows, w_rows = [], []
    chosen = jnp.zeros((N_EXPERTS, tm), F32)
    for _ in range(TOP_K):
        m = jnp.max(masked, axis=0, keepdims=True)
        ik = jnp.min(jnp.where(masked == m, ei, float(N_EXPERTS)), axis=0, keepdims=True)
        hit = ei == ik
        w_rows.append(jnp.sum(jnp.where(hit, scores, 0.0), axis=0, keepdims=True))
        idx_rows.append(ik)
        masked = jnp.where(hit, -jnp.inf, masked)
        chosen = jnp.where(hit, 1.0, chosen)
    idx_ref[...] = jnp.concatenate(idx_rows, axis=0).astype(I32)
    wts = jnp.concatenate(w_rows, axis=0)
    wt_ref[...] = wts / jnp.sum(wts, axis=0, keepdims=True) * ROUTED_SCALE

    before = cnt_sc[...] + jnp.dot(chosen.astype(BF16), tri_ref[...], preferred_element_type=F32)
    rank_rows = [jnp.sum(jnp.where(ei == idx_rows[kk], before, 0.0), axis=0, keepdims=True)
                 for kk in range(TOP_K)]
    rank_ref[...] = jnp.concatenate(rank_rows, axis=0).astype(I32)
    cnt_sc[...] = cnt_sc[...] + jnp.sum(chosen, axis=1, keepdims=True)
    cout_ref[...] = cnt_sc[...]


def _post(x2, ya, yb, ga, gb, gate1, shift2, scale2, prm, count_in, nb, tm):
    t, d = x2.shape
    ns = t // nb // tm
    row = lambda w: pl.BlockSpec((tm, w), lambda b, i: (b * ns + i, 0))
    colb = pl.BlockSpec((TOP_K, tm), lambda b, i: (0, b * ns + i))
    r = jnp.arange(tm)
    tri = (r[:, None] < r[None, :]).astype(BF16)
    in_specs = [row(d), row(FOX_WIDTH), row(HGRN_WIDTH), row(d), row(d),
                _mod_spec(gate1, tm, ns), _mod_spec(shift2, tm, ns), _mod_spec(scale2, tm, ns),
                _const_spec((1, d)), _const_spec((FOX_WIDTH, d)), _const_spec((HGRN_WIDTH, d)),
                _const_spec((d, d)), _const_spec((N_EXPERTS, d)), _const_spec((N_EXPERTS, d)),
                _const_spec((N_EXPERTS, 1)), _const_spec((tm, tm)), _const_spec((N_EXPERTS, 1))]
    out_shape = [jax.ShapeDtypeStruct((t, d), F32), jax.ShapeDtypeStruct((t * ROW_CHUNKS, LANES), F32),
                 jax.ShapeDtypeStruct((TOP_K, t), I32), jax.ShapeDtypeStruct((TOP_K, t), F32),
                 jax.ShapeDtypeStruct((TOP_K, t), I32), jax.ShapeDtypeStruct((N_EXPERTS, 1), F32)]
    out_specs = [row(d), pl.BlockSpec((tm * ROW_CHUNKS, LANES), lambda b, i: (b * ns + i, 0)),
                 colb, colb, colb, pl.BlockSpec((N_EXPERTS, 1), lambda b, i: (0, 0))]
    return pl.pallas_call(
        functools.partial(_post_kernel, tm=tm), grid=(nb, ns), in_specs=in_specs,
        out_specs=out_specs, out_shape=out_shape,
        scratch_shapes=[pltpu.VMEM((N_EXPERTS, 1), F32)],
        compiler_params=_cparams(("arbitrary", "arbitrary")), name="post",
    )(x2, ya, yb, ga, gb, gate1, shift2, scale2, prm["g2"], prm["wpa"], prm["wpb"], prm["wo"],
      prm["wrh"], prm["wrl"], prm["br"], tri, count_in)


def _pos_kernel(idx_ref, rank_ref, pst_ref, pos_ref, *, tm):
    ei = lax.broadcasted_iota(I32, (N_EXPERTS, tm), 0)
    pst = pst_ref[...]
    rows = [jnp.sum(jnp.where(ei == idx_ref[kk:kk + 1, :], pst, 0.0), axis=0, keepdims=True)
            for kk in range(TOP_K)]
    pos_ref[...] = jnp.concatenate(rows, axis=0).astype(I32) + rank_ref[...]


def _pos(idx, rank, pstart_col, tm):
    t = idx.shape[1]
    colb = pl.BlockSpec((TOP_K, tm), lambda i: (0, i))
    return pl.pallas_call(
        functools.partial(_pos_kernel, tm=tm), grid=(t // tm,),
        in_specs=[colb, colb, _const_spec((N_EXPERTS, 1))], out_specs=colb,
        out_shape=jax.ShapeDtypeStruct((TOP_K, t), I32),
        compiler_params=_cparams(("parallel",)), name="moe_pos",
    )(idx, rank, pstart_col)


def _padfill_kernel(cnt_ref, pad_ref, pst_ref, xs_ref, zero_sc, sem):
    zero_sc[...] = jnp.zeros(zero_sc.shape, zero_sc.dtype)
    sizes = [1 << b for b in reversed(range(MOE_BLK.bit_length() - 1))]

    def chunks(e, act):
        n = pad_ref[e] - cnt_ref[e]
        row = pst_ref[e] + cnt_ref[e]
        for size in sizes:
            @pl.when((n & size) != 0)
            def _():
                act(pltpu.make_async_copy(zero_sc.at[pl.ds(0, size)], xs_ref.at[pl.ds(row, size)], sem))

            row = row + (n & size)

    def issue(e, c):
        chunks(e, lambda cp: cp.start())
        return c

    def drain(e, c):
        chunks(e, lambda cp: cp.wait())
        return c

    lax.fori_loop(0, N_EXPERTS, issue, 0)
    lax.fori_loop(0, N_EXPERTS, drain, 0)


def _padfill(counts, padded, pstart, n_rows, d):
    row_tile = (d // LANES, LANES)
    return pl.pallas_call(
        _padfill_kernel,
        grid_spec=pltpu.PrefetchScalarGridSpec(
            num_scalar_prefetch=3, grid=(1,), in_specs=[],
            out_specs=pl.BlockSpec(memory_space=pl.ANY),
            scratch_shapes=[pltpu.VMEM((MOE_BLK // 2,) + row_tile, F32), pltpu.SemaphoreType.DMA(())]),
        out_shape=jax.ShapeDtypeStruct((n_rows,) + row_tile, F32),
        compiler_params=pltpu.CompilerParams(dimension_semantics=("arbitrary",), has_side_effects=True),
        name="moe_padfill",
    )(counts, padded, pstart)


def _dispatch_kernel(pos_ref, h2_ref, xs_in_ref, xs_ref, sem, *, tm):
    del xs_in_ref

    for t in range(tm):
        for kk in range(TOP_K):
            pltpu.make_async_copy(h2_ref.at[t], xs_ref.at[pos_ref[kk, t]], sem).start(priority=kk % 2)
    for kk in range(TOP_K):
        pltpu.make_async_copy(h2_ref, xs_ref.at[pl.ds(0, tm)], sem).wait()


def _dispatch(pos, h2_flat, xs, tm):
    h2 = h2_flat.reshape(-1, ROW_CHUNKS, LANES)
    t = h2.shape[0]
    return pl.pallas_call(
        functools.partial(_dispatch_kernel, tm=tm),
        grid=(t // tm,),
        in_specs=[pl.BlockSpec((TOP_K, tm), lambda i: (0, i), memory_space=pltpu.SMEM),
                  pl.BlockSpec((tm,) + h2.shape[1:], lambda i: (i, 0, 0)),
                  pl.BlockSpec(memory_space=pl.ANY)],
        out_specs=pl.BlockSpec(memory_space=pl.ANY),
        out_shape=jax.ShapeDtypeStruct(xs.shape, xs.dtype),
        scratch_shapes=[pltpu.SemaphoreType.DMA(())],
        input_output_aliases={2: 0},
        compiler_params=pltpu.CompilerParams(dimension_semantics=("arbitrary",), has_side_effects=True,
                                             vmem_limit_bytes=VMEM_LIMIT),
        name="moe_dispatch",
    )(pos, h2, xs)


def _moe_kernel(be_ref, nu_ref, first_ref, slot_ref, nxt_ref, x_ref, wg_ref, wu_ref, wd_ref, y_ref,
                wg_buf, wu_buf, wd_buf, wgu_sc, wd_sc, sem):
    i = pl.program_id(0)
    e = be_ref[i]

    def fetch(expert, slot):
        return (pltpu.make_async_copy(wg_ref.at[expert], wg_buf.at[slot], sem.at[slot]),
                pltpu.make_async_copy(wu_ref.at[expert], wu_buf.at[slot], sem.at[slot]),
                pltpu.make_async_copy(wd_ref.at[expert], wd_buf.at[slot], sem.at[slot]))

    @pl.when(first_ref[i] == 1)
    def _switch_expert():
        slot = slot_ref[i]

        @pl.when(i == 0)
        def _():
            for cp in fetch(e, slot):
                cp.start()

        for cp in fetch(e, slot):
            cp.wait()
        nxt = nxt_ref[e]

        @pl.when(nxt >= 0)
        def _():
            for cp in fetch(nxt, 1 - slot):
                cp.start()

        wgu_sc[:, 0:D_EXPERT] = wg_buf[slot].astype(BF16)
        wgu_sc[:, D_EXPERT:2 * D_EXPERT] = wu_buf[slot].astype(BF16)
        wd_sc[...] = wd_buf[slot].astype(BF16)

    @pl.when(i < nu_ref[0])
    def _compute():
        gu = jnp.dot(_load_rows(x_ref).astype(BF16), wgu_sc[...], preferred_element_type=F32)
        gt = gu[:, 0:D_EXPERT]
        h = gt * _sigmoid(gt) * gu[:, D_EXPERT:2 * D_EXPERT]
        _store_rows(y_ref, jnp.dot(h.astype(BF16), wd_sc[...], preferred_element_type=F32))


def _moe(blk_expert, n_used, first, slot, nxt, xs_pool, w_gate, w_up, w_down):
    n_rows = xs_pool.shape[0]
    xs = xs_pool.reshape(n_rows * ROW_CHUNKS, LANES)
    d = w_gate.shape[1]
    nblk = n_rows // MOE_BLK
    xmap = lambda i, be, nu, fi, sl, nx: (jnp.minimum(i, nu[0] - 1), 0)
    hbm = pl.BlockSpec(memory_space=pl.ANY)
    return pl.pallas_call(
        _moe_kernel,
        grid_spec=pltpu.PrefetchScalarGridSpec(
            num_scalar_prefetch=5, grid=(nblk,),
            in_specs=[pl.BlockSpec((MOE_BLK * ROW_CHUNKS, LANES), xmap), hbm, hbm, hbm],
            out_specs=pl.BlockSpec((MOE_BLK * ROW_CHUNKS, LANES), xmap),
            scratch_shapes=[pltpu.VMEM((2, d, D_EXPERT), F32), pltpu.VMEM((2, d, D_EXPERT), F32),
                            pltpu.VMEM((2, D_EXPERT, d), F32),
                            pltpu.VMEM((d, 2 * D_EXPERT), BF16), pltpu.VMEM((D_EXPERT, d), BF16),
                            pltpu.SemaphoreType.DMA((2,))]),
        out_shape=jax.ShapeDtypeStruct(xs.shape, F32),
        compiler_params=_cparams(("arbitrary",)), name="moe_experts",
    )(blk_expert, n_used, first, slot, nxt, xs, w_gate, w_up, w_down)


def _combine_kernel(pos_ref, posn_ref, x1_ref, h2_ref, g2_ref, wt_ref, wsgu_ref, wsd_ref, ys_ref, ysflat_ref,
                    o_ref, buf, sem, *, tm):
    lin = pl.program_id(0) * pl.num_programs(1) + pl.program_id(1)
    total = pl.num_programs(0) * pl.num_programs(1)
    slot = lin % 2

    def issue(p_ref, s):
        for t in range(tm):
            for kk in range(TOP_K):
                pltpu.make_async_copy(ys_ref.at[p_ref[kk, t]], buf.at[s, kk, pl.ds(t * ROW_CHUNKS, ROW_CHUNKS)],
                                      sem.at[s]).start(priority=kk % 2)

    @pl.when(lin == 0)
    def _():
        issue(pos_ref, 0)

    issue(posn_ref, 1 - slot)

    gu = jnp.dot(_load_rows(h2_ref).astype(BF16), wsgu_ref[...], preferred_element_type=F32)
    gt = gu[:, 0:D_SHARED]
    hs = gt * _sigmoid(gt) * gu[:, D_SHARED:2 * D_SHARED]
    shared = jnp.dot(hs.astype(BF16), wsd_ref[...], preferred_element_type=F32)

    for kk in range(TOP_K):
        pltpu.make_async_copy(ysflat_ref.at[pl.ds(0, tm * ROW_CHUNKS)], buf.at[slot, kk], sem.at[slot]).wait()

    nchunk = shared.shape[-1] // LANES
    acc = [shared[:, c * LANES:(c + 1) * LANES] for c in range(nchunk)]
    for kk in range(TOP_K):
        wrow = jnp.broadcast_to(wt_ref[kk:kk + 1, :], (LANES, tm))
        wcol = wrow.T
        for c in range(nchunk):
            acc[c] = acc[c] + _row_chunk(buf.at[slot, kk], c)[...] * wcol
    o_ref[...] = x1_ref[...] + g2_ref[0] * jnp.concatenate(acc, axis=1)

    @pl.when(lin + 1 == total)
    def _():
        for kk in range(TOP_K):
            pltpu.make_async_copy(ysflat_ref.at[pl.ds(0, tm * ROW_CHUNKS)], buf.at[1 - slot, kk],
                                  sem.at[1 - slot]).wait()


def _combine(pos, x1, h2, gate2, wts, wsgu, wsd, ys_flat, nb, tm):
    t, d = x1.shape
    ns = t // nb // tm
    row = pl.BlockSpec((tm, d), lambda b, i: (b * ns + i, 0))
    last = nb * ns - 1
    ys = ys_flat.reshape(-1, ROW_CHUNKS, LANES)
    return pl.pallas_call(
        functools.partial(_combine_kernel, tm=tm),
        grid=(nb, ns),
        in_specs=[pl.BlockSpec((TOP_K, tm), lambda b, i: (0, b * ns + i), memory_space=pltpu.SMEM),
                  pl.BlockSpec((TOP_K, tm), lambda b, i: (0, jnp.minimum(b * ns + i + 1, last)),
                               memory_space=pltpu.SMEM),
                  row, pl.BlockSpec((tm * ROW_CHUNKS, LANES), lambda b, i: (b * ns + i, 0)),
                  _mod_spec(gate2, tm, ns),
                  pl.BlockSpec((TOP_K, tm), lambda b, i: (0, b * ns + i)),
                  _const_spec(wsgu.shape), _const_spec(wsd.shape),
                  pl.BlockSpec(memory_space=pl.ANY), pl.BlockSpec(memory_space=pl.ANY)],
        out_specs=row,
        out_shape=jax.ShapeDtypeStruct((t, d), F32),
        scratch_shapes=[pltpu.VMEM((2, TOP_K, tm * ROW_CHUNKS, LANES), F32), pltpu.SemaphoreType.DMA((2,))],
        compiler_params=_cparams(("arbitrary", "arbitrary")), name="moe_combine",
    )(pos, pos, x1, h2, gate2, wts, wsgu, wsd, ys, ys_flat)


def _prepare_params(g_norm1, w_in, b_fox_f, g_q, g_k, hgrn_lb, g_hgrn_o, w_proj_a, w_proj_b, w_out,
                    g_norm2, w_router, b_router, w_sh_gate, w_sh_up, w_sh_down):
    d = D_MODEL
    fw, hw = FOX_WIDTH, HGRN_WIDTH
    c0 = 3 * fw
    c1 = c0 + FOX_HEADS
    c2 = c1 + 4 * hw
    wff = w_in[:, c0:c1]
    head = jnp.arange(fw) // FOX_HEAD_DIM
    wr_t = w_router.T
    wrh = wr_t.astype(BF16)
    return dict(
        g1=g_norm1.reshape(1, d),
        wqkv=w_in[:, :c0].astype(BF16),
        wff=jnp.pad(wff, ((0, 0), (0, LANES - FOX_HEADS))).astype(BF16),
        wfft=jnp.pad(wff.T, ((0, 16 - FOX_HEADS), (0, 0))).astype(BF16),
        bf=b_fox_f.reshape(1, FOX_HEADS), bft=b_fox_f.reshape(FOX_HEADS, 1),
        gq=jnp.tile(g_q, FOX_HEADS).reshape(1, fw), gk=jnp.tile(g_k, FOX_HEADS).reshape(1, fw),
        bd=(head[:, None] == head[None, :]).astype(BF16) * (1.0 / FOX_HEAD_DIM),
        lbp=hgrn_lb,
        wh=w_in[:, c1:c2].astype(BF16), wg=w_in[:, c2:].astype(BF16),
        gn=g_hgrn_o.reshape(1, HGRN_DK),
        g2=g_norm2.reshape(1, d),
        wpa=w_proj_a.astype(BF16), wpb=w_proj_b.astype(BF16), wo=w_out.astype(BF16),
        wrh=wrh, wrl=(wr_t - wrh.astype(F32)).astype(BF16), br=b_router.reshape(N_EXPERTS, 1),
        wsgu=jnp.concatenate([w_sh_gate, w_sh_up], axis=1).astype(BF16), wsd=w_sh_down.astype(BF16),
    )


def _mixers(x2, nb, seq, tm, shift1, scale1, prm, past, tile, tb):
    (q, k, kb, v, vb, lf, lft, hq, hl, hi, og, ga, gb) = _inproj(x2, nb if shift1.shape[1] == 1 else 1, tm,
                                                                 shift1, scale1, prm)
    t = nb * seq
    hp = FOX_HEADS // 2
    lft = lft.transpose(1, 0, 2).reshape(FOX_HEADS, nb, seq).transpose(1, 0, 2)
    if past is None:
        fcum = _cumsum_lanes(lft.reshape(nb * FOX_HEADS, seq)).reshape(nb, hp, 2, seq)
        ya = _fox_prompt(q, kb, vb, fcum, nb, seq, tile, min(32, tile))
        s0 = None
    else:
        cache_k, cache_v, cache_lf, s0 = past
        plen = cache_k.shape[1]
        ltot = plen + seq
        lpad = -(-ltot // LANES) * LANES
        lf_all = jnp.concatenate([cache_lf.transpose(0, 2, 1), lft,
                                  jnp.zeros((nb, FOX_HEADS, lpad - ltot), F32)], axis=-1)
        fcum = _cumsum_lanes(lf_all.reshape(nb * FOX_HEADS, lpad)).reshape(nb, hp, 2, lpad)
        ya = _fox_sample(q, kb, vb, cache_k.reshape(nb, plen, FOX_WIDTH), cache_v.reshape(nb, plen, FOX_WIDTH),
                         fcum, nb, seq, plen)
    yb, sfin = _hgrn(hq, hl, hi, og, prm["gn"], s0, nb, seq, tb)
    return k, v, lf, sfin, ya, yb, ga, gb


def kernel(x_prompt, x_sample, cache_fox_k, cache_fox_v, cache_fox_logf, state_hgrn, c_prompt, c_sample,
           w_ada, b_ada, g_norm1, w_in, b_fox_f, g_q, g_k, hgrn_lb, g_hgrn_o, w_proj_a, w_proj_b, w_out,
           g_norm2, w_router, b_router, w_exp_gate, w_exp_up, w_exp_down, w_sh_gate, w_sh_up, w_sh_down):
    assert w_ada.shape[0] == 1 and hgrn_lb.shape[0] == 2, "single-layer trunk"
    d = D_MODEL
    bp, sp, _ = x_prompt.shape
    bs, ss, _ = x_sample.shape
    tp, ts = bp * sp, bs * ss
    prm = _prepare_params(g_norm1[0], w_in[0], b_fox_f[0], g_q[0], g_k[0], hgrn_lb, g_hgrn_o[0],
                          w_proj_a[0], w_proj_b[0], w_out[0], g_norm2[0], w_router[0], b_router[0],
                          w_sh_gate[0], w_sh_up[0], w_sh_down[0])

    bc = bp + bs
    bc_pad = -(-bc // 8) * 8
    c_all = jnp.concatenate([c_prompt, c_sample, jnp.zeros((bc_pad - bc, d), F32)], axis=0)
    mod = _ada(c_all, w_ada[0], b_ada[0])
    mod_p = [mod[:bp, j * d:(j + 1) * d].reshape(bp, 1, d) for j in range(6)]
    mod_s = [jnp.repeat(mod[bp:bc, j * d:(j + 1) * d], ss, axis=0).reshape(1, ts, d) for j in range(6)]

    tm_p = min(512, sp)
    tm_s = min(256, ts)
    xp2 = x_prompt.reshape(tp, d)
    xs2 = x_sample.reshape(ts, d)
    tile = min(1024, sp)
    tb = min(256, sp)

    kp, vp, lfp, sfin_p, ya_p, yb_p, ga_p, gb_p = _mixers(
        xp2, bp, sp, tm_p, mod_p[0], mod_p[1], prm, None, tile, tb)
    past = (cache_fox_k[0], cache_fox_v[0], cache_fox_logf[0], state_hgrn[0])
    ks, vs, lfs, sfin_s, ya_s, yb_s, ga_s, gb_s = _mixers(
        xs2, bs, ss, tm_s, mod_s[0], mod_s[1], prm, past, None, min(256, ss))

    zero_cnt = jnp.zeros((N_EXPERTS, 1), F32)
    x1_p, h2_p, idx_p, wt_p, rank_p, cnt_p = _post(xp2, ya_p, yb_p, ga_p, gb_p, mod_p[2], mod_p[3], mod_p[4],
                                                    prm, zero_cnt, bp, tm_p)
    x1_s, h2_s, idx_s, wt_s, rank_s, cnt_all = _post(xs2, ya_s, yb_s, ga_s, gb_s, mod_s[2], mod_s[3], mod_s[4],
                                                     prm, cnt_p, 1, tm_s)

    counts = cnt_all.reshape(N_EXPERTS).astype(I32)
    padded = (counts + MOE_BLK - 1) // MOE_BLK * MOE_BLK
    pend = jnp.cumsum(padded)
    pstart = pend - padded
    nblk = -(-((tp + ts) * TOP_K + N_EXPERTS * (MOE_BLK - 1)) // MOE_BLK)
    n_used = (pend[-1] // MOE_BLK).reshape(1)
    blk_row0 = jnp.arange(nblk, dtype=I32) * MOE_BLK
    blk_expert = jnp.minimum(jnp.sum((pend[None, :] <= blk_row0[:, None]).astype(I32), axis=1), N_EXPERTS - 1)
    blk_used = jnp.arange(nblk) < n_used[0]
    blk_expert = jnp.where(blk_used, blk_expert, blk_expert[jnp.maximum(n_used[0] - 1, 0)])
    blk_first = blk_used & jnp.concatenate([jnp.ones((1,), bool), blk_expert[1:] != blk_expert[:-1]])
    blk_slot = ((jnp.cumsum(blk_first.astype(I32)) - 1) & 1).astype(I32)
    eids = jnp.arange(N_EXPERTS, dtype=I32)
    later = (eids[None, :] > eids[:, None]) & (padded[None, :] > 0)
    nxt_expert = jnp.min(jnp.where(later, eids[None, :], N_EXPERTS), axis=1)
    nxt_expert = jnp.where(nxt_expert < N_EXPERTS, nxt_expert, -1).astype(I32)
    pstart_col = pstart.astype(F32).reshape(N_EXPERTS, 1)
    pos_p = _pos(idx_p, rank_p, pstart_col, tm_p)
    pos_s = _pos(idx_s, rank_s, pstart_col, tm_s)

    xs_pool = _padfill(counts, padded, pstart, nblk * MOE_BLK, d)
    xs_pool = _dispatch(pos_p, h2_p, xs_pool, min(128, tm_p))
    xs_pool = _dispatch(pos_s, h2_s, xs_pool, min(128, tm_s))
    ys_pool = _moe(blk_expert, n_used, blk_first.astype(I32), blk_slot, nxt_expert, xs_pool,
                   w_exp_gate[0], w_exp_up[0], w_exp_down[0])

    tm_c = min(128, sp)
    y_p = _combine(pos_p, x1_p, h2_p, mod_p[5], wt_p, prm["wsgu"], prm["wsd"], ys_pool, bp, tm_c)
    y_s = _combine(pos_s, x1_s, h2_s, mod_s[5], wt_s, prm["wsgu"], prm["wsd"], ys_pool, 1, min(128, ts))

    return (y_p.reshape(bp, sp, d), y_s.reshape(bs, ss, d),
            kp.reshape(1, bp, sp, FOX_HEADS, FOX_HEAD_DIM), vp.reshape(1, bp, sp, FOX_HEADS, FOX_HEAD_DIM),
            lfp.reshape(1, bp, sp, FOX_HEADS), sfin_p[None],
            ks.reshape(1, bs, ss, FOX_HEADS, FOX_HEAD_DIM), vs.reshape(1, bs, ss, FOX_HEADS, FOX_HEAD_DIM),
            lfs.reshape(1, bs, ss, FOX_HEADS), sfin_s[None])
```

```python
import functools

import jax
import jax.numpy as jnp
from jax import lax
from jax.experimental import pallas as pl
from jax.experimental.pallas import tpu as pltpu

F32 = jnp.float32
BF16 = jnp.bfloat16
I32 = jnp.int32

D_MODEL = 1024
FOX_HEADS = 8
FOX_HEAD_DIM = 64
FOX_WIDTH = FOX_HEADS * FOX_HEAD_DIM
HGRN_HEADS = 4
HGRN_DK = 128
HGRN_WIDTH = HGRN_HEADS * HGRN_DK
N_EXPERTS = 256
TOP_K = 8
N_GROUPS = 8
TOPK_GROUPS = 4
GROUP_SIZE = N_EXPERTS // N_GROUPS
D_EXPERT = 256
D_SHARED = 256
ROUTED_SCALE = 2.5
NORM_EPS = 1e-6
NEG_BIG = -1e30
LOG2E = 1.4426950408889634
QK_SCALE = FOX_HEAD_DIM ** -0.5 * LOG2E

LANES = 128
SUB = 16
MOE_BLK = 512
VMEM_LIMIT = 56 * 1024 * 1024

NT_DIMS = (((1,), (1,)), ((), ()))
TN_DIMS = (((0,), (0,)), ((), ()))


def _cparams(sem):
    return pltpu.CompilerParams(dimension_semantics=sem, vmem_limit_bytes=VMEM_LIMIT)


def _const_spec(shape):
    nd = len(shape)
    return pl.BlockSpec(shape, lambda *_: (0,) * nd, pipeline_mode=pl.Buffered(1))


def _sigmoid(z):
    return 1.0 / (1.0 + jnp.exp(-z))


def _log_sigmoid(z):
    return jnp.minimum(z, 0.0) - jnp.log(1.0 + jnp.exp(-jnp.abs(z)))


ROW_CHUNKS = D_MODEL // LANES


def _row_chunk(ref, c):
    return ref.at[pl.ds(c, ref.shape[0] // ROW_CHUNKS, stride=ROW_CHUNKS), :]


def _store_rows(ref, val):
    for c in range(ROW_CHUNKS):
        _row_chunk(ref, c)[...] = val[:, c * LANES:(c + 1) * LANES]


def _load_rows(ref):
    return jnp.concatenate([_row_chunk(ref, c)[...] for c in range(ROW_CHUNKS)], axis=1)


def _split3(a):
    hi = a.astype(BF16)
    r1 = a - hi.astype(F32)
    mid = r1.astype(BF16)
    lo = (r1 - mid.astype(F32)).astype(BF16)
    return hi, mid, lo


def _ada_kernel(c_ref, w_ref, b_ref, o_ref):
    c = c_ref[...]
    s = c * _sigmoid(c)
    o_ref[...] = jnp.dot(s.astype(BF16), w_ref[...].astype(BF16), preferred_element_type=F32) + b_ref[...]


def _ada(c_all, w_ada, b_ada):
    bc, d = c_all.shape
    n = w_ada.shape[1]
    tn = 1024
    return pl.pallas_call(
        _ada_kernel,
        grid=(n // tn,),
        in_specs=[pl.BlockSpec((bc, d), lambda j: (0, 0)),
                  pl.BlockSpec((d, tn), lambda j: (0, j)),
                  pl.BlockSpec((1, tn), lambda j: (0, j))],
        out_specs=pl.BlockSpec((bc, tn), lambda j: (0, j)),
        out_shape=jax.ShapeDtypeStruct((bc, n), F32),
        compiler_params=_cparams(("parallel",)),
        name="ada",
    )(c_all, w_ada, b_ada.reshape(1, n))


def _inproj_kernel(x_ref, sh_ref, sc_ref, g1_ref, wqkv_ref, wff_ref, wfft_ref, bf_ref, bft_ref,
                   gq_ref, gk_ref, bd_ref, lbp_ref, wh_ref, wg_ref,
                   q_ref, k_ref, kb_ref, v_ref, vb_ref, lf_ref, lft_ref,
                   hq_ref, hl_ref, hi_ref, og_ref, ga_ref, gb_ref):
    x = x_ref[...]
    ms = jnp.mean(x * x, axis=-1, keepdims=True)
    h = x * lax.rsqrt(ms + NORM_EPS) * g1_ref[...]
    h = h * (1.0 + sc_ref[0]) + sh_ref[0]
    hb = h.astype(BF16)

    def headnorm(a, g):
        ss = jnp.dot((a * a).astype(BF16), bd_ref[...], preferred_element_type=F32)
        return a * lax.rsqrt(ss + NORM_EPS) * g

    fq = jnp.dot(hb, wqkv_ref[:, 0:FOX_WIDTH], preferred_element_type=F32)
    q_ref[...] = (headnorm(fq, gq_ref[...]) * QK_SCALE).astype(BF16)
    fk = jnp.dot(hb, wqkv_ref[:, FOX_WIDTH:2 * FOX_WIDTH], preferred_element_type=F32)
    k = headnorm(fk, gk_ref[...])
    kb_ref[...] = k.astype(BF16)
    fv = jnp.dot(hb, wqkv_ref[:, 2 * FOX_WIDTH:3 * FOX_WIDTH], preferred_element_type=F32)
    vb_ref[...] = fv.astype(BF16)
    k_ref[...] = k.reshape(k_ref.shape)
    v_ref[...] = fv.reshape(v_ref.shape)

    ff = jnp.dot(hb, wff_ref[...], preferred_element_type=F32)
    lf_ref[...] = _log_sigmoid(ff[:, 0:FOX_HEADS] + bf_ref[...])
    fft = lax.dot_general(wfft_ref[...], hb, NT_DIMS, preferred_element_type=F32)
    lft_ref[0] = _log_sigmoid(fft[0:FOX_HEADS, :] + bft_ref[...])

    lbp = lbp_ref[...]
    e = jnp.exp(lbp - jnp.max(lbp, axis=0, keepdims=True))
    lb = e[0:1, :] / jnp.sum(e, axis=0, keepdims=True)

    w = HGRN_WIDTH
    hq = jnp.dot(hb, wh_ref[:, 0:w], preferred_element_type=F32)
    hq_ref[...] = hq * _sigmoid(hq)
    hf = jnp.dot(hb, wh_ref[:, w:2 * w], preferred_element_type=F32)
    hl_ref[...] = jnp.log(lb + (1.0 - lb) * _sigmoid(hf))
    hi_ref[...] = jnp.dot(hb, wh_ref[:, 2 * w:3 * w], preferred_element_type=F32)
    hg = jnp.dot(hb, wh_ref[:, 3 * w:4 * w], preferred_element_type=F32)
    og_ref[...] = _sigmoid(hg)
    ga = jnp.dot(hb, wg_ref[:, 0:D_MODEL], preferred_element_type=F32)
    ga_ref[...] = _sigmoid(ga).astype(BF16)
    gb = jnp.dot(hb, wg_ref[:, D_MODEL:2 * D_MODEL], preferred_element_type=F32)
    gb_ref[...] = _sigmoid(gb).astype(BF16)


def _mod_spec(arr, tm, ns):
    d = arr.shape[-1]
    if arr.shape[1] == 1:
        return pl.BlockSpec((1, 1, d), lambda b, i: (b, 0, 0))
    return pl.BlockSpec((1, tm, d), lambda b, i: (0, b * ns + i, 0))


def _inproj(x2, nb, tm, shift1, scale1, prm):
    t, d = x2.shape
    ns = t // nb // tm
    row = lambda w: pl.BlockSpec((tm, w), lambda b, i: (b * ns + i, 0))
    in_specs = [row(d), _mod_spec(shift1, tm, ns), _mod_spec(scale1, tm, ns),
                _const_spec((1, d)), _const_spec(prm["wqkv"].shape), _const_spec(prm["wff"].shape),
                _const_spec(prm["wfft"].shape), _const_spec((1, FOX_HEADS)), _const_spec((FOX_HEADS, 1)),
                _const_spec((1, FOX_WIDTH)), _const_spec((1, FOX_WIDTH)), _const_spec((FOX_WIDTH, FOX_WIDTH)),
                _const_spec(prm["lbp"].shape), _const_spec(prm["wh"].shape), _const_spec(prm["wg"].shape)]
    fw, hw = FOX_WIDTH, HGRN_WIDTH
    heads = pl.BlockSpec((tm, FOX_HEADS, FOX_HEAD_DIM), lambda b, i: (b * ns + i, 0, 0))
    out_shape = [jax.ShapeDtypeStruct((t, fw), BF16),
                 jax.ShapeDtypeStruct((t, FOX_HEADS, FOX_HEAD_DIM), F32),
                 jax.ShapeDtypeStruct((t, fw), BF16),
                 jax.ShapeDtypeStruct((t, FOX_HEADS, FOX_HEAD_DIM), F32),
                 jax.ShapeDtypeStruct((t, fw), BF16),
                 jax.ShapeDtypeStruct((t, FOX_HEADS), F32),
                 jax.ShapeDtypeStruct((nb * ns, FOX_HEADS, tm), F32),
                 jax.ShapeDtypeStruct((t, hw), F32),
                 jax.ShapeDtypeStruct((t, hw), F32),
                 jax.ShapeDtypeStruct((t, hw), F32),
                 jax.ShapeDtypeStruct((t, hw), F32),
                 jax.ShapeDtypeStruct((t, d), BF16),
                 jax.ShapeDtypeStruct((t, d), BF16)]
    out_specs = [row(fw), heads, row(fw), heads, row(fw), row(FOX_HEADS),
                 pl.BlockSpec((1, FOX_HEADS, tm), lambda b, i: (b * ns + i, 0, 0)),
                 row(hw), row(hw), row(hw), row(hw), row(d), row(d)]
    return pl.pallas_call(
        _inproj_kernel, grid=(nb, ns), in_specs=in_specs, out_specs=out_specs, out_shape=out_shape,
        compiler_params=_cparams(("parallel", "parallel")), name="inproj",
    )(x2, shift1, scale1, prm["g1"], prm["wqkv"], prm["wff"], prm["wfft"], prm["bf"], prm["bft"],
      prm["gq"], prm["gk"], prm["bd"], prm["lbp"], prm["wh"], prm["wg"])


def _cumsum_kernel(x_ref, o_ref):
    x = x_ref[...]
    n = x.shape[-1]
    lane = lax.broadcasted_iota(I32, x.shape, 1)
    s = 1
    while s < n:
        x = x + jnp.where(lane >= s, pltpu.roll(x, s, axis=1), 0.0)
        s *= 2
    o_ref[...] = x * LOG2E


def _cumsum_lanes(x):
    return pl.pallas_call(
        _cumsum_kernel, out_shape=jax.ShapeDtypeStruct(x.shape, F32),
        compiler_params=pltpu.CompilerParams(vmem_limit_bytes=VMEM_LIMIT), name="cumsum",
    )(x)


FOX_PAIRS_PER_STEP = 2


def _fox_prompt_kernel(qt_ref, kt_ref, q_ref, k_ref, v_ref, fk_ref, o_ref, *scratch, tile, rsub):
    step = pl.program_id(2)
    qi = qt_ref[step]
    ki = kt_ref[step]
    nheads = 2 * FOX_PAIRS_PER_STEP
    heads = [scratch[4 * h:4 * h + 4] for h in range(nheads)]
    lane = lax.broadcasted_iota(I32, (1, LANES), 1)
    lo_half = lane < FOX_HEAD_DIM

    @pl.when(ki == 0)
    def _init():
        for m_sc, a_sc, _, _ in heads:
            m_sc[...] = jnp.full(m_sc.shape, NEG_BIG, F32)
            a_sc[...] = jnp.zeros(a_sc.shape, F32)

    def tile_update(diagonal):
        nsub = tile // rsub
        if diagonal:
            ahead = (lax.broadcasted_iota(I32, (rsub, tile), 1)
                     - lax.broadcasted_iota(I32, (rsub, tile), 0))
        for h, (_, _, s_sc, _) in enumerate(heads):
            lanes = slice((h // 2) * LANES, (h // 2 + 1) * LANES)
            q = q_ref[:, lanes]
            sel = lo_half if h % 2 == 0 else jnp.logical_not(lo_half)
            qh = jnp.where(sel, q, jnp.zeros_like(q))
            s_sc[...] = lax.dot_general(qh, k_ref[:, lanes], NT_DIMS, preferred_element_type=F32)
        for h, (m_sc, a_sc, s_sc, p_sc) in enumerate(heads):
            lanes = slice((h // 2) * LANES, (h // 2 + 1) * LANES)
            v = v_ref[:, lanes]
            fk = fk_ref[0, h // 2]
            hs = h % 2
            sel = lo_half if hs == 0 else jnp.logical_not(lo_half)
            v_ones = jnp.where(sel, v, jnp.ones_like(v))

            def biased(r):
                rows = slice(r * rsub, (r + 1) * rsub)
                cw = min(tile, -(-((r + 1) * rsub) // LANES) * LANES) if diagonal else tile
                s = s_sc[rows, 0:cw] - fk[hs:hs + 1, 0:cw]
                if diagonal:
                    s = jnp.where(ahead[:, 0:cw] <= r * rsub, s, NEG_BIG)
                return s, rows, cw

            m_old = m_sc[...]
            m_parts = []
            for r in range(nsub):
                s, rows, _ = biased(r)
                m_parts.append(jnp.maximum(m_old[rows], jnp.max(s, axis=1, keepdims=True)))
            for r in range(nsub):
                s, rows, cw = biased(r)
                nc = cw // LANES
                p = jnp.exp2(s - jnp.concatenate([m_parts[r]] * nc, axis=1))
                p_sc[rows, 0:cw] = p.astype(BF16)
                if cw < tile:
                    p_sc[rows, cw:tile] = jnp.zeros((rsub, tile - cw), BF16)
            m_new = jnp.concatenate(m_parts, axis=0)
            alpha = jnp.exp2(m_old - m_new)
            m_sc[...] = m_new
            a_sc[...] = alpha * a_sc[...] + jnp.dot(p_sc[...], v_ones, preferred_element_type=F32)

    @pl.when(ki < qi)
    def _full():
        tile_update(False)

    @pl.when(ki == qi)
    def _diag():
        tile_update(True)
        for pair in range(FOX_PAIRS_PER_STEP):
            acc0 = heads[2 * pair][1][...]
            acc1 = heads[2 * pair + 1][1][...]
            o0 = acc0 / pltpu.roll(acc0, FOX_HEAD_DIM, axis=1)
            o1 = acc1 / pltpu.roll(acc1, FOX_HEAD_DIM, axis=1)
            o_ref[:, pair * LANES:(pair + 1) * LANES] = jnp.where(lo_half, o0, o1).astype(o_ref.dtype)


def _fox_prompt(q, kb, vb, fcum, nb, seq, tile, rsub):
    nt = seq // tile
    pps = FOX_PAIRS_PER_STEP
    hp = FOX_HEADS // 2 // pps
    width = pps * LANES
    qt = jnp.asarray([qi for qi in range(nt) for _ in range(qi + 1)], I32)
    kt = jnp.asarray([ki for qi in range(nt) for ki in range(qi + 1)], I32)
    qmap = lambda b, h, s, qt, kt: (b * nt + qt[s], h)
    kmap = lambda b, h, s, qt, kt: (b * nt + kt[s], h)
    fmap = lambda b, h, s, qt, kt: (b, h, 0, kt[s])
    per_head = [pltpu.VMEM((tile, LANES), F32), pltpu.VMEM((tile, LANES), F32),
                pltpu.VMEM((tile, tile), F32), pltpu.VMEM((tile, tile), BF16)]
    return pl.pallas_call(
        functools.partial(_fox_prompt_kernel, tile=tile, rsub=rsub),
        grid_spec=pltpu.PrefetchScalarGridSpec(
            num_scalar_prefetch=2, grid=(nb, hp, nt * (nt + 1) // 2),
            in_specs=[pl.BlockSpec((tile, width), qmap), pl.BlockSpec((tile, width), kmap),
                      pl.BlockSpec((tile, width), kmap), pl.BlockSpec((1, pps, 2, tile), fmap)],
            out_specs=pl.BlockSpec((tile, width), qmap),
            scratch_shapes=per_head * (2 * pps)),
        out_shape=jax.ShapeDtypeStruct((nb * seq, FOX_WIDTH), BF16),
        compiler_params=_cparams(("parallel", "parallel", "arbitrary")),
        name="fox_prompt",
    )(qt, kt, q, kb, vb, fcum)


def _fox_sample_kernel(q_ref, kn_ref, vn_ref, kp_ref, vp_ref, fk_ref, o_ref, *, past, seq):
    lane = lax.broadcasted_iota(I32, (1, LANES), 1)
    lo_half = lane < FOX_HEAD_DIM
    q = q_ref[...]
    kn = kn_ref[...]
    vn = vn_ref[...]
    kp = kp_ref[0].astype(BF16)
    vp = vp_ref[0].astype(BF16)
    fk = fk_ref[0, 0]
    row = lax.broadcasted_iota(I32, (seq, seq), 0)
    col = lax.broadcasted_iota(I32, (seq, seq), 1)
    causal = col <= row
    outs = []
    for hs in range(2):
        sel = lo_half if hs == 0 else jnp.logical_not(lo_half)
        qh = jnp.where(sel, q, jnp.zeros_like(q))
        sp = lax.dot_general(qh, kp, NT_DIMS, preferred_element_type=F32) - fk[hs:hs + 1, 0:past]
        sn = lax.dot_general(qh, kn, NT_DIMS, preferred_element_type=F32) - fk[hs:hs + 1, past:past + seq]
        sn = jnp.where(causal, sn, NEG_BIG)
        m = jnp.maximum(jnp.max(sp, axis=1, keepdims=True), jnp.max(sn, axis=1, keepdims=True))
        pp = jnp.exp2(sp - m)
        pn = jnp.exp2(sn - m)
        l = jnp.sum(pp, axis=1, keepdims=True) + jnp.sum(pn, axis=1, keepdims=True)
        o = (jnp.dot(pp.astype(BF16), vp, preferred_element_type=F32)
             + jnp.dot(pn.astype(BF16), vn, preferred_element_type=F32))
        outs.append(o / l)
    o_ref[...] = jnp.where(lo_half, outs[0], outs[1]).astype(o_ref.dtype)


def _fox_sample(q, kb, vb, cache_k, cache_v, fcum, nb, seq, past):
    hp = FOX_HEADS // 2
    lpad = fcum.shape[-1]
    rmap = lambda b, h: (b, h)
    cmap = lambda b, h: (b, 0, h)
    return pl.pallas_call(
        functools.partial(_fox_sample_kernel, past=past, seq=seq),
        grid=(nb, hp),
        in_specs=[pl.BlockSpec((seq, LANES), rmap), pl.BlockSpec((seq, LANES), rmap),
                  pl.BlockSpec((seq, LANES), rmap),
                  pl.BlockSpec((1, past, LANES), cmap), pl.BlockSpec((1, past, LANES), cmap),
                  pl.BlockSpec((1, 1, 2, lpad), lambda b, h: (b, h, 0, 0))],
        out_specs=pl.BlockSpec((seq, LANES), rmap),
        out_shape=jax.ShapeDtypeStruct((nb * seq, FOX_WIDTH), BF16),
        compiler_params=_cparams(("parallel", "parallel")),
        name="fox_sample",
    )(q, kb, vb, cache_k, cache_v, fcum)


def _hgrn_kernel(*refs, tb, has_state):
    if has_state:
        q_ref, g_ref, i_ref, og_ref, gn_ref, tri_ref, s0_ref, y_ref, sfin_ref, *st_sc, b_sc, o_sc = refs
    else:
        q_ref, g_ref, i_ref, og_ref, gn_ref, tri_ref, y_ref, sfin_ref, *st_sc, b_sc, o_sc = refs
        s0_ref = None
    step = pl.program_id(1)
    nstep = pl.num_programs(1)

    @pl.when(step == 0)
    def _init():
        for h in range(HGRN_HEADS):
            if has_state:
                st_sc[h][...] = s0_ref[0, h].T
            else:
                st_sc[h][...] = jnp.zeros((HGRN_DK, HGRN_DK), F32)

    g = g_ref[...]
    tri = tri_ref[...]
    b = None
    for part in _split3(g):
        pb = jnp.dot(tri, part, preferred_element_type=F32)
        b = pb if b is None else b + pb
    b_sc[...] = b

    trow = lax.broadcasted_iota(I32, (SUB, 1), 0)

    def sub_chunk(c, carry):
        r0 = pl.multiple_of(c * SUB, SUB)
        for h in range(HGRN_HEADS):
            cs = slice(h * HGRN_DK, (h + 1) * HGRN_DK)
            q = q_ref[pl.ds(r0, SUB), cs]
            gg = g_ref[pl.ds(r0, SUB), cs]
            iv = i_ref[pl.ds(r0, SUB), cs]
            bb = b_sc[pl.ds(r0, SUB), cs]
            kk = 1.0 - jnp.exp(gg)
            st = st_sc[h][...]
            o = lax.dot_general((q * jnp.exp(bb)).astype(BF16), st.astype(BF16), NT_DIMS,
                                preferred_element_type=F32)
            for s in range(SUB):
                lo = 0 if s < 8 else 8
                ks = kk[s:s + 1, :]
                bs = bb[s:s + 1, :]
                ivs = iv[s:s + 1, :]
                e = jnp.exp(jnp.where(trow[lo:] >= s, bb[lo:] - bs, NEG_BIG))
                a = jnp.sum(q[lo:] * ks * e, axis=-1, keepdims=True)
                upd = a * ivs
                if lo:
                    upd = jnp.concatenate([jnp.zeros((8, HGRN_DK), F32), upd], axis=0)
                o = o + upd
            o_sc[pl.ds(r0, SUB), cs] = o
            bl = bb[SUB - 1:SUB, :]
            kd = kk * jnp.exp(bl - bb)
            u = lax.dot_general(iv.astype(BF16), kd.astype(BF16), TN_DIMS, preferred_element_type=F32)
            st_sc[h][...] = st * jnp.exp(bl) + u
        return carry

    lax.fori_loop(0, tb // SUB, sub_chunk, 0, unroll=min(4, tb // SUB))

    for h in range(HGRN_HEADS):
        cs = slice(h * HGRN_DK, (h + 1) * HGRN_DK)
        o = o_sc[:, cs]
        ms = jnp.mean(o * o, axis=-1, keepdims=True)
        y_ref[:, cs] = (o * lax.rsqrt(ms + NORM_EPS) * gn_ref[...] * og_ref[:, cs]).astype(y_ref.dtype)

    @pl.when(step == nstep - 1)
    def _fin():
        for h in range(HGRN_HEADS):
            sfin_ref[0, h] = st_sc[h][...].T


def _hgrn(qs, gl, iv, og, gn, s0, nb, seq, tb):
    ns = seq // tb
    t = nb * seq
    row = pl.BlockSpec((tb, HGRN_WIDTH), lambda b, i: (b * ns + i, 0))
    r = jnp.arange(tb)
    tri = ((r[:, None] // SUB == r[None, :] // SUB) & (r[None, :] <= r[:, None])).astype(BF16)
    in_specs = [row, row, row, row, _const_spec((1, HGRN_DK)), _const_spec((tb, tb))]
    args = [qs, gl, iv, og, gn, tri]
    if s0 is not None:
        in_specs.append(pl.BlockSpec((1, HGRN_HEADS, HGRN_DK, HGRN_DK), lambda b, i: (b, 0, 0, 0)))
        args.append(s0)
    return pl.pallas_call(
        functools.partial(_hgrn_kernel, tb=tb, has_state=s0 is not None),
        grid=(nb, ns), in_specs=in_specs,
        out_specs=[row, pl.BlockSpec((1, HGRN_HEADS, HGRN_DK, HGRN_DK), lambda b, i: (b, 0, 0, 0))],
        out_shape=[jax.ShapeDtypeStruct((t, HGRN_WIDTH), BF16),
                   jax.ShapeDtypeStruct((nb, HGRN_HEADS, HGRN_DK, HGRN_DK), F32)],
        scratch_shapes=[pltpu.VMEM((HGRN_DK, HGRN_DK), F32)] * HGRN_HEADS
                       + [pltpu.VMEM((tb, HGRN_WIDTH), F32), pltpu.VMEM((tb, HGRN_WIDTH), F32)],
        compiler_params=_cparams(("parallel", "arbitrary")), name="hgrn",
    )(*args)


def _post_kernel(x_ref, ya_ref, yb_ref, ga_ref, gb_ref, g1_ref, sh2_ref, sc2_ref, gn2_ref,
                 wpa_ref, wpb_ref, wo_ref, wrh_ref, wrl_ref, br_ref, tri_ref, cin_ref,
                 x1_ref, h2_ref, idx_ref, wt_ref, rank_ref, cout_ref, cnt_sc, *, tm):
    first = jnp.logical_and(pl.program_id(0) == 0, pl.program_id(1) == 0)

    @pl.when(first)
    def _init():
        cnt_sc[...] = cin_ref[...]

    merged = (ga_ref[...].astype(F32) * jnp.dot(ya_ref[...], wpa_ref[...], preferred_element_type=F32)
              + gb_ref[...].astype(F32) * jnp.dot(yb_ref[...], wpb_ref[...], preferred_element_type=F32))
    x1 = x_ref[...] + g1_ref[0] * jnp.dot(merged.astype(BF16), wo_ref[...], preferred_element_type=F32)
    x1_ref[...] = x1
    ms = jnp.mean(x1 * x1, axis=-1, keepdims=True)
    h2 = x1 * lax.rsqrt(ms + NORM_EPS) * gn2_ref[...]
    h2 = h2 * (1.0 + sc2_ref[0]) + sh2_ref[0]
    _store_rows(h2_ref, h2)

    hh = h2.astype(BF16)
    hl = (h2 - hh.astype(F32)).astype(BF16)
    wrh = wrh_ref[...]
    logits = (lax.dot_general(wrh, hh, NT_DIMS, preferred_element_type=F32)
              + lax.dot_general(wrh, hl, NT_DIMS, preferred_element_type=F32)
              + lax.dot_general(wrl_ref[...], hh, NT_DIMS, preferred_element_type=F32))
    scores = _sigmoid(logits)
    biased = scores + br_ref[...]

    b3 = biased.reshape(N_GROUPS, GROUP_SIZE, tm)
    it3 = lax.broadcasted_iota(I32, (N_GROUPS, GROUP_SIZE, tm), 1).astype(F32)
    m1 = jnp.max(b3, axis=1, keepdims=True)
    i1 = jnp.min(jnp.where(b3 == m1, it3, float(GROUP_SIZE)), axis=1, keepdims=True)
    m2 = jnp.max(jnp.where(it3 == i1, -jnp.inf, b3), axis=1, keepdims=True)
    gs = (m1 + m2).reshape(N_GROUPS, tm)

    gi = lax.broadcasted_iota(I32, (N_GROUPS, tm), 0)
    beat = jnp.zeros((N_GROUPS, tm), F32)
    for g in range(N_GROUPS):
        r = gs[g:g + 1, :]
        beat = beat + jnp.where((r > gs) | ((r == gs) & (g < gi)), 1.0, 0.0)
    gpen = jnp.where(beat < TOPK_GROUPS, 0.0, -jnp.inf)
    masked = (b3 + gpen.reshape(N_GROUPS, 1, tm)).reshape(N_EXPERTS, tm)

    ei = lax.broadcasted_iota(I32, (N_EXPERTS, tm), 0).astype(F32)
    idx_rows, w_rows = [], []
    chosen = jnp.zeros((N_EXPERTS, tm), F32)
    for _ in range(TOP_K):
        m = jnp.max(masked, axis=0, keepdims=True)
        ik = jnp.min(jnp.where(masked == m, ei, float(N_EXPERTS)), axis=0, keepdims=True)
        hit = ei == ik
        w_rows.append(jnp.sum(jnp.where(hit, scores, 0.0), axis=0, keepdims=True))
        idx_rows.append(ik)
        masked = jnp.where(hit, -jnp.inf, masked)
        chosen = jnp.where(hit, 1.0, chosen)
    idx_ref[...] = jnp.concatenate(idx_rows, axis=0).astype(I32)
    wts = jnp.concatenate(w_rows, axis=0)
    wt_ref[...] = wts / jnp.sum(wts, axis=0, keepdims=True) * ROUTED_SCALE

    before = cnt_sc[...] + jnp.dot(chosen.astype(BF16), tri_ref[...], preferred_element_type=F32)
    rank_rows = [jnp.sum(jnp.where(ei == idx_rows[kk], before, 0.0), axis=0, keepdims=True)
                 for kk in range(TOP_K)]
    rank_ref[...] = jnp.concatenate(rank_rows, axis=0).astype(I32)
    cnt_sc[...] = cnt_sc[...] + jnp.sum(chosen, axis=1, keepdims=True)
    cout_ref[...] = cnt_sc[...]


def _post(x2, ya, yb, ga, gb, gate1, shift2, scale2, prm, count_in, nb, tm):
    t, d = x2.shape
    ns = t // nb // tm
    row = lambda w: pl.BlockSpec((tm, w), lambda b, i: (b * ns + i, 0))
    colb = pl.BlockSpec((TOP_K, tm), lambda b, i: (0, b * ns + i))
    r = jnp.arange(tm)
    tri = (r[:, None] < r[None, :]).astype(BF16)
    in_specs = [row(d), row(FOX_WIDTH), row(HGRN_WIDTH), row(d), row(d),
                _mod_spec(gate1, tm, ns), _mod_spec(shift2, tm, ns), _mod_spec(scale2, tm, ns),
                _const_spec((1, d)), _const_spec((FOX_WIDTH, d)), _const_spec((HGRN_WIDTH, d)),
                _const_spec((d, d)), _const_spec((N_EXPERTS, d)), _const_spec((N_EXPERTS, d)),
                _const_spec((N_EXPERTS, 1)), _const_spec((tm, tm)), _const_spec((N_EXPERTS, 1))]
    out_shape = [jax.ShapeDtypeStruct((t, d), F32), jax.ShapeDtypeStruct((t * ROW_CHUNKS, LANES), F32),
                 jax.ShapeDtypeStruct((TOP_K, t), I32), jax.ShapeDtypeStruct((TOP_K, t), F32),
                 jax.ShapeDtypeStruct((TOP_K, t), I32), jax.ShapeDtypeStruct((N_EXPERTS, 1), F32)]
    out_specs = [row(d), pl.BlockSpec((tm * ROW_CHUNKS, LANES), lambda b, i: (b * ns + i, 0)),
                 colb, colb, colb, pl.BlockSpec((N_EXPERTS, 1), lambda b, i: (0, 0))]
    return pl.pallas_call(
        functools.partial(_post_kernel, tm=tm), grid=(nb, ns), in_specs=in_specs,
        out_specs=out_specs, out_shape=out_shape,
        scratch_shapes=[pltpu.VMEM((N_EXPERTS, 1), F32)],
        compiler_params=_cparams(("arbitrary", "arbitrary")), name="post",
    )(x2, ya, yb, ga, gb, gate1, shift2, scale2, prm["g2"], prm["wpa"], prm["wpb"], prm["wo"],
      prm["wrh"], prm["wrl"], prm["br"], tri, count_in)


def _pos_kernel(idx_ref, rank_ref, pst_ref, pos_ref, *, tm):
    ei = lax.broadcasted_iota(I32, (N_EXPERTS, tm), 0)
    pst = pst_ref[...]
    rows = [jnp.sum(jnp.where(ei == idx_ref[kk:kk + 1, :], pst, 0.0), axis=0, keepdims=True)
            for kk in range(TOP_K)]
    pos_ref[...] = jnp.concatenate(rows, axis=0).astype(I32) + rank_ref[...]


def _pos(idx, rank, pstart_col, tm):
    t = idx.shape[1]
    colb = pl.BlockSpec((TOP_K, tm), lambda i: (0, i))
    return pl.pallas_call(
        functools.partial(_pos_kernel, tm=tm), grid=(t // tm,),
        in_specs=[colb, colb, _const_spec((N_EXPERTS, 1))], out_specs=colb,
        out_shape=jax.ShapeDtypeStruct((TOP_K, t), I32),
        compiler_params=_cparams(("parallel",)), name="moe_pos",
    )(idx, rank, pstart_col)


def _padfill_kernel(cnt_ref, pad_ref, pst_ref, xs_ref, zero_sc, sem):
    zero_sc[...] = jnp.zeros(zero_sc.shape, zero_sc.dtype)
    sizes = [1 << b for b in reversed(range(MOE_BLK.bit_length() - 1))]

    def chunks(e, act):
        n = pad_ref[e] - cnt_ref[e]
        row = pst_ref[e] + cnt_ref[e]
        for size in sizes:
            @pl.when((n & size) != 0)
            def _():
                act(pltpu.make_async_copy(zero_sc.at[pl.ds(0, size)], xs_ref.at[pl.ds(row, size)], sem))

            row = row + (n & size)

    def issue(e, c):
        chunks(e, lambda cp: cp.start())
        return c

    def drain(e, c):
        chunks(e, lambda cp: cp.wait())
        return c

    lax.fori_loop(0, N_EXPERTS, issue, 0)
    lax.fori_loop(0, N_EXPERTS, drain, 0)


def _padfill(counts, padded, pstart, n_rows, d):
    row_tile = (d // LANES, LANES)
    return pl.pallas_call(
        _padfill_kernel,
        grid_spec=pltpu.PrefetchScalarGridSpec(
            num_scalar_prefetch=3, grid=(1,), in_specs=[],
            out_specs=pl.BlockSpec(memory_space=pl.ANY),
            scratch_shapes=[pltpu.VMEM((MOE_BLK // 2,) + row_tile, F32), pltpu.SemaphoreType.DMA(())]),
        out_shape=jax.ShapeDtypeStruct((n_rows,) + row_tile, F32),
        compiler_params=pltpu.CompilerParams(dimension_semantics=("arbitrary",), has_side_effects=True),
        name="moe_padfill",
    )(counts, padded, pstart)


def _dispatch_kernel(pos_ref, h2_ref, xs_in_ref, xs_ref, sem, *, tm):
    del xs_in_ref

    for t in range(tm):
        for kk in range(TOP_K):
            pltpu.make_async_copy(h2_ref.at[t], xs_ref.at[pos_ref[kk, t]], sem).start(priority=kk % 2)
    for kk in range(TOP_K):
        pltpu.make_async_copy(h2_ref, xs_ref.at[pl.ds(0, tm)], sem).wait()


def _dispatch(pos, h2_flat, xs, tm):
    h2 = h2_flat.reshape(-1, ROW_CHUNKS, LANES)
    t = h2.shape[0]
    return pl.pallas_call(
        functools.partial(_dispatch_kernel, tm=tm),
        grid=(t // tm,),
        in_specs=[pl.BlockSpec((TOP_K, tm), lambda i: (0, i), memory_space=pltpu.SMEM),
                  pl.BlockSpec((tm,) + h2.shape[1:], lambda i: (i, 0, 0)),
                  pl.BlockSpec(memory_space=pl.ANY)],
        out_specs=pl.BlockSpec(memory_space=pl.ANY),
        out_shape=jax.ShapeDtypeStruct(xs.shape, xs.dtype),
        scratch_shapes=[pltpu.SemaphoreType.DMA(())],
        input_output_aliases={2: 0},
        compiler_params=pltpu.CompilerParams(dimension_semantics=("arbitrary",), has_side_effects=True,
                                             vmem_limit_bytes=VMEM_LIMIT),
        name="moe_dispatch",
    )(pos, h2, xs)


def _moe_kernel(be_ref, nu_ref, first_ref, slot_ref, nxt_ref, x_ref, wg_ref, wu_ref, wd_ref, y_ref,
                wg_buf, wu_buf, wd_buf, wgu_sc, wd_sc, sem):
    i = pl.program_id(0)
    e = be_ref[i]

    def fetch(expert, slot):
        return (pltpu.make_async_copy(wg_ref.at[expert], wg_buf.at[slot], sem.at[slot]),
                pltpu.make_async_copy(wu_ref.at[expert], wu_buf.at[slot], sem.at[slot]),
                pltpu.make_async_copy(wd_ref.at[expert], wd_buf.at[slot], sem.at[slot]))

    @pl.when(first_ref[i] == 1)
    def _switch_expert():
        slot = slot_ref[i]

        @pl.when(i == 0)
        def _():
            for cp in fetch(e, slot):
                cp.start()

        for cp in fetch(e, slot):
            cp.wait()
        nxt = nxt_ref[e]

        @pl.when(nxt >= 0)
        def _():
            for cp in fetch(nxt, 1 - slot):
                cp.start()

        wgu_sc[:, 0:D_EXPERT] = wg_buf[slot].astype(BF16)
        wgu_sc[:, D_EXPERT:2 * D_EXPERT] = wu_buf[slot].astype(BF16)
        wd_sc[...] = wd_buf[slot].astype(BF16)

    @pl.when(i < nu_ref[0])
    def _compute():
        gu = jnp.dot(_load_rows(x_ref).astype(BF16), wgu_sc[...], preferred_element_type=F32)
        gt = gu[:, 0:D_EXPERT]
        h = gt * _sigmoid(gt) * gu[:, D_EXPERT:2 * D_EXPERT]
        _store_rows(y_ref, jnp.dot(h.astype(BF16), wd_sc[...], preferred_element_type=F32))


def _moe(blk_expert, n_used, first, slot, nxt, xs_pool, w_gate, w_up, w_down):
    n_rows = xs_pool.shape[0]
    xs = xs_pool.reshape(n_rows * ROW_CHUNKS, LANES)
    d = w_gate.shape[1]
    nblk = n_rows // MOE_BLK
    xmap = lambda i, be, nu, fi, sl, nx: (jnp.minimum(i, nu[0] - 1), 0)
    hbm = pl.BlockSpec(memory_space=pl.ANY)
    return pl.pallas_call(
        _moe_kernel,
        grid_spec=pltpu.PrefetchScalarGridSpec(
            num_scalar_prefetch=5, grid=(nblk,),
            in_specs=[pl.BlockSpec((MOE_BLK * ROW_CHUNKS, LANES), xmap), hbm, hbm, hbm],
            out_specs=pl.BlockSpec((MOE_BLK * ROW_CHUNKS, LANES), xmap),
            scratch_shapes=[pltpu.VMEM((2, d, D_EXPERT), F32), pltpu.VMEM((2, d, D_EXPERT), F32),
                            pltpu.VMEM((2, D_EXPERT, d), F32),
                            pltpu.VMEM((d, 2 * D_EXPERT), BF16), pltpu.VMEM((D_EXPERT, d), BF16),
                            pltpu.SemaphoreType.DMA((2,))]),
        out_shape=jax.ShapeDtypeStruct(xs.shape, F32),
        compiler_params=_cparams(("arbitrary",)), name="moe_experts",
    )(blk_expert, n_used, first, slot, nxt, xs, w_gate, w_up, w_down)


def _combine_kernel(pos_ref, posn_ref, x1_ref, h2_ref, g2_ref, wt_ref, wsgu_ref, wsd_ref, ys_ref, ysflat_ref,
                    o_ref, buf, sem, *, tm):
    lin = pl.program_id(0) * pl.num_programs(1) + pl.program_id(1)
    total = pl.num_programs(0) * pl.num_programs(1)
    slot = lin % 2

    def issue(p_ref, s):
        for t in range(tm):
            for kk in range(TOP_K):
                pltpu.make_async_copy(ys_ref.at[p_ref[kk, t]], buf.at[s, kk, pl.ds(t * ROW_CHUNKS, ROW_CHUNKS)],
                                      sem.at[s]).start(priority=kk % 2)

    @pl.when(lin == 0)
    def _():
        issue(pos_ref, 0)

    issue(posn_ref, 1 - slot)

    gu = jnp.dot(_load_rows(h2_ref).astype(BF16), wsgu_ref[...], preferred_element_type=F32)
    gt = gu[:, 0:D_SHARED]
    hs = gt * _sigmoid(gt) * gu[:, D_SHARED:2 * D_SHARED]
    shared = jnp.dot(hs.astype(BF16), wsd_ref[...], preferred_element_type=F32)

    for kk in range(TOP_K):
        pltpu.make_async_copy(ysflat_ref.at[pl.ds(0, tm * ROW_CHUNKS)], buf.at[slot, kk], sem.at[slot]).wait()

    nchunk = shared.shape[-1] // LANES
    acc = [shared[:, c * LANES:(c + 1) * LANES] for c in range(nchunk)]
    for kk in range(TOP_K):
        wrow = jnp.broadcast_to(wt_ref[kk:kk + 1, :], (LANES, tm))
        wcol = wrow.T
        for c in range(nchunk):
            acc[c] = acc[c] + _row_chunk(buf.at[slot, kk], c)[...] * wcol
    o_ref[...] = x1_ref[...] + g2_ref[0] * jnp.concatenate(acc, axis=1)

    @pl.when(lin + 1 == total)
    def _():
        for kk in range(TOP_K):
            pltpu.make_async_copy(ysflat_ref.at[pl.ds(0, tm * ROW_CHUNKS)], buf.at[1 - slot, kk],
                                  sem.at[1 - slot]).wait()


def _combine(pos, x1, h2, gate2, wts, wsgu, wsd, ys_flat, nb, tm):
    t, d = x1.shape
    ns = t // nb // tm
    row = pl.BlockSpec((tm, d), lambda b, i: (b * ns + i, 0))
    last = nb * ns - 1
    ys = ys_flat.reshape(-1, ROW_CHUNKS, LANES)
    return pl.pallas_call(
        functools.partial(_combine_kernel, tm=tm),
        grid=(nb, ns),
        in_specs=[pl.BlockSpec((TOP_K, tm), lambda b, i: (0, b * ns + i), memory_space=pltpu.SMEM),
                  pl.BlockSpec((TOP_K, tm), lambda b, i: (0, jnp.minimum(b * ns + i + 1, last)),
                               memory_space=pltpu.SMEM),
                  row, pl.BlockSpec((tm * ROW_CHUNKS, LANES), lambda b, i: (b * ns + i, 0)),
                  _mod_spec(gate2, tm, ns),
                  pl.BlockSpec((TOP_K, tm), lambda b, i: (0, b * ns + i)),
                  _const_spec(wsgu.shape), _const_spec(wsd.shape),
                  pl.BlockSpec(memory_space=pl.ANY), pl.BlockSpec(memory_space=pl.ANY)],
        out_specs=row,
        out_shape=jax.ShapeDtypeStruct((t, d), F32),
        scratch_shapes=[pltpu.VMEM((2, TOP_K, tm * ROW_CHUNKS, LANES), F32), pltpu.SemaphoreType.DMA((2,))],
        compiler_params=_cparams(("arbitrary", "arbitrary")), name="moe_combine",
    )(pos, pos, x1, h2, gate2, wts, wsgu, wsd, ys, ys_flat)


def _prepare_params(g_norm1, w_in, b_fox_f, g_q, g_k, hgrn_lb, g_hgrn_o, w_proj_a, w_proj_b, w_out,
                    g_norm2, w_router, b_router, w_sh_gate, w_sh_up, w_sh_down):
    d = D_MODEL
    fw, hw = FOX_WIDTH, HGRN_WIDTH
    c0 = 3 * fw
    c1 = c0 + FOX_HEADS
    c2 = c1 + 4 * hw
    wff = w_in[:, c0:c1]
    head = jnp.arange(fw) // FOX_HEAD_DIM
    wr_t = w_router.T
    wrh = wr_t.astype(BF16)
    return dict(
        g1=g_norm1.reshape(1, d),
        wqkv=w_in[:, :c0].astype(BF16),
        wff=jnp.pad(wff, ((0, 0), (0, LANES - FOX_HEADS))).astype(BF16),
        wfft=jnp.pad(wff.T, ((0, 16 - FOX_HEADS), (0, 0))).astype(BF16),
        bf=b_fox_f.reshape(1, FOX_HEADS), bft=b_fox_f.reshape(FOX_HEADS, 1),
        gq=jnp.tile(g_q, FOX_HEADS).reshape(1, fw), gk=jnp.tile(g_k, FOX_HEADS).reshape(1, fw),
        bd=(head[:, None] == head[None, :]).astype(BF16) * (1.0 / FOX_HEAD_DIM),
        lbp=hgrn_lb,
        wh=w_in[:, c1:c2].astype(BF16), wg=w_in[:, c2:].astype(BF16),
        gn=g_hgrn_o.reshape(1, HGRN_DK),
        g2=g_norm2.reshape(1, d),
        wpa=w_proj_a.astype(BF16), wpb=w_proj_b.astype(BF16), wo=w_out.astype(BF16),
        wrh=wrh, wrl=(wr_t - wrh.astype(F32)).astype(BF16), br=b_router.reshape(N_EXPERTS, 1),
        wsgu=jnp.concatenate([w_sh_gate, w_sh_up], axis=1).astype(BF16), wsd=w_sh_down.astype(BF16),
    )


def _mixers(x2, nb, seq, tm, shift1, scale1, prm, past, tile, tb):
    (q, k, kb, v, vb, lf, lft, hq, hl, hi, og, ga, gb) = _inproj(x2, nb if shift1.shape[1] == 1 else 1, tm,
                                                                 shift1, scale1, prm)
    t = nb * seq
    hp = FOX_HEADS // 2
    lft = lft.transpose(1, 0, 2).reshape(FOX_HEADS, nb, seq).transpose(1, 0, 2)
    if past is None:
        fcum = _cumsum_lanes(lft.reshape(nb * FOX_HEADS, seq)).reshape(nb, hp, 2, seq)
        ya = _fox_prompt(q, kb, vb, fcum, nb, seq, tile, min(32, tile))
        s0 = None
    else:
        cache_k, cache_v, cache_lf, s0 = past
        plen = cache_k.shape[1]
        ltot = plen + seq
        lpad = -(-ltot // LANES) * LANES
        lf_all = jnp.concatenate([cache_lf.transpose(0, 2, 1), lft,
                                  jnp.zeros((nb, FOX_HEADS, lpad - ltot), F32)], axis=-1)
        fcum = _cumsum_lanes(lf_all.reshape(nb * FOX_HEADS, lpad)).reshape(nb, hp, 2, lpad)
        ya = _fox_sample(q, kb, vb, cache_k.reshape(nb, plen, FOX_WIDTH), cache_v.reshape(nb, plen, FOX_WIDTH),
                         fcum, nb, seq, plen)
    yb, sfin = _hgrn(hq, hl, hi, og, prm["gn"], s0, nb, seq, tb)
    return k, v, lf, sfin, ya, yb, ga, gb


def kernel(x_prompt, x_sample, cache_fox_k, cache_fox_v, cache_fox_logf, state_hgrn, c_prompt, c_sample,
           w_ada, b_ada, g_norm1, w_in, b_fox_f, g_q, g_k, hgrn_lb, g_hgrn_o, w_proj_a, w_proj_b, w_out,
           g_norm2, w_router, b_router, w_exp_gate, w_exp_up, w_exp_down, w_sh_gate, w_sh_up, w_sh_down):
    assert w_ada.shape[0] == 1 and hgrn_lb.shape[0] == 2, "single-layer trunk"
    d = D_MODEL
    bp, sp, _ = x_prompt.shape
    bs, ss, _ = x_sample.shape
    tp, ts = bp * sp, bs * ss
    prm = _prepare_params(g_norm1[0], w_in[0], b_fox_f[0], g_q[0], g_k[0], hgrn_lb, g_hgrn_o[0],
                          w_proj_a[0], w_proj_b[0], w_out[0], g_norm2[0], w_router[0], b_router[0],
                          w_sh_gate[0], w_sh_up[0], w_sh_down[0])

    bc = bp + bs
    bc_pad = -(-bc // 8) * 8
    c_all = jnp.concatenate([c_prompt, c_sample, jnp.zeros((bc_pad - bc, d), F32)], axis=0)
    mod = _ada(c_all, w_ada[0], b_ada[0])
    mod_p = [mod[:bp, j * d:(j + 1) * d].reshape(bp, 1, d) for j in range(6)]
    mod_s = [jnp.repeat(mod[bp:bc, j * d:(j + 1) * d], ss, axis=0).reshape(1, ts, d) for j in range(6)]

    tm_p = min(512, sp)
    tm_s = min(256, ts)
    xp2 = x_prompt.reshape(tp, d)
    xs2 = x_sample.reshape(ts, d)
    tile = min(1024, sp)
    tb = min(256, sp)

    kp, vp, lfp, sfin_p, ya_p, yb_p, ga_p, gb_p = _mixers(
        xp2, bp, sp, tm_p, mod_p[0], mod_p[1], prm, None, tile, tb)
    past = (cache_fox_k[0], cache_fox_v[0], cache_fox_logf[0], state_hgrn[0])
    ks, vs, lfs, sfin_s, ya_s, yb_s, ga_s, gb_s = _mixers(
        xs2, bs, ss, tm_s, mod_s[0], mod_s[1], prm, past, None, min(256, ss))

    zero_cnt = jnp.zeros((N_EXPERTS, 1), F32)
    x1_p, h2_p, idx_p, wt_p, rank_p, cnt_p = _post(xp2, ya_p, yb_p, ga_p, gb_p, mod_p[2], mod_p[3], mod_p[4],
                                                    prm, zero_cnt, bp, tm_p)
    x1_s, h2_s, idx_s, wt_s, rank_s, cnt_all = _post(xs2, ya_s, yb_s, ga_s, gb_s, mod_s[2], mod_s[3], mod_s[4],
                                                     prm, cnt_p, 1, tm_s)

    counts = cnt_all.reshape(N_EXPERTS).astype(I32)
    padded = (counts + MOE_BLK - 1) // MOE_BLK * MOE_BLK
    pend = jnp.cumsum(padded)
    pstart = pend - padded
    nblk = -(-((tp + ts) * TOP_K + N_EXPERTS * (MOE_BLK - 1)) // MOE_BLK)
    n_used = (pend[-1] // MOE_BLK).reshape(1)
    blk_row0 = jnp.arange(nblk, dtype=I32) * MOE_BLK
    blk_expert = jnp.minimum(jnp.sum((pend[None, :] <= blk_row0[:, None]).astype(I32), axis=1), N_EXPERTS - 1)
    blk_used = jnp.arange(nblk) < n_used[0]
    blk_expert = jnp.where(blk_used, blk_expert, blk_expert[jnp.maximum(n_used[0] - 1, 0)])
    blk_first = blk_used & jnp.concatenate([jnp.ones((1,), bool), blk_expert[1:] != blk_expert[:-1]])
    blk_slot = ((jnp.cumsum(blk_first.astype(I32)) - 1) & 1).astype(I32)
    eids = jnp.arange(N_EXPERTS, dtype=I32)
    later = (eids[None, :] > eids[:, None]) & (padded[None, :] > 0)
    nxt_expert = jnp.min(jnp.where(later, eids[None, :], N_EXPERTS), axis=1)
    nxt_expert = jnp.where(nxt_expert < N_EXPERTS, nxt_expert, -1).astype(I32)
    pstart_col = pstart.astype(F32).reshape(N_EXPERTS, 1)
    pos_p = _pos(idx_p, rank_p, pstart_col, tm_p)
    pos_s = _pos(idx_s, rank_s, pstart_col, tm_s)

    xs_pool = _padfill(counts, padded, pstart, nblk * MOE_BLK, d)
    xs_pool = _dispatch(pos_p, h2_p, xs_pool, min(128, tm_p))
    xs_pool = _dispatch(pos_s, h2_s, xs_pool, min(128, tm_s))
    ys_pool = _moe(blk_expert, n_used, blk_first.astype(I32), blk_slot, nxt_expert, xs_pool,
                   w_exp_gate[0], w_exp_up[0], w_exp_down[0])

    tm_c = min(128, sp)
    y_p = _combine(pos_p, x1_p, h2_p, mod_p[5], wt_p, prm["wsgu"], prm["wsd"], ys_pool, bp, tm_c)
    y_s = _combine(pos_s, x1_s, h2_s, mod_s[5], wt_s, prm["wsgu"], prm["wsd"], ys_pool, 1, min(128, ts))

    return (y_p.reshape(bp, sp, d), y_s.reshape(bs, ss, d),
            kp.reshape(1, bp, sp, FOX_HEADS, FOX_HEAD_DIM), vp.reshape(1, bp, sp, FOX_HEADS, FOX_HEAD_DIM),
            lfp.reshape(1, bp, sp, FOX_HEADS), sfin_p[None],
            ks.reshape(1, bs, ss, FOX_HEADS, FOX_HEAD_DIM), vs.reshape(1, bs, ss, FOX_HEADS, FOX_HEAD_DIM),
            lfs.reshape(1, bs, ss, FOX_HEADS), sfin_s[None])
```

```python
import functools

import jax
import jax.numpy as jnp
from jax import lax
from jax.experimental import pallas as pl
from jax.experimental.pallas import tpu as pltpu

F32 = jnp.float32
BF16 = jnp.bfloat16
I32 = jnp.int32

D_MODEL = 1024
FOX_HEADS = 8
FOX_HEAD_DIM = 64
FOX_WIDTH = FOX_HEADS * FOX_HEAD_DIM
HGRN_HEADS = 4
HGRN_DK = 128
HGRN_WIDTH = HGRN_HEADS * HGRN_DK
N_EXPERTS = 256
TOP_K = 8
N_GROUPS = 8
TOPK_GROUPS = 4
GROUP_SIZE = N_EXPERTS // N_GROUPS
D_EXPERT = 256
D_SHARED = 256
ROUTED_SCALE = 2.5
NORM_EPS = 1e-6
NEG_BIG = -1e30
LOG2E = 1.4426950408889634
QK_SCALE = FOX_HEAD_DIM ** -0.5 * LOG2E

LANES = 128
SUB = 16
MOE_BLK = 512
VMEM_LIMIT = 56 * 1024 * 1024

NT_DIMS = (((1,), (1,)), ((), ()))
TN_DIMS = (((0,), (0,)), ((), ()))


def _cparams(sem):
    return pltpu.CompilerParams(dimension_semantics=sem, vmem_limit_bytes=VMEM_LIMIT)


def _const_spec(shape):
    nd = len(shape)
    return pl.BlockSpec(shape, lambda *_: (0,) * nd, pipeline_mode=pl.Buffered(1))


def _sigmoid(z):
    return 1.0 / (1.0 + jnp.exp(-z))


def _log_sigmoid(z):
    return jnp.minimum(z, 0.0) - jnp.log(1.0 + jnp.exp(-jnp.abs(z)))


ROW_CHUNKS = D_MODEL // LANES


def _row_chunk(ref, c):
    return ref.at[pl.ds(c, ref.shape[0] // ROW_CHUNKS, stride=ROW_CHUNKS), :]


def _store_rows(ref, val):
    for c in range(ROW_CHUNKS):
        _row_chunk(ref, c)[...] = val[:, c * LANES:(c + 1) * LANES]


def _load_rows(ref):
    return jnp.concatenate([_row_chunk(ref, c)[...] for c in range(ROW_CHUNKS)], axis=1)


def _split3(a):
    hi = a.astype(BF16)
    r1 = a - hi.astype(F32)
    mid = r1.astype(BF16)
    lo = (r1 - mid.astype(F32)).astype(BF16)
    return hi, mid, lo


def _ada_kernel(c_ref, w_ref, b_ref, o_ref):
    c = c_ref[...]
    s = c * _sigmoid(c)
    o_ref[...] = jnp.dot(s.astype(BF16), w_ref[...].astype(BF16), preferred_element_type=F32) + b_ref[...]


def _ada(c_all, w_ada, b_ada):
    bc, d = c_all.shape
    n = w_ada.shape[1]
    tn = 1024
    return pl.pallas_call(
        _ada_kernel,
        grid=(n // tn,),
        in_specs=[pl.BlockSpec((bc, d), lambda j: (0, 0)),
                  pl.BlockSpec((d, tn), lambda j: (0, j)),
                  pl.BlockSpec((1, tn), lambda j: (0, j))],
        out_specs=pl.BlockSpec((bc, tn), lambda j: (0, j)),
        out_shape=jax.ShapeDtypeStruct((bc, n), F32),
        compiler_params=_cparams(("parallel",)),
        name="ada",
    )(c_all, w_ada, b_ada.reshape(1, n))


def _inproj_kernel(x_ref, sh_ref, sc_ref, g1_ref, wqkv_ref, wff_ref, wfft_ref, bf_ref, bft_ref,
                   gq_ref, gk_ref, bd_ref, lbp_ref, wh_ref, wg_ref,
                   q_ref, k_ref, kb_ref, v_ref, vb_ref, lf_ref, lft_ref,
                   hq_ref, hl_ref, hi_ref, og_ref, ga_ref, gb_ref):
    x = x_ref[...]
    ms = jnp.mean(x * x, axis=-1, keepdims=True)
    h = x * lax.rsqrt(ms + NORM_EPS) * g1_ref[...]
    h = h * (1.0 + sc_ref[0]) + sh_ref[0]
    hb = h.astype(BF16)

    def headnorm(a, g):
        ss = jnp.dot((a * a).astype(BF16), bd_ref[...], preferred_element_type=F32)
        return a * lax.rsqrt(ss + NORM_EPS) * g

    fq = jnp.dot(hb, wqkv_ref[:, 0:FOX_WIDTH], preferred_element_type=F32)
    q_ref[...] = (headnorm(fq, gq_ref[...]) * QK_SCALE).astype(BF16)
    fk = jnp.dot(hb, wqkv_ref[:, FOX_WIDTH:2 * FOX_WIDTH], preferred_element_type=F32)
    k = headnorm(fk, gk_ref[...])
    kb_ref[...] = k.astype(BF16)
    fv = jnp.dot(hb, wqkv_ref[:, 2 * FOX_WIDTH:3 * FOX_WIDTH], preferred_element_type=F32)
    vb_ref[...] = fv.astype(BF16)
    k_ref[...] = k.reshape(k_ref.shape)
    v_ref[...] = fv.reshape(v_ref.shape)

    ff = jnp.dot(hb, wff_ref[...], preferred_element_type=F32)
    lf_ref[...] = _log_sigmoid(ff[:, 0:FOX_HEADS] + bf_ref[...])
    fft = lax.dot_general(wfft_ref[...], hb, NT_DIMS, preferred_element_type=F32)
    lft_ref[0] = _log_sigmoid(fft[0:FOX_HEADS, :] + bft_ref[...])

    lbp = lbp_ref[...]
    e = jnp.exp(lbp - jnp.max(lbp, axis=0, keepdims=True))
    lb = e[0:1, :] / jnp.sum(e, axis=0, keepdims=True)

    w = HGRN_WIDTH
    hq = jnp.dot(hb, wh_ref[:, 0:w], preferred_element_type=F32)
    hq_ref[...] = hq * _sigmoid(hq)
    hf = jnp.dot(hb, wh_ref[:, w:2 * w], preferred_element_type=F32)
    hl_ref[...] = jnp.log(lb + (1.0 - lb) * _sigmoid(hf))
    hi_ref[...] = jnp.dot(hb, wh_ref[:, 2 * w:3 * w], preferred_element_type=F32)
    hg = jnp.dot(hb, wh_ref[:, 3 * w:4 * w], preferred_element_type=F32)
    og_ref[...] = _sigmoid(hg)
    ga = jnp.dot(hb, wg_ref[:, 0:D_MODEL], preferred_element_type=F32)
    ga_ref[...] = _sigmoid(ga).astype(BF16)
    gb = jnp.dot(hb, wg_ref[:, D_MODEL:2 * D_MODEL], preferred_element_type=F32)
    gb_ref[...] = _sigmoid(gb).astype(BF16)


def _mod_spec(arr, tm, ns):
    d = arr.shape[-1]
    if arr.shape[1] == 1:
        return pl.BlockSpec((1, 1, d), lambda b, i: (b, 0, 0))
    return pl.BlockSpec((1, tm, d), lambda b, i: (0, b * ns + i, 0))


def _inproj(x2, nb, tm, shift1, scale1, prm):
    t, d = x2.shape
    ns = t // nb // tm
    row = lambda w: pl.BlockSpec((tm, w), lambda b, i: (b * ns + i, 0))
    in_specs = [row(d), _mod_spec(shift1, tm, ns), _mod_spec(scale1, tm, ns),
                _const_spec((1, d)), _const_spec(prm["wqkv"].shape), _const_spec(prm["wff"].shape),
                _const_spec(prm["wfft"].shape), _const_spec((1, FOX_HEADS)), _const_spec((FOX_HEADS, 1)),
                _const_spec((1, FOX_WIDTH)), _const_spec((1, FOX_WIDTH)), _const_spec((FOX_WIDTH, FOX_WIDTH)),
                _const_spec(prm["lbp"].shape), _const_spec(prm["wh"].shape), _const_spec(prm["wg"].shape)]
    fw, hw = FOX_WIDTH, HGRN_WIDTH
    heads = pl.BlockSpec((tm, FOX_HEADS, FOX_HEAD_DIM), lambda b, i: (b * ns + i, 0, 0))
    out_shape = [jax.ShapeDtypeStruct((t, fw), BF16),
                 jax.ShapeDtypeStruct((t, FOX_HEADS, FOX_HEAD_DIM), F32),
                 jax.ShapeDtypeStruct((t, fw), BF16),
                 jax.ShapeDtypeStruct((t, FOX_HEADS, FOX_HEAD_DIM), F32),
                 jax.ShapeDtypeStruct((t, fw), BF16),
                 jax.ShapeDtypeStruct((t, FOX_HEADS), F32),
                 jax.ShapeDtypeStruct((nb * ns, FOX_HEADS, tm), F32),
                 jax.ShapeDtypeStruct((t, hw), F32),
                 jax.ShapeDtypeStruct((t, hw), F32),
                 jax.ShapeDtypeStruct((t, hw), F32),
                 jax.ShapeDtypeStruct((t, hw), F32),
                 jax.ShapeDtypeStruct((t, d), BF16),
                 jax.ShapeDtypeStruct((t, d), BF16)]
    out_specs = [row(fw), heads, row(fw), heads, row(fw), row(FOX_HEADS),
                 pl.BlockSpec((1, FOX_HEADS, tm), lambda b, i: (b * ns + i, 0, 0)),
                 row(hw), row(hw), row(hw), row(hw), row(d), row(d)]
    return pl.pallas_call(
        _inproj_kernel, grid=(nb, ns), in_specs=in_specs, out_specs=out_specs, out_shape=out_shape,
        compiler_params=_cparams(("parallel", "parallel")), name="inproj",
    )(x2, shift1, scale1, prm["g1"], prm["wqkv"], prm["wff"], prm["wfft"], prm["bf"], prm["bft"],
      prm["gq"], prm["gk"], prm["bd"], prm["lbp"], prm["wh"], prm["wg"])


def _cumsum_kernel(x_ref, o_ref):
    x = x_ref[...]
    n = x.shape[-1]
    lane = lax.broadcasted_iota(I32, x.shape, 1)
    s = 1
    while s < n:
        x = x + jnp.where(lane >= s, pltpu.roll(x, s, axis=1), 0.0)
        s *= 2
    o_ref[...] = x * LOG2E


def _cumsum_lanes(x):
    return pl.pallas_call(
        _cumsum_kernel, out_shape=jax.ShapeDtypeStruct(x.shape, F32),
        compiler_params=pltpu.CompilerParams(vmem_limit_bytes=VMEM_LIMIT), name="cumsum",
    )(x)


FOX_PAIRS_PER_STEP = 2


def _fox_prompt_kernel(qt_ref, kt_ref, q_ref, k_ref, v_ref, fk_ref, o_ref, *scratch, tile, rsub):
    step = pl.program_id(2)
    qi = qt_ref[step]
    ki = kt_ref[step]
    nheads = 2 * FOX_PAIRS_PER_STEP
    heads = [scratch[4 * h:4 * h + 4] for h in range(nheads)]
    lane = lax.broadcasted_iota(I32, (1, LANES), 1)
    lo_half = lane < FOX_HEAD_DIM

    @pl.when(ki == 0)
    def _init():
        for m_sc, a_sc, _, _ in heads:
            m_sc[...] = jnp.full(m_sc.shape, NEG_BIG, F32)
            a_sc[...] = jnp.zeros(a_sc.shape, F32)

    def tile_update(diagonal):
        nsub = tile // rsub
        if diagonal:
            ahead = (lax.broadcasted_iota(I32, (rsub, tile), 1)
                     - lax.broadcasted_iota(I32, (rsub, tile), 0))
        for h, (_, _, s_sc, _) in enumerate(heads):
            lanes = slice((h // 2) * LANES, (h // 2 + 1) * LANES)
            q = q_ref[:, lanes]
            sel = lo_half if h % 2 == 0 else jnp.logical_not(lo_half)
            qh = jnp.where(sel, q, jnp.zeros_like(q))
            s_sc[...] = lax.dot_general(qh, k_ref[:, lanes], NT_DIMS, preferred_element_type=F32)
        for h, (m_sc, a_sc, s_sc, p_sc) in enumerate(heads):
            lanes = slice((h // 2) * LANES, (h // 2 + 1) * LANES)
            v = v_ref[:, lanes]
            fk = fk_ref[0, h // 2]
            hs = h % 2
            sel = lo_half if hs == 0 else jnp.logical_not(lo_half)
            v_ones = jnp.where(sel, v, jnp.ones_like(v))

            def biased(r):
                rows = slice(r * rsub, (r + 1) * rsub)
                cw = min(tile, -(-((r + 1) * rsub) // LANES) * LANES) if diagonal else tile
                s = s_sc[rows, 0:cw] - fk[hs:hs + 1, 0:cw]
                if diagonal:
                    s = jnp.where(ahead[:, 0:cw] <= r * rsub, s, NEG_BIG)
                return s, rows, cw

            m_old = m_sc[...]
            m_parts = []
            for r in range(nsub):
                s, rows, _ = biased(r)
                m_parts.append(jnp.maximum(m_old[rows], jnp.max(s, axis=1, keepdims=True)))
            for r in range(nsub):
                s, rows, cw = biased(r)
                nc = cw // LANES
                p = jnp.exp2(s - jnp.concatenate([m_parts[r]] * nc, axis=1))
                p_sc[rows, 0:cw] = p.astype(BF16)
                if cw < tile:
                    p_sc[rows, cw:tile] = jnp.zeros((rsub, tile - cw), BF16)
            m_new = jnp.concatenate(m_parts, axis=0)
            alpha = jnp.exp2(m_old - m_new)
            m_sc[...] = m_new
            a_sc[...] = alpha * a_sc[...] + jnp.dot(p_sc[...], v_ones, preferred_element_type=F32)

    @pl.when(ki < qi)
    def _full():
        tile_update(False)

    @pl.when(ki == qi)
    def _diag():
        tile_update(True)
        for pair in range(FOX_PAIRS_PER_STEP):
            acc0 = heads[2 * pair][1][...]
            acc1 = heads[2 * pair + 1][1][...]
            o0 = acc0 / pltpu.roll(acc0, FOX_HEAD_DIM, axis=1)
            o1 = acc1 / pltpu.roll(acc1, FOX_HEAD_DIM, axis=1)
            o_ref[:, pair * LANES:(pair + 1) * LANES] = jnp.where(lo_half, o0, o1).astype(o_ref.dtype)


def _fox_prompt(q, kb, vb, fcum, nb, seq, tile, rsub):
    nt = seq // tile
    pps = FOX_PAIRS_PER_STEP
    hp = FOX_HEADS // 2 // pps
    width = pps * LANES
    qt = jnp.asarray([qi for qi in range(nt) for _ in range(qi + 1)], I32)
    kt = jnp.asarray([ki for qi in range(nt) for ki in range(qi + 1)], I32)
    qmap = lambda b, h, s, qt, kt: (b * nt + qt[s], h)
    kmap = lambda b, h, s, qt, kt: (b * nt + kt[s], h)
    fmap = lambda b, h, s, qt, kt: (b, h, 0, kt[s])
    per_head = [pltpu.VMEM((tile, LANES), F32), pltpu.VMEM((tile, LANES), F32),
                pltpu.VMEM((tile, tile), F32), pltpu.VMEM((tile, tile), BF16)]
    return pl.pallas_call(
        functools.partial(_fox_prompt_kernel, tile=tile, rsub=rsub),
        grid_spec=pltpu.PrefetchScalarGridSpec(
            num_scalar_prefetch=2, grid=(nb, hp, nt * (nt + 1) // 2),
            in_specs=[pl.BlockSpec((tile, width), qmap), pl.BlockSpec((tile, width), kmap),
                      pl.BlockSpec((tile, width), kmap), pl.BlockSpec((1, pps, 2, tile), fmap)],
            out_specs=pl.BlockSpec((tile, width), qmap),
            scratch_shapes=per_head * (2 * pps)),
        out_shape=jax.ShapeDtypeStruct((nb * seq, FOX_WIDTH), BF16),
        compiler_params=_cparams(("parallel", "parallel", "arbitrary")),
        name="fox_prompt",
    )(qt, kt, q, kb, vb, fcum)


def _fox_sample_kernel(q_ref, kn_ref, vn_ref, kp_ref, vp_ref, fk_ref, o_ref, *, past, seq):
    lane = lax.broadcasted_iota(I32, (1, LANES), 1)
    lo_half = lane < FOX_HEAD_DIM
    q = q_ref[...]
    kn = kn_ref[...]
    vn = vn_ref[...]
    kp = kp_ref[0].astype(BF16)
    vp = vp_ref[0].astype(BF16)
    fk = fk_ref[0, 0]
    row = lax.broadcasted_iota(I32, (seq, seq), 0)
    col = lax.broadcasted_iota(I32, (seq, seq), 1)
    causal = col <= row
    outs = []
    for hs in range(2):
        sel = lo_half if hs == 0 else jnp.logical_not(lo_half)
        qh = jnp.where(sel, q, jnp.zeros_like(q))
        sp = lax.dot_general(qh, kp, NT_DIMS, preferred_element_type=F32) - fk[hs:hs + 1, 0:past]
        sn = lax.dot_general(qh, kn, NT_DIMS, preferred_element_type=F32) - fk[hs:hs + 1, past:past + seq]
        sn = jnp.where(causal, sn, NEG_BIG)
        m = jnp.maximum(jnp.max(sp, axis=1, keepdims=True), jnp.max(sn, axis=1, keepdims=True))
        pp = jnp.exp2(sp - m)
        pn = jnp.exp2(sn - m)
        l = jnp.sum(pp, axis=1, keepdims=True) + jnp.sum(pn, axis=1, keepdims=True)
        o = (jnp.dot(pp.astype(BF16), vp, preferred_element_type=F32)
             + jnp.dot(pn.astype(BF16), vn, preferred_element_type=F32))
        outs.append(o / l)
    o_ref[...] = jnp.where(lo_half, outs[0], outs[1]).astype(o_ref.dtype)


def _fox_sample(q, kb, vb, cache_k, cache_v, fcum, nb, seq, past):
    hp = FOX_HEADS // 2
    lpad = fcum.shape[-1]
    rmap = lambda b, h: (b, h)
    cmap = lambda b, h: (b, 0, h)
    return pl.pallas_call(
        functools.partial(_fox_sample_kernel, past=past, seq=seq),
        grid=(nb, hp),
        in_specs=[pl.BlockSpec((seq, LANES), rmap), pl.BlockSpec((seq, LANES), rmap),
                  pl.BlockSpec((seq, LANES), rmap),
                  pl.BlockSpec((1, past, LANES), cmap), pl.BlockSpec((1, past, LANES), cmap),
                  pl.BlockSpec((1, 1, 2, lpad), lambda b, h: (b, h, 0, 0))],
        out_specs=pl.BlockSpec((seq, LANES), rmap),
        out_shape=jax.ShapeDtypeStruct((nb * seq, FOX_WIDTH), BF16),
        compiler_params=_cparams(("parallel", "parallel")),
        name="fox_sample",
    )(q, kb, vb, cache_k, cache_v, fcum)


def _hgrn_kernel(*refs, tb, has_state):
    if has_state:
        q_ref, g_ref, i_ref, og_ref, gn_ref, tri_ref, s0_ref, y_ref, sfin_ref, *st_sc, b_sc, o_sc = refs
    else:
        q_ref, g_ref, i_ref, og_ref, gn_ref, tri_ref, y_ref, sfin_ref, *st_sc, b_sc, o_sc = refs
        s0_ref = None
    step = pl.program_id(1)
    nstep = pl.num_programs(1)

    @pl.when(step == 0)
    def _init():
        for h in range(HGRN_HEADS):
            if has_state:
                st_sc[h][...] = s0_ref[0, h].T
            else:
                st_sc[h][...] = jnp.zeros((HGRN_DK, HGRN_DK), F32)

    g = g_ref[...]
    tri = tri_ref[...]
    b = None
    for part in _split3(g):
        pb = jnp.dot(tri, part, preferred_element_type=F32)
        b = pb if b is None else b + pb
    b_sc[...] = b

    trow = lax.broadcasted_iota(I32, (SUB, 1), 0)

    def sub_chunk(c, carry):
        r0 = pl.multiple_of(c * SUB, SUB)
        for h in range(HGRN_HEADS):
            cs = slice(h * HGRN_DK, (h + 1) * HGRN_DK)
            q = q_ref[pl.ds(r0, SUB), cs]
            gg = g_ref[pl.ds(r0, SUB), cs]
            iv = i_ref[pl.ds(r0, SUB), cs]
            bb = b_sc[pl.ds(r0, SUB), cs]
            kk = 1.0 - jnp.exp(gg)
            st = st_sc[h][...]
            o = lax.dot_general((q * jnp.exp(bb)).astype(BF16), st.astype(BF16), NT_DIMS,
                                preferred_element_type=F32)
            for s in range(SUB):
                lo = 0 if s < 8 else 8
                ks = kk[s:s + 1, :]
                bs = bb[s:s + 1, :]
                ivs = iv[s:s + 1, :]
                e = jnp.exp(jnp.where(trow[lo:] >= s, bb[lo:] - bs, NEG_BIG))
                a = jnp.sum(q[lo:] * ks * e, axis=-1, keepdims=True)
                upd = a * ivs
                if lo:
                    upd = jnp.concatenate([jnp.zeros((8, HGRN_DK), F32), upd], axis=0)
                o = o + upd
            o_sc[pl.ds(r0, SUB), cs] = o
            bl = bb[SUB - 1:SUB, :]
            kd = kk * jnp.exp(bl - bb)
            u = lax.dot_general(iv.astype(BF16), kd.astype(BF16), TN_DIMS, preferred_element_type=F32)
            st_sc[h][...] = st * jnp.exp(bl) + u
        return carry

    lax.fori_loop(0, tb // SUB, sub_chunk, 0, unroll=min(4, tb // SUB))

    for h in range(HGRN_HEADS):
        cs = slice(h * HGRN_DK, (h + 1) * HGRN_DK)
        o = o_sc[:, cs]
        ms = jnp.mean(o * o, axis=-1, keepdims=True)
        y_ref[:, cs] = (o * lax.rsqrt(ms + NORM_EPS) * gn_ref[...] * og_ref[:, cs]).astype(y_ref.dtype)

    @pl.when(step == nstep - 1)
    def _fin():
        for h in range(HGRN_HEADS):
            sfin_ref[0, h] = st_sc[h][...].T


def _hgrn(qs, gl, iv, og, gn, s0, nb, seq, tb):
    ns = seq // tb
    t = nb * seq
    row = pl.BlockSpec((tb, HGRN_WIDTH), lambda b, i: (b * ns + i, 0))
    r = jnp.arange(tb)
    tri = ((r[:, None] // SUB == r[None, :] // SUB) & (r[None, :] <= r[:, None])).astype(BF16)
    in_specs = [row, row, row, row, _const_spec((1, HGRN_DK)), _const_spec((tb, tb))]
    args = [qs, gl, iv, og, gn, tri]
    if s0 is not None:
        in_specs.append(pl.BlockSpec((1, HGRN_HEADS, HGRN_DK, HGRN_DK), lambda b, i: (b, 0, 0, 0)))
        args.append(s0)
    return pl.pallas_call(
        functools.partial(_hgrn_kernel, tb=tb, has_state=s0 is not None),
        grid=(nb, ns), in_specs=in_specs,
        out_specs=[row, pl.BlockSpec((1, HGRN_HEADS, HGRN_DK, HGRN_DK), lambda b, i: (b, 0, 0, 0))],
        out_shape=[jax.ShapeDtypeStruct((t, HGRN_WIDTH), BF16),
                   jax.ShapeDtypeStruct((nb, HGRN_HEADS, HGRN_DK, HGRN_DK), F32)],
        scratch_shapes=[pltpu.VMEM((HGRN_DK, HGRN_DK), F32)] * HGRN_HEADS
                       + [pltpu.VMEM((tb, HGRN_WIDTH), F32), pltpu.VMEM((tb, HGRN_WIDTH), F32)],
        compiler_params=_cparams(("parallel", "arbitrary")), name="hgrn",
    )(*args)


def _post_kernel(x_ref, ya_ref, yb_ref, ga_ref, gb_ref, g1_ref, sh2_ref, sc2_ref, gn2_ref,
                 wpa_ref, wpb_ref, wo_ref, wrh_ref, wrl_ref, br_ref, tri_ref, cin_ref,
                 x1_ref, h2_ref, idx_ref, wt_ref, rank_ref, cout_ref, cnt_sc, *, tm):
    first = jnp.logical_and(pl.program_id(0) == 0, pl.program_id(1) == 0)

    @pl.when(first)
    def _init():
        cnt_sc[...] = cin_ref[...]

    merged = (ga_ref[...].astype(F32) * jnp.dot(ya_ref[...], wpa_ref[...], preferred_element_type=F32)
              + gb_ref[...].astype(F32) * jnp.dot(yb_ref[...], wpb_ref[...], preferred_element_type=F32))
    x1 = x_ref[...] + g1_ref[0] * jnp.dot(merged.astype(BF16), wo_ref[...], preferred_element_type=F32)
    x1_ref[...] = x1
    ms = jnp.mean(x1 * x1, axis=-1, keepdims=True)
    h2 = x1 * lax.rsqrt(ms + NORM_EPS) * gn2_ref[...]
    h2 = h2 * (1.0 + sc2_ref[0]) + sh2_ref[0]
    _store_rows(h2_ref, h2)

    hh = h2.astype(BF16)
    hl = (h2 - hh.astype(F32)).astype(BF16)
    wrh = wrh_ref[...]
    logits = (lax.dot_general(wrh, hh, NT_DIMS, preferred_element_type=F32)
              + lax.dot_general(wrh, hl, NT_DIMS, preferred_element_type=F32)
              + lax.dot_general(wrl_ref[...], hh, NT_DIMS, preferred_element_type=F32))
    scores = _sigmoid(logits)
    biased = scores + br_ref[...]

    b3 = biased.reshape(N_GROUPS, GROUP_SIZE, tm)
    it3 = lax.broadcasted_iota(I32, (N_GROUPS, GROUP_SIZE, tm), 1).astype(F32)
    m1 = jnp.max(b3, axis=1, keepdims=True)
    i1 = jnp.min(jnp.where(b3 == m1, it3, float(GROUP_SIZE)), axis=1, keepdims=True)
    m2 = jnp.max(jnp.where(it3 == i1, -jnp.inf, b3), axis=1, keepdims=True)
    gs = (m1 + m2).reshape(N_GROUPS, tm)

    gi = lax.broadcasted_iota(I32, (N_GROUPS, tm), 0)
    beat = jnp.zeros((N_GROUPS, tm), F32)
    for g in range(N_GROUPS):
        r = gs[g:g + 1, :]
        beat = beat + jnp.where((r > gs) | ((r == gs) & (g < gi)), 1.0, 0.0)
    gpen = jnp.where(beat < TOPK_GROUPS, 0.0, -jnp.inf)
    masked = (b3 + gpen.reshape(N_GROUPS, 1, tm)).reshape(N_EXPERTS, tm)

    ei = lax.broadcasted_iota(I32, (N_EXPERTS, tm), 0).astype(F32)
    idx_rows, w_rows = [], []
    chosen = jnp.zeros((N_EXPERTS, tm), F32)
    for _ in range(TOP_K):
        m = jnp.max(masked, axis=0, keepdims=True)
        ik = jnp.min(jnp.where(masked == m, ei, float(N_EXPERTS)), axis=0, keepdims=True)
        hit = ei == ik
        w_rows.append(jnp.sum(jnp.where(hit, scores, 0.0), axis=0, keepdims=True))
        idx_rows.append(ik)
        masked = jnp.where(hit, -jnp.inf, masked)
        chosen = jnp.where(hit, 1.0, chosen)
    idx_ref[...] = jnp.concatenate(idx_rows, axis=0).astype(I32)
    wts = jnp.concatenate(w_rows, axis=0)
    wt_ref[...] = wts / jnp.sum(wts, axis=0, keepdims=True) * ROUTED_SCALE

    before = cnt_sc[...] + jnp.dot(chosen.astype(BF16), tri_ref[...], preferred_element_type=F32)
    rank_rows = [jnp.sum(jnp.where(ei == idx_rows[kk], before, 0.0), axis=0, keepdims=True)
                 for kk in range(TOP_K)]
    rank_ref[...] = jnp.concatenate(rank_rows, axis=0).astype(I32)
    cnt_sc[...] = cnt_sc[...] + jnp.sum(chosen, axis=1, keepdims=True)
    cout_ref[...] = cnt_sc[...]


def _post(x2, ya, yb, ga, gb, gate1, shift2, scale2, prm, count_in, nb, tm):
    t, d = x2.shape
    ns = t // nb // tm
    row = lambda w: pl.BlockSpec((tm, w), lambda b, i: (b * ns + i, 0))
    colb = pl.BlockSpec((TOP_K, tm), lambda b, i: (0, b * ns + i))
    r = jnp.arange(tm)
    tri = (r[:, None] < r[None, :]).astype(BF16)
    in_specs = [row(d), row(FOX_WIDTH), row(HGRN_WIDTH), row(d), row(d),
                _mod_spec(gate1, tm, ns), _mod_spec(shift2, tm, ns), _mod_spec(scale2, tm, ns),
                _const_spec((1, d)), _const_spec((FOX_WIDTH, d)), _const_spec((HGRN_WIDTH, d)),
                _const_spec((d, d)), _const_spec((N_EXPERTS, d)), _const_spec((N_EXPERTS, d)),
                _const_spec((N_EXPERTS, 1)), _const_spec((tm, tm)), _const_spec((N_EXPERTS, 1))]
    out_shape = [jax.ShapeDtypeStruct((t, d), F32), jax.ShapeDtypeStruct((t * ROW_CHUNKS, LANES), F32),
                 jax.ShapeDtypeStruct((TOP_K, t), I32), jax.ShapeDtypeStruct((TOP_K, t), F32),
                 jax.ShapeDtypeStruct((TOP_K, t), I32), jax.ShapeDtypeStruct((N_EXPERTS, 1), F32)]
    out_specs = [row(d), pl.BlockSpec((tm * ROW_CHUNKS, LANES), lambda b, i: (b * ns + i, 0)),
                 colb, colb, colb, pl.BlockSpec((N_EXPERTS, 1), lambda b, i: (0, 0))]
    return pl.pallas_call(
        functools.partial(_post_kernel, tm=tm), grid=(nb, ns), in_specs=in_specs,
        out_specs=out_specs, out_shape=out_shape,
        scratch_shapes=[pltpu.VMEM((N_EXPERTS, 1), F32)],
        compiler_params=_cparams(("arbitrary", "arbitrary")), name="post",
    )(x2, ya, yb, ga, gb, gate1, shift2, scale2, prm["g2"], prm["wpa"], prm["wpb"], prm["wo"],
      prm["wrh"], prm["wrl"], prm["br"], tri, count_in)


def _pos_kernel(idx_ref, rank_ref, pst_ref, pos_ref, *, tm):
    ei = lax.broadcasted_iota(I32, (N_EXPERTS, tm), 0)
    pst = pst_ref[...]
    rows = [jnp.sum(jnp.where(ei == idx_ref[kk:kk + 1, :], pst, 0.0), axis=0, keepdims=True)
            for kk in range(TOP_K)]
    pos_ref[...] = jnp.concatenate(rows, axis=0).astype(I32) + rank_ref[...]


def _pos(idx, rank, pstart_col, tm):
    t = idx.shape[1]
    colb = pl.BlockSpec((TOP_K, tm), lambda i: (0, i))
    return pl.pallas_call(
        functools.partial(_pos_kernel, tm=tm), grid=(t // tm,),
        in_specs=[colb, colb, _const_spec((N_EXPERTS, 1))], out_specs=colb,
        out_shape=jax.ShapeDtypeStruct((TOP_K, t), I32),
        compiler_params=_cparams(("parallel",)), name="moe_pos",
    )(idx, rank, pstart_col)


def _padfill_kernel(cnt_ref, pad_ref, pst_ref, xs_ref, zero_sc, sem):
    zero_sc[...] = jnp.zeros(zero_sc.shape, zero_sc.dtype)
    sizes = [1 << b for b in reversed(range(MOE_BLK.bit_length() - 1))]

    def chunks(e, act):
        n = pad_ref[e] - cnt_ref[e]
        row = pst_ref[e] + cnt_ref[e]
        for size in sizes:
            @pl.when((n & size) != 0)
            def _():
                act(pltpu.make_async_copy(zero_sc.at[pl.ds(0, size)], xs_ref.at[pl.ds(row, size)], sem))

            row = row + (n & size)

    def issue(e, c):
        chunks(e, lambda cp: cp.start())
        return c

    def drain(e, c):
        chunks(e, lambda cp: cp.wait())
        return c

    lax.fori_loop(0, N_EXPERTS, issue, 0)
    lax.fori_loop(0, N_EXPERTS, drain, 0)


def _padfill(counts, padded, pstart, n_rows, d):
    row_tile = (d // LANES, LANES)
    return pl.pallas_call(
        _padfill_kernel,
        grid_spec=pltpu.PrefetchScalarGridSpec(
            num_scalar_prefetch=3, grid=(1,), in_specs=[],
            out_specs=pl.BlockSpec(memory_space=pl.ANY),
            scratch_shapes=[pltpu.VMEM((MOE_BLK // 2,) + row_tile, F32), pltpu.SemaphoreType.DMA(())]),
        out_shape=jax.ShapeDtypeStruct((n_rows,) + row_tile, F32),
        compiler_params=pltpu.CompilerParams(dimension_semantics=("arbitrary",), has_side_effects=True),
        name="moe_padfill",
    )(counts, padded, pstart)


def _dispatch_kernel(pos_ref, h2_hbm, xs_in_ref, xs_ref, ring, load_sem, out_sem, *, tm):
    del xs_in_ref
    i = pl.program_id(0)
    last = pl.num_programs(0) - 1
    cur = i % 3
    par = i % 2

    def load(step, slot):
        return pltpu.make_async_copy(h2_hbm.at[pl.ds(step * tm, tm)], ring.at[slot], load_sem.at[slot])

    def drain(s):
        for kk in range(TOP_K):
            pltpu.make_async_copy(ring.at[0], xs_ref.at[pl.ds(0, tm)], out_sem.at[s]).wait()

    @pl.when(i == 0)
    def _():
        load(0, 0).start()

    @pl.when(i < last)
    def _():
        load(i + 1, (i + 1) % 3).start()

    load(i, cur).wait()
    for t in range(tm):
        for kk in range(TOP_K):
            pltpu.make_async_copy(ring.at[cur, t], xs_ref.at[pos_ref[kk, t]],
                                  out_sem.at[par]).start(priority=kk % 2)

    @pl.when(i > 0)
    def _():
        drain(1 - par)

    @pl.when(i == last)
    def _():
        drain(par)


def _dispatch(pos, h2_flat, xs, tm):
    h2 = h2_flat.reshape(-1, ROW_CHUNKS, LANES)
    t = h2.shape[0]
    return pl.pallas_call(
        functools.partial(_dispatch_kernel, tm=tm),
        grid=(t // tm,),
        in_specs=[pl.BlockSpec((TOP_K, tm), lambda i: (0, i), memory_space=pltpu.SMEM),
                  pl.BlockSpec(memory_space=pl.ANY),
                  pl.BlockSpec(memory_space=pl.ANY)],
        out_specs=pl.BlockSpec(memory_space=pl.ANY),
        out_shape=jax.ShapeDtypeStruct(xs.shape, xs.dtype),
        scratch_shapes=[pltpu.VMEM((3, tm) + h2.shape[1:], F32), pltpu.SemaphoreType.DMA((3,)),
                        pltpu.SemaphoreType.DMA((2,))],
        input_output_aliases={2: 0},
        compiler_params=pltpu.CompilerParams(dimension_semantics=("arbitrary",), has_side_effects=True,
                                             vmem_limit_bytes=VMEM_LIMIT),
        name="moe_dispatch",
    )(pos, h2, xs)


def _moe_kernel(be_ref, nu_ref, first_ref, slot_ref, nxt_ref, x_ref, wg_ref, wu_ref, wd_ref, y_ref,
                wg_buf, wu_buf, wd_buf, wgu_sc, wd_sc, sem):
    i = pl.program_id(0)
    e = be_ref[i]

    def fetch(expert, slot):
        return (pltpu.make_async_copy(wg_ref.at[expert], wg_buf.at[slot], sem.at[slot]),
                pltpu.make_async_copy(wu_ref.at[expert], wu_buf.at[slot], sem.at[slot]),
                pltpu.make_async_copy(wd_ref.at[expert], wd_buf.at[slot], sem.at[slot]))

    @pl.when(first_ref[i] == 1)
    def _switch_expert():
        slot = slot_ref[i]

        @pl.when(i == 0)
        def _():
            for cp in fetch(e, slot):
                cp.start()

        for cp in fetch(e, slot):
            cp.wait()
        nxt = nxt_ref[e]

        @pl.when(nxt >= 0)
        def _():
            for cp in fetch(nxt, 1 - slot):
                cp.start()

        wgu_sc[:, 0:D_EXPERT] = wg_buf[slot].astype(BF16)
        wgu_sc[:, D_EXPERT:2 * D_EXPERT] = wu_buf[slot].astype(BF16)
        wd_sc[...] = wd_buf[slot].astype(BF16)

    @pl.when(i < nu_ref[0])
    def _compute():
        gu = jnp.dot(_load_rows(x_ref).astype(BF16), wgu_sc[...], preferred_element_type=F32)
        gt = gu[:, 0:D_EXPERT]
        h = gt * _sigmoid(gt) * gu[:, D_EXPERT:2 * D_EXPERT]
        _store_rows(y_ref, jnp.dot(h.astype(BF16), wd_sc[...], preferred_element_type=F32))


def _moe(blk_expert, n_used, first, slot, nxt, xs_pool, w_gate, w_up, w_down):
    n_rows = xs_pool.shape[0]
    xs = xs_pool.reshape(n_rows * ROW_CHUNKS, LANES)
    d = w_gate.shape[1]
    nblk = n_rows // MOE_BLK
    xmap = lambda i, be, nu, fi, sl, nx: (jnp.minimum(i, nu[0] - 1), 0)
    hbm = pl.BlockSpec(memory_space=pl.ANY)
    return pl.pallas_call(
        _moe_kernel,
        grid_spec=pltpu.PrefetchScalarGridSpec(
            num_scalar_prefetch=5, grid=(nblk,),
            in_specs=[pl.BlockSpec((MOE_BLK * ROW_CHUNKS, LANES), xmap), hbm, hbm, hbm],
            out_specs=pl.BlockSpec((MOE_BLK * ROW_CHUNKS, LANES), xmap),
            scratch_shapes=[pltpu.VMEM((2, d, D_EXPERT), F32), pltpu.VMEM((2, d, D_EXPERT), F32),
                            pltpu.VMEM((2, D_EXPERT, d), F32),
                            pltpu.VMEM((d, 2 * D_EXPERT), BF16), pltpu.VMEM((D_EXPERT, d), BF16),
                            pltpu.SemaphoreType.DMA((2,))]),
        out_shape=jax.ShapeDtypeStruct(xs.shape, F32),
        compiler_params=_cparams(("arbitrary",)), name="moe_experts",
    )(blk_expert, n_used, first, slot, nxt, xs, w_gate, w_up, w_down)


def _combine_kernel(pos_ref, posn_ref, x1_ref, h2_ref, g2_ref, wt_ref, wsgu_ref, wsd_ref, ys_ref, ysflat_ref,
                    o_ref, buf, sem, *, tm):
    lin = pl.program_id(0) * pl.num_programs(1) + pl.program_id(1)
    total = pl.num_programs(0) * pl.num_programs(1)
    slot = lin % 2

    def issue(p_ref, s):
        for t in range(tm):
            for kk in range(TOP_K):
                pltpu.make_async_copy(ys_ref.at[p_ref[kk, t]], buf.at[s, kk, pl.ds(t * ROW_CHUNKS, ROW_CHUNKS)],
                                      sem.at[s]).start(priority=kk % 2)

    @pl.when(lin == 0)
    def _():
        issue(pos_ref, 0)

    gu = jnp.dot(_load_rows(h2_ref).astype(BF16), wsgu_ref[...], preferred_element_type=F32)
    gt = gu[:, 0:D_SHARED]
    hs = gt * _sigmoid(gt) * gu[:, D_SHARED:2 * D_SHARED]
    shared = jnp.dot(hs.astype(BF16), wsd_ref[...], preferred_element_type=F32)
    wcols = [jnp.broadcast_to(wt_ref[kk:kk + 1, :], (LANES, tm)).T for kk in range(TOP_K)]

    issue(posn_ref, 1 - slot)

    for kk in range(TOP_K):
        pltpu.make_async_copy(ysflat_ref.at[pl.ds(0, tm * ROW_CHUNKS)], buf.at[slot, kk], sem.at[slot]).wait()

    nchunk = shared.shape[-1] // LANES
    acc = [shared[:, c * LANES:(c + 1) * LANES] for c in range(nchunk)]
    for kk in range(TOP_K):
        for c in range(nchunk):
            acc[c] = acc[c] + _row_chunk(buf.at[slot, kk], c)[...] * wcols[kk]
    o_ref[...] = x1_ref[...] + g2_ref[0] * jnp.concatenate(acc, axis=1)

    @pl.when(lin + 1 == total)
    def _():
        for kk in range(TOP_K):
            pltpu.make_async_copy(ysflat_ref.at[pl.ds(0, tm * ROW_CHUNKS)], buf.at[1 - slot, kk],
                                  sem.at[1 - slot]).wait()


def _combine(pos, x1, h2, gate2, wts, wsgu, wsd, ys_flat, nb, tm):
    t, d = x1.shape
    ns = t // nb // tm
    row = pl.BlockSpec((tm, d), lambda b, i: (b * ns + i, 0))
    last = nb * ns - 1
    ys = ys_flat.reshape(-1, ROW_CHUNKS, LANES)
    return pl.pallas_call(
        functools.partial(_combine_kernel, tm=tm),
        grid=(nb, ns),
        in_specs=[pl.BlockSpec((TOP_K, tm), lambda b, i: (0, b * ns + i), memory_space=pltpu.SMEM),
                  pl.BlockSpec((TOP_K, tm), lambda b, i: (0, jnp.minimum(b * ns + i + 1, last)),
                               memory_space=pltpu.SMEM),
                  row, pl.BlockSpec((tm * ROW_CHUNKS, LANES), lambda b, i: (b * ns + i, 0)),
                  _mod_spec(gate2, tm, ns),
                  pl.BlockSpec((TOP_K, tm), lambda b, i: (0, b * ns + i)),
                  _const_spec(wsgu.shape), _const_spec(wsd.shape),
                  pl.BlockSpec(memory_space=pl.ANY), pl.BlockSpec(memory_space=pl.ANY)],
        out_specs=row,
        out_shape=jax.ShapeDtypeStruct((t, d), F32),
        scratch_shapes=[pltpu.VMEM((2, TOP_K, tm * ROW_CHUNKS, LANES), F32), pltpu.SemaphoreType.DMA((2,))],
        compiler_params=_cparams(("arbitrary", "arbitrary")), name="moe_combine",
    )(pos, pos, x1, h2, gate2, wts, wsgu, wsd, ys, ys_flat)


def _prepare_params(g_norm1, w_in, b_fox_f, g_q, g_k, hgrn_lb, g_hgrn_o, w_proj_a, w_proj_b, w_out,
                    g_norm2, w_router, b_router, w_sh_gate, w_sh_up, w_sh_down):
    d = D_MODEL
    fw, hw = FOX_WIDTH, HGRN_WIDTH
    c0 = 3 * fw
    c1 = c0 + FOX_HEADS
    c2 = c1 + 4 * hw
    wff = w_in[:, c0:c1]
    head = jnp.arange(fw) // FOX_HEAD_DIM
    wr_t = w_router.T
    wrh = wr_t.astype(BF16)
    return dict(
        g1=g_norm1.reshape(1, d),
        wqkv=w_in[:, :c0].astype(BF16),
        wff=jnp.pad(wff, ((0, 0), (0, LANES - FOX_HEADS))).astype(BF16),
        wfft=jnp.pad(wff.T, ((0, 16 - FOX_HEADS), (0, 0))).astype(BF16),
        bf=b_fox_f.reshape(1, FOX_HEADS), bft=b_fox_f.reshape(FOX_HEADS, 1),
        gq=jnp.tile(g_q, FOX_HEADS).reshape(1, fw), gk=jnp.tile(g_k, FOX_HEADS).reshape(1, fw),
        bd=(head[:, None] == head[None, :]).astype(BF16) * (1.0 / FOX_HEAD_DIM),
        lbp=hgrn_lb,
        wh=w_in[:, c1:c2].astype(BF16), wg=w_in[:, c2:].astype(BF16),
        gn=g_hgrn_o.reshape(1, HGRN_DK),
        g2=g_norm2.reshape(1, d),
        wpa=w_proj_a.astype(BF16), wpb=w_proj_b.astype(BF16), wo=w_out.astype(BF16),
        wrh=wrh, wrl=(wr_t - wrh.astype(F32)).astype(BF16), br=b_router.reshape(N_EXPERTS, 1),
        wsgu=jnp.concatenate([w_sh_gate, w_sh_up], axis=1).astype(BF16), wsd=w_sh_down.astype(BF16),
    )


def _mixers(x2, nb, seq, tm, shift1, scale1, prm, past, tile, tb):
    (q, k, kb, v, vb, lf, lft, hq, hl, hi, og, ga, gb) = _inproj(x2, nb if shift1.shape[1] == 1 else 1, tm,
                                                                 shift1, scale1, prm)
    t = nb * seq
    hp = FOX_HEADS // 2
    lft = lft.transpose(1, 0, 2).reshape(FOX_HEADS, nb, seq).transpose(1, 0, 2)
    if past is None:
        fcum = _cumsum_lanes(lft.reshape(nb * FOX_HEADS, seq)).reshape(nb, hp, 2, seq)
        ya = _fox_prompt(q, kb, vb, fcum, nb, seq, tile, min(32, tile))
        s0 = None
    else:
        cache_k, cache_v, cache_lf, s0 = past
        plen = cache_k.shape[1]
        ltot = plen + seq
        lpad = -(-ltot // LANES) * LANES
        lf_all = jnp.concatenate([cache_lf.transpose(0, 2, 1), lft,
                                  jnp.zeros((nb, FOX_HEADS, lpad - ltot), F32)], axis=-1)
        fcum = _cumsum_lanes(lf_all.reshape(nb * FOX_HEADS, lpad)).reshape(nb, hp, 2, lpad)
        ya = _fox_sample(q, kb, vb, cache_k.reshape(nb, plen, FOX_WIDTH), cache_v.reshape(nb, plen, FOX_WIDTH),
                         fcum, nb, seq, plen)
    yb, sfin = _hgrn(hq, hl, hi, og, prm["gn"], s0, nb, seq, tb)
    return k, v, lf, sfin, ya, yb, ga, gb


def kernel(x_prompt, x_sample, cache_fox_k, cache_fox_v, cache_fox_logf, state_hgrn, c_prompt, c_sample,
           w_ada, b_ada, g_norm1, w_in, b_fox_f, g_q, g_k, hgrn_lb, g_hgrn_o, w_proj_a, w_proj_b, w_out,
           g_norm2, w_router, b_router, w_exp_gate, w_exp_up, w_exp_down, w_sh_gate, w_sh_up, w_sh_down):
    assert w_ada.shape[0] == 1 and hgrn_lb.shape[0] == 2, "single-layer trunk"
    d = D_MODEL
    bp, sp, _ = x_prompt.shape
    bs, ss, _ = x_sample.shape
    tp, ts = bp * sp, bs * ss
    prm = _prepare_params(g_norm1[0], w_in[0], b_fox_f[0], g_q[0], g_k[0], hgrn_lb, g_hgrn_o[0],
                          w_proj_a[0], w_proj_b[0], w_out[0], g_norm2[0], w_router[0], b_router[0],
                          w_sh_gate[0], w_sh_up[0], w_sh_down[0])

    bc = bp + bs
    bc_pad = -(-bc // 8) * 8
    c_all = jnp.concatenate([c_prompt, c_sample, jnp.zeros((bc_pad - bc, d), F32)], axis=0)
    mod = _ada(c_all, w_ada[0], b_ada[0])
    mod_p = [mod[:bp, j * d:(j + 1) * d].reshape(bp, 1, d) for j in range(6)]
    mod_s = [jnp.repeat(mod[bp:bc, j * d:(j + 1) * d], ss, axis=0).reshape(1, ts, d) for j in range(6)]

    tm_p = min(512, sp)
    tm_s = min(256, ts)
    xp2 = x_prompt.reshape(tp, d)
    xs2 = x_sample.reshape(ts, d)
    tile = min(1024, sp)
    tb = min(256, sp)

    kp, vp, lfp, sfin_p, ya_p, yb_p, ga_p, gb_p = _mixers(
        xp2, bp, sp, tm_p, mod_p[0], mod_p[1], prm, None, tile, tb)
    past = (cache_fox_k[0], cache_fox_v[0], cache_fox_logf[0], state_hgrn[0])
    ks, vs, lfs, sfin_s, ya_s, yb_s, ga_s, gb_s = _mixers(
        xs2, bs, ss, tm_s, mod_s[0], mod_s[1], prm, past, None, min(256, ss))

    zero_cnt = jnp.zeros((N_EXPERTS, 1), F32)
    x1_p, h2_p, idx_p, wt_p, rank_p, cnt_p = _post(xp2, ya_p, yb_p, ga_p, gb_p, mod_p[2], mod_p[3], mod_p[4],
                                                    prm, zero_cnt, bp, tm_p)
    x1_s, h2_s, idx_s, wt_s, rank_s, cnt_all = _post(xs2, ya_s, yb_s, ga_s, gb_s, mod_s[2], mod_s[3], mod_s[4],
                                                     prm, cnt_p, 1, tm_s)

    counts = cnt_all.reshape(N_EXPERTS).astype(I32)
    padded = (counts + MOE_BLK - 1) // MOE_BLK * MOE_BLK
    pend = jnp.cumsum(padded)
    pstart = pend - padded
    nblk = -(-((tp + ts) * TOP_K + N_EXPERTS * (MOE_BLK - 1)) // MOE_BLK)
    n_used = (pend[-1] // MOE_BLK).reshape(1)
    blk_row0 = jnp.arange(nblk, dtype=I32) * MOE_BLK
    blk_expert = jnp.minimum(jnp.sum((pend[None, :] <= blk_row0[:, None]).astype(I32), axis=1), N_EXPERTS - 1)
    blk_used = jnp.arange(nblk) < n_used[0]
    blk_expert = jnp.where(blk_used, blk_expert, blk_expert[jnp.maximum(n_used[0] - 1, 0)])
    blk_first = blk_used & jnp.concatenate([jnp.ones((1,), bool), blk_expert[1:] != blk_expert[:-1]])
    blk_slot = ((jnp.cumsum(blk_first.astype(I32)) - 1) & 1).astype(I32)
    eids = jnp.arange(N_EXPERTS, dtype=I32)
    later = (eids[None, :] > eids[:, None]) & (padded[None, :] > 0)
    nxt_expert = jnp.min(jnp.where(later, eids[None, :], N_EXPERTS), axis=1)
    nxt_expert = jnp.where(nxt_expert < N_EXPERTS, nxt_expert, -1).astype(I32)
    pstart_col = pstart.astype(F32).reshape(N_EXPERTS, 1)
    pos_p = _pos(idx_p, rank_p, pstart_col, tm_p)
    pos_s = _pos(idx_s, rank_s, pstart_col, tm_s)

    xs_pool = _padfill(counts, padded, pstart, nblk * MOE_BLK, d)
    xs_pool = _dispatch(pos_p, h2_p, xs_pool, min(128, tm_p))
    xs_pool = _dispatch(pos_s, h2_s, xs_pool, min(128, tm_s))
    ys_pool = _moe(blk_expert, n_used, blk_first.astype(I32), blk_slot, nxt_expert, xs_pool,
                   w_exp_gate[0], w_exp_up[0], w_exp_down[0])

    tm_c = min(128, sp)
    y_p = _combine(pos_p, x1_p, h2_p, mod_p[5], wt_p, prm["wsgu"], prm["wsd"], ys_pool, bp, tm_c)
    y_s = _combine(pos_s, x1_s, h2_s, mod_s[5], wt_s, prm["wsgu"], prm["wsd"], ys_pool, 1, min(128, ts))

    return (y_p.reshape(bp, sp, d), y_s.reshape(bs, ss, d),
            kp.reshape(1, bp, sp, FOX_HEADS, FOX_HEAD_DIM), vp.reshape(1, bp, sp, FOX_HEADS, FOX_HEAD_DIM),
            lfp.reshape(1, bp, sp, FOX_HEADS), sfin_p[None],
            ks.reshape(1, bs, ss, FOX_HEADS, FOX_HEAD_DIM), vs.reshape(1, bs, ss, FOX_HEADS, FOX_HEAD_DIM),
            lfs.reshape(1, bs, ss, FOX_HEADS), sfin_s[None])
```

```python
import functools

import jax
import jax.numpy as jnp
from jax import lax
from jax.experimental import pallas as pl
from jax.experimental.pallas import tpu as pltpu

F32 = jnp.float32
BF16 = jnp.bfloat16
I32 = jnp.int32

D_MODEL = 1024
FOX_HEADS = 8
FOX_HEAD_DIM = 64
FOX_WIDTH = FOX_HEADS * FOX_HEAD_DIM
HGRN_HEADS = 4
HGRN_DK = 128
HGRN_WIDTH = HGRN_HEADS * HGRN_DK
N_EXPERTS = 256
TOP_K = 8
N_GROUPS = 8
TOPK_GROUPS = 4
GROUP_SIZE = N_EXPERTS // N_GROUPS
D_EXPERT = 256
D_SHARED = 256
ROUTED_SCALE = 2.5
NORM_EPS = 1e-6
NEG_BIG = -1e30
LOG2E = 1.4426950408889634
QK_SCALE = FOX_HEAD_DIM ** -0.5 * LOG2E

LANES = 128
SUB = 16
MOE_BLK = 512
VMEM_LIMIT = 56 * 1024 * 1024

NT_DIMS = (((1,), (1,)), ((), ()))
TN_DIMS = (((0,), (0,)), ((), ()))


def _cparams(sem):
    return pltpu.CompilerParams(dimension_semantics=sem, vmem_limit_bytes=VMEM_LIMIT)


def _const_spec(shape):
    nd = len(shape)
    return pl.BlockSpec(shape, lambda *_: (0,) * nd, pipeline_mode=pl.Buffered(1))


def _sigmoid(z):
    return 1.0 / (1.0 + jnp.exp(-z))


def _log_sigmoid(z):
    return jnp.minimum(z, 0.0) - jnp.log(1.0 + jnp.exp(-jnp.abs(z)))


ROW_CHUNKS = D_MODEL // LANES


def _row_chunk(ref, c):
    return ref.at[pl.ds(c, ref.shape[0] // ROW_CHUNKS, stride=ROW_CHUNKS), :]


def _store_rows(ref, val):
    for c in range(ROW_CHUNKS):
        _row_chunk(ref, c)[...] = val[:, c * LANES:(c + 1) * LANES]


def _load_rows(ref):
    return jnp.concatenate([_row_chunk(ref, c)[...] for c in range(ROW_CHUNKS)], axis=1)


def _split3(a):
    hi = a.astype(BF16)
    r1 = a - hi.astype(F32)
    mid = r1.astype(BF16)
    lo = (r1 - mid.astype(F32)).astype(BF16)
    return hi, mid, lo


def _ada_kernel(c_ref, w_ref, b_ref, o_ref):
    c = c_ref[...]
    s = c * _sigmoid(c)
    o_ref[...] = jnp.dot(s.astype(BF16), w_ref[...].astype(BF16), preferred_element_type=F32) + b_ref[...]


def _ada(c_all, w_ada, b_ada):
    bc, d = c_all.shape
    n = w_ada.shape[1]
    tn = 1024
    return pl.pallas_call(
        _ada_kernel,
        grid=(n // tn,),
        in_specs=[pl.BlockSpec((bc, d), lambda j: (0, 0)),
                  pl.BlockSpec((d, tn), lambda j: (0, j)),
                  pl.BlockSpec((1, tn), lambda j: (0, j))],
        out_specs=pl.BlockSpec((bc, tn), lambda j: (0, j)),
        out_shape=jax.ShapeDtypeStruct((bc, n), F32),
        compiler_params=_cparams(("parallel",)),
        name="ada",
    )(c_all, w_ada, b_ada.reshape(1, n))


def _inproj_kernel(x_ref, sh_ref, sc_ref, g1_ref, wqkv_ref, wff_ref, wfft_ref, bf_ref, bft_ref,
                   gq_ref, gk_ref, bd_ref, lbp_ref, wh_ref, wg_ref,
                   q_ref, k_ref, kb_ref, v_ref, vb_ref, lf_ref, lft_ref,
                   hq_ref, hl_ref, hi_ref, og_ref, ga_ref, gb_ref):
    x = x_ref[...]
    ms = jnp.mean(x * x, axis=-1, keepdims=True)
    h = x * lax.rsqrt(ms + NORM_EPS) * g1_ref[...]
    h = h * (1.0 + sc_ref[0]) + sh_ref[0]
    hb = h.astype(BF16)

    def headnorm(a, g):
        ss = jnp.dot((a * a).astype(BF16), bd_ref[...], preferred_element_type=F32)
        return a * lax.rsqrt(ss + NORM_EPS) * g

    fq = jnp.dot(hb, wqkv_ref[:, 0:FOX_WIDTH], preferred_element_type=F32)
    q_ref[...] = (headnorm(fq, gq_ref[...]) * QK_SCALE).astype(BF16)
    fk = jnp.dot(hb, wqkv_ref[:, FOX_WIDTH:2 * FOX_WIDTH], preferred_element_type=F32)
    k = headnorm(fk, gk_ref[...])
    kb_ref[...] = k.astype(BF16)
    fv = jnp.dot(hb, wqkv_ref[:, 2 * FOX_WIDTH:3 * FOX_WIDTH], preferred_element_type=F32)
    vb_ref[...] = fv.astype(BF16)
    k_ref[...] = k.reshape(k_ref.shape)
    v_ref[...] = fv.reshape(v_ref.shape)

    ff = jnp.dot(hb, wff_ref[...], preferred_element_type=F32)
    lf_ref[...] = _log_sigmoid(ff[:, 0:FOX_HEADS] + bf_ref[...])
    fft = lax.dot_general(wfft_ref[...], hb, NT_DIMS, preferred_element_type=F32)
    lft_ref[0] = _log_sigmoid(fft[0:FOX_HEADS, :] + bft_ref[...])

    lbp = lbp_ref[...]
    e = jnp.exp(lbp - jnp.max(lbp, axis=0, keepdims=True))
    lb = e[0:1, :] / jnp.sum(e, axis=0, keepdims=True)

    w = HGRN_WIDTH
    hq = jnp.dot(hb, wh_ref[:, 0:w], preferred_element_type=F32)
    hq_ref[...] = hq * _sigmoid(hq)
    hf = jnp.dot(hb, wh_ref[:, w:2 * w], preferred_element_type=F32)
    hl_ref[...] = jnp.log(lb + (1.0 - lb) * _sigmoid(hf))
    hi_ref[...] = jnp.dot(hb, wh_ref[:, 2 * w:3 * w], preferred_element_type=F32)
    hg = jnp.dot(hb, wh_ref[:, 3 * w:4 * w], preferred_element_type=F32)
    og_ref[...] = _sigmoid(hg)
    ga = jnp.dot(hb, wg_ref[:, 0:D_MODEL], preferred_element_type=F32)
    ga_ref[...] = _sigmoid(ga).astype(BF16)
    gb = jnp.dot(hb, wg_ref[:, D_MODEL:2 * D_MODEL], preferred_element_type=F32)
    gb_ref[...] = _sigmoid(gb).astype(BF16)


def _mod_spec(arr, tm, ns):
    d = arr.shape[-1]
    if arr.shape[1] == 1:
        return pl.BlockSpec((1, 1, d), lambda b, i: (b, 0, 0))
    return pl.BlockSpec((1, tm, d), lambda b, i: (0, b * ns + i, 0))


def _inproj(x2, nb, tm, shift1, scale1, prm):
    t, d = x2.shape
    ns = t // nb // tm
    row = lambda w: pl.BlockSpec((tm, w), lambda b, i: (b * ns + i, 0))
    in_specs = [row(d), _mod_spec(shift1, tm, ns), _mod_spec(scale1, tm, ns),
                _const_spec((1, d)), _const_spec(prm["wqkv"].shape), _const_spec(prm["wff"].shape),
                _const_spec(prm["wfft"].shape), _const_spec((1, FOX_HEADS)), _const_spec((FOX_HEADS, 1)),
                _const_spec((1, FOX_WIDTH)), _const_spec((1, FOX_WIDTH)), _const_spec((FOX_WIDTH, FOX_WIDTH)),
                _const_spec(prm["lbp"].shape), _const_spec(prm["wh"].shape), _const_spec(prm["wg"].shape)]
    fw, hw = FOX_WIDTH, HGRN_WIDTH
    heads = pl.BlockSpec((tm, FOX_HEADS, FOX_HEAD_DIM), lambda b, i: (b * ns + i, 0, 0))
    out_shape = [jax.ShapeDtypeStruct((t, fw), BF16),
                 jax.ShapeDtypeStruct((t, FOX_HEADS, FOX_HEAD_DIM), F32),
                 jax.ShapeDtypeStruct((t, fw), BF16),
                 jax.ShapeDtypeStruct((t, FOX_HEADS, FOX_HEAD_DIM), F32),
                 jax.ShapeDtypeStruct((t, fw), BF16),
                 jax.ShapeDtypeStruct((t, FOX_HEADS), F32),
                 jax.ShapeDtypeStruct((nb * ns, FOX_HEADS, tm), F32),
                 jax.ShapeDtypeStruct((t, hw), F32),
                 jax.ShapeDtypeStruct((t, hw), F32),
                 jax.ShapeDtypeStruct((t, hw), F32),
                 jax.ShapeDtypeStruct((t, hw), F32),
                 jax.ShapeDtypeStruct((t, d), BF16),
                 jax.ShapeDtypeStruct((t, d), BF16)]
    out_specs = [row(fw), heads, row(fw), heads, row(fw), row(FOX_HEADS),
                 pl.BlockSpec((1, FOX_HEADS, tm), lambda b, i: (b * ns + i, 0, 0)),
                 row(hw), row(hw), row(hw), row(hw), row(d), row(d)]
    return pl.pallas_call(
        _inproj_kernel, grid=(nb, ns), in_specs=in_specs, out_specs=out_specs, out_shape=out_shape,
        compiler_params=_cparams(("parallel", "parallel")), name="inproj",
    )(x2, shift1, scale1, prm["g1"], prm["wqkv"], prm["wff"], prm["wfft"], prm["bf"], prm["bft"],
      prm["gq"], prm["gk"], prm["bd"], prm["lbp"], prm["wh"], prm["wg"])


def _cumsum_kernel(x_ref, o_ref):
    x = x_ref[...]
    n = x.shape[-1]
    lane = lax.broadcasted_iota(I32, x.shape, 1)
    s = 1
    while s < n:
        x = x + jnp.where(lane >= s, pltpu.roll(x, s, axis=1), 0.0)
        s *= 2
    o_ref[...] = x * LOG2E


def _cumsum_lanes(x):
    return pl.pallas_call(
        _cumsum_kernel, out_shape=jax.ShapeDtypeStruct(x.shape, F32),
        compiler_params=pltpu.CompilerParams(vmem_limit_bytes=VMEM_LIMIT), name="cumsum",
    )(x)


FOX_PAIRS_PER_STEP = 2


def _fox_prompt_kernel(qt_ref, kt_ref, q_ref, k_ref, v_ref, fk_ref, o_ref, *scratch, tile, rsub):
    step = pl.program_id(2)
    qi = qt_ref[step]
    ki = kt_ref[step]
    nheads = 2 * FOX_PAIRS_PER_STEP
    heads = [scratch[4 * h:4 * h + 4] for h in range(nheads)]
    lane = lax.broadcasted_iota(I32, (1, LANES), 1)
    lo_half = lane < FOX_HEAD_DIM

    @pl.when(ki == 0)
    def _init():
        for m_sc, a_sc, _, _ in heads:
            m_sc[...] = jnp.full(m_sc.shape, NEG_BIG, F32)
            a_sc[...] = jnp.zeros(a_sc.shape, F32)

    def tile_update(diagonal):
        nsub = tile // rsub
        if diagonal:
            ahead = (lax.broadcasted_iota(I32, (rsub, tile), 1)
                     - lax.broadcasted_iota(I32, (rsub, tile), 0))
        for h, (_, _, s_sc, _) in enumerate(heads):
            lanes = slice((h // 2) * LANES, (h // 2 + 1) * LANES)
            q = q_ref[:, lanes]
            sel = lo_half if h % 2 == 0 else jnp.logical_not(lo_half)
            qh = jnp.where(sel, q, jnp.zeros_like(q))
            s_sc[...] = lax.dot_general(qh, k_ref[:, lanes], NT_DIMS, preferred_element_type=F32)
        for h, (m_sc, a_sc, s_sc, p_sc) in enumerate(heads):
            lanes = slice((h // 2) * LANES, (h // 2 + 1) * LANES)
            v = v_ref[:, lanes]
            fk = fk_ref[0, h // 2]
            hs = h % 2
            sel = lo_half if hs == 0 else jnp.logical_not(lo_half)
            v_ones = jnp.where(sel, v, jnp.ones_like(v))

            def biased(r):
                rows = slice(r * rsub, (r + 1) * rsub)
                cw = min(tile, -(-((r + 1) * rsub) // LANES) * LANES) if diagonal else tile
                s = s_sc[rows, 0:cw] - fk[hs:hs + 1, 0:cw]
                if diagonal:
                    s = jnp.where(ahead[:, 0:cw] <= r * rsub, s, NEG_BIG)
                return s, rows, cw

            m_old = m_sc[...]
            m_parts = []
            for r in range(nsub):
                s, rows, _ = biased(r)
                m_parts.append(jnp.maximum(m_old[rows], jnp.max(s, axis=1, keepdims=True)))
            for r in range(nsub):
                s, rows, cw = biased(r)
                nc = cw // LANES
                p = jnp.exp2(s - jnp.concatenate([m_parts[r]] * nc, axis=1))
                p_sc[rows, 0:cw] = p.astype(BF16)
                if cw < tile:
                    p_sc[rows, cw:tile] = jnp.zeros((rsub, tile - cw), BF16)
            m_new = jnp.concatenate(m_parts, axis=0)
            alpha = jnp.exp2(m_old - m_new)
            m_sc[...] = m_new
            a_sc[...] = alpha * a_sc[...] + jnp.dot(p_sc[...], v_ones, preferred_element_type=F32)

    @pl.when(ki < qi)
    def _full():
        tile_update(False)

    @pl.when(ki == qi)
    def _diag():
        tile_update(True)
        for pair in range(FOX_PAIRS_PER_STEP):
            acc0 = heads[2 * pair][1][...]
            acc1 = heads[2 * pair + 1][1][...]
            o0 = acc0 / pltpu.roll(acc0, FOX_HEAD_DIM, axis=1)
            o1 = acc1 / pltpu.roll(acc1, FOX_HEAD_DIM, axis=1)
            o_ref[:, pair * LANES:(pair + 1) * LANES] = jnp.where(lo_half, o0, o1).astype(o_ref.dtype)


def _fox_prompt(q, kb, vb, fcum, nb, seq, tile, rsub):
    nt = seq // tile
    pps = FOX_PAIRS_PER_STEP
    hp = FOX_HEADS // 2 // pps
    width = pps * LANES
    qt = jnp.asarray([qi for qi in range(nt) for _ in range(qi + 1)], I32)
    kt = jnp.asarray([ki for qi in range(nt) for ki in range(qi + 1)], I32)
    qmap = lambda b, h, s, qt, kt: (b * nt + qt[s], h)
    kmap = lambda b, h, s, qt, kt: (b * nt + kt[s], h)
    fmap = lambda b, h, s, qt, kt: (b, h, 0, kt[s])
    per_head = [pltpu.VMEM((tile, LANES), F32), pltpu.VMEM((tile, LANES), F32),
                pltpu.VMEM((tile, tile), F32), pltpu.VMEM((tile, tile), BF16)]
    return pl.pallas_call(
        functools.partial(_fox_prompt_kernel, tile=tile, rsub=rsub),
        grid_spec=pltpu.PrefetchScalarGridSpec(
            num_scalar_prefetch=2, grid=(nb, hp, nt * (nt + 1) // 2),
            in_specs=[pl.BlockSpec((tile, width), qmap), pl.BlockSpec((tile, width), kmap),
                      pl.BlockSpec((tile, width), kmap), pl.BlockSpec((1, pps, 2, tile), fmap)],
            out_specs=pl.BlockSpec((tile, width), qmap),
            scratch_shapes=per_head * (2 * pps)),
        out_shape=jax.ShapeDtypeStruct((nb * seq, FOX_WIDTH), BF16),
        compiler_params=_cparams(("parallel", "parallel", "arbitrary")),
        name="fox_prompt",
    )(qt, kt, q, kb, vb, fcum)


def _fox_sample_kernel(q_ref, kn_ref, vn_ref, kp_ref, vp_ref, fk_ref, o_ref, *, past, seq):
    lane = lax.broadcasted_iota(I32, (1, LANES), 1)
    lo_half = lane < FOX_HEAD_DIM
    row = lax.broadcasted_iota(I32, (seq, seq), 0)
    col = lax.broadcasted_iota(I32, (seq, seq), 1)
    causal = col <= row
    for pair in range(FOX_HEADS // 2):
        lanes = slice(pair * LANES, (pair + 1) * LANES)
        q = q_ref[:, lanes]
        kn = kn_ref[:, lanes]
        vn = vn_ref[:, lanes]
        kp = kp_ref[0, :, lanes].astype(BF16)
        vp = vp_ref[0, :, lanes].astype(BF16)
        fk = fk_ref[0, pair]
        outs = []
        for hs in range(2):
            sel = lo_half if hs == 0 else jnp.logical_not(lo_half)
            qh = jnp.where(sel, q, jnp.zeros_like(q))
            sp = lax.dot_general(qh, kp, NT_DIMS, preferred_element_type=F32) - fk[hs:hs + 1, 0:past]
            sn = lax.dot_general(qh, kn, NT_DIMS, preferred_element_type=F32) - fk[hs:hs + 1, past:past + seq]
            sn = jnp.where(causal, sn, NEG_BIG)
            m = jnp.maximum(jnp.max(sp, axis=1, keepdims=True), jnp.max(sn, axis=1, keepdims=True))
            pp = jnp.exp2(sp - m)
            pn = jnp.exp2(sn - m)
            l = jnp.sum(pp, axis=1, keepdims=True) + jnp.sum(pn, axis=1, keepdims=True)
            o = (jnp.dot(pp.astype(BF16), vp, preferred_element_type=F32)
                 + jnp.dot(pn.astype(BF16), vn, preferred_element_type=F32))
            outs.append(o / l)
        o_ref[:, lanes] = jnp.where(lo_half, outs[0], outs[1]).astype(o_ref.dtype)


def _fox_sample(q, kb, vb, cache_k, cache_v, fcum, nb, seq, past):
    hp = FOX_HEADS // 2
    lpad = fcum.shape[-1]
    rmap = lambda b: (b, 0)
    cmap = lambda b: (b, 0, 0)
    return pl.pallas_call(
        functools.partial(_fox_sample_kernel, past=past, seq=seq),
        grid=(nb,),
        in_specs=[pl.BlockSpec((seq, FOX_WIDTH), rmap), pl.BlockSpec((seq, FOX_WIDTH), rmap),
                  pl.BlockSpec((seq, FOX_WIDTH), rmap),
                  pl.BlockSpec((1, past, FOX_WIDTH), cmap), pl.BlockSpec((1, past, FOX_WIDTH), cmap),
                  pl.BlockSpec((1, hp, 2, lpad), lambda b: (b, 0, 0, 0))],
        out_specs=pl.BlockSpec((seq, FOX_WIDTH), rmap),
        out_shape=jax.ShapeDtypeStruct((nb * seq, FOX_WIDTH), BF16),
        compiler_params=_cparams(("parallel",)),
        name="fox_sample",
    )(q, kb, vb, cache_k, cache_v, fcum)


def _hgrn_kernel(*refs, tb, has_state):
    if has_state:
        q_ref, g_ref, i_ref, og_ref, gn_ref, tri_ref, s0_ref, y_ref, sfin_ref, *st_sc, b_sc, o_sc = refs
    else:
        q_ref, g_ref, i_ref, og_ref, gn_ref, tri_ref, y_ref, sfin_ref, *st_sc, b_sc, o_sc = refs
        s0_ref = None
    step = pl.program_id(1)
    nstep = pl.num_programs(1)

    @pl.when(step == 0)
    def _init():
        for h in range(HGRN_HEADS):
            if has_state:
                st_sc[h][...] = s0_ref[0, h].T
            else:
                st_sc[h][...] = jnp.zeros((HGRN_DK, HGRN_DK), F32)

    g = g_ref[...]
    tri = tri_ref[...]
    b = None
    for part in _split3(g):
        pb = jnp.dot(tri, part, preferred_element_type=F32)
        b = pb if b is None else b + pb
    b_sc[...] = b

    trow = lax.broadcasted_iota(I32, (SUB, 1), 0)

    def sub_chunk(c, carry):
        r0 = pl.multiple_of(c * SUB, SUB)
        for h in range(HGRN_HEADS):
            cs = slice(h * HGRN_DK, (h + 1) * HGRN_DK)
            q = q_ref[pl.ds(r0, SUB), cs]
            gg = g_ref[pl.ds(r0, SUB), cs]
            iv = i_ref[pl.ds(r0, SUB), cs]
            bb = b_sc[pl.ds(r0, SUB), cs]
            kk = 1.0 - jnp.exp(gg)
            st = st_sc[h][...]
            o = lax.dot_general((q * jnp.exp(bb)).astype(BF16), st.astype(BF16), NT_DIMS,
                                preferred_element_type=F32)
            for s in range(SUB):
                lo = 0 if s < 8 else 8
                ks = kk[s:s + 1, :]
                bs = bb[s:s + 1, :]
                ivs = iv[s:s + 1, :]
                e = jnp.exp(jnp.where(trow[lo:] >= s, bb[lo:] - bs, NEG_BIG))
                a = jnp.sum(q[lo:] * ks * e, axis=-1, keepdims=True)
                upd = a * ivs
                if lo:
                    upd = jnp.concatenate([jnp.zeros((8, HGRN_DK), F32), upd], axis=0)
                o = o + upd
            o_sc[pl.ds(r0, SUB), cs] = o
            bl = bb[SUB - 1:SUB, :]
            kd = kk * jnp.exp(bl - bb)
            u = lax.dot_general(iv.astype(BF16), kd.astype(BF16), TN_DIMS, preferred_element_type=F32)
            st_sc[h][...] = st * jnp.exp(bl) + u
        return carry

    lax.fori_loop(0, tb // SUB, sub_chunk, 0, unroll=min(4, tb // SUB))

    for h in range(HGRN_HEADS):
        cs = slice(h * HGRN_DK, (h + 1) * HGRN_DK)
        o = o_sc[:, cs]
        ms = jnp.mean(o * o, axis=-1, keepdims=True)
        y_ref[:, cs] = (o * lax.rsqrt(ms + NORM_EPS) * gn_ref[...] * og_ref[:, cs]).astype(y_ref.dtype)

    @pl.when(step == nstep - 1)
    def _fin():
        for h in range(HGRN_HEADS):
            sfin_ref[0, h] = st_sc[h][...].T


def _hgrn(qs, gl, iv, og, gn, s0, nb, seq, tb):
    ns = seq // tb
    t = nb * seq
    row = pl.BlockSpec((tb, HGRN_WIDTH), lambda b, i: (b * ns + i, 0))
    r = jnp.arange(tb)
    tri = ((r[:, None] // SUB == r[None, :] // SUB) & (r[None, :] <= r[:, None])).astype(BF16)
    in_specs = [row, row, row, row, _const_spec((1, HGRN_DK)), _const_spec((tb, tb))]
    args = [qs, gl, iv, og, gn, tri]
    if s0 is not None:
        in_specs.append(pl.BlockSpec((1, HGRN_HEADS, HGRN_DK, HGRN_DK), lambda b, i: (b, 0, 0, 0)))
        args.append(s0)
    return pl.pallas_call(
        functools.partial(_hgrn_kernel, tb=tb, has_state=s0 is not None),
        grid=(nb, ns), in_specs=in_specs,
        out_specs=[row, pl.BlockSpec((1, HGRN_HEADS, HGRN_DK, HGRN_DK), lambda b, i: (b, 0, 0, 0))],
        out_shape=[jax.ShapeDtypeStruct((t, HGRN_WIDTH), BF16),
                   jax.ShapeDtypeStruct((nb, HGRN_HEADS, HGRN_DK, HGRN_DK), F32)],
        scratch_shapes=[pltpu.VMEM((HGRN_DK, HGRN_DK), F32)] * HGRN_HEADS
                       + [pltpu.VMEM((tb, HGRN_WIDTH), F32), pltpu.VMEM((tb, HGRN_WIDTH), F32)],
        compiler_params=_cparams(("parallel", "arbitrary")), name="hgrn",
    )(*args)


def _post_kernel(x_ref, ya_ref, yb_ref, ga_ref, gb_ref, g1_ref, sh2_ref, sc2_ref, gn2_ref,
                 wpa_ref, wpb_ref, wo_ref, wrh_ref, wrl_ref, br_ref, tri_ref, cin_ref,
                 x1_ref, h2_ref, idx_ref, wt_ref, rank_ref, cout_ref, cnt_sc, *, tm):
    first = jnp.logical_and(pl.program_id(0) == 0, pl.program_id(1) == 0)

    @pl.when(first)
    def _init():
        cnt_sc[...] = cin_ref[...]

    merged = (ga_ref[...].astype(F32) * jnp.dot(ya_ref[...], wpa_ref[...], preferred_element_type=F32)
              + gb_ref[...].astype(F32) * jnp.dot(yb_ref[...], wpb_ref[...], preferred_element_type=F32))
    x1 = x_ref[...] + g1_ref[0] * jnp.dot(merged.astype(BF16), wo_ref[...], preferred_element_type=F32)
    x1_ref[...] = x1
    ms = jnp.mean(x1 * x1, axis=-1, keepdims=True)
    h2 = x1 * lax.rsqrt(ms + NORM_EPS) * gn2_ref[...]
    h2 = h2 * (1.0 + sc2_ref[0]) + sh2_ref[0]
    _store_rows(h2_ref, h2)

    hh = h2.astype(BF16)
    hl = (h2 - hh.astype(F32)).astype(BF16)
    wrh = wrh_ref[...]
    logits = (lax.dot_general(wrh, hh, NT_DIMS, preferred_element_type=F32)
              + lax.dot_general(wrh, hl, NT_DIMS, preferred_element_type=F32)
              + lax.dot_general(wrl_ref[...], hh, NT_DIMS, preferred_element_type=F32))
    scores = _sigmoid(logits)
    biased = scores + br_ref[...]

    b3 = biased.reshape(N_GROUPS, GROUP_SIZE, tm)
    it3 = lax.broadcasted_iota(I32, (N_GROUPS, GROUP_SIZE, tm), 1).astype(F32)
    m1 = jnp.max(b3, axis=1, keepdims=True)
    i1 = jnp.min(jnp.where(b3 == m1, it3, float(GROUP_SIZE)), axis=1, keepdims=True)
    m2 = jnp.max(jnp.where(it3 == i1, -jnp.inf, b3), axis=1, keepdims=True)
    gs = (m1 + m2).reshape(N_GROUPS, tm)

    gi = lax.broadcasted_iota(I32, (N_GROUPS, tm), 0)
    beat = jnp.zeros((N_GROUPS, tm), F32)
    for g in range(N_GROUPS):
        r = gs[g:g + 1, :]
        beat = beat + jnp.where((r > gs) | ((r == gs) & (g < gi)), 1.0, 0.0)
    gpen = jnp.where(beat < TOPK_GROUPS, 0.0, -jnp.inf)
    masked = (b3 + gpen.reshape(N_GROUPS, 1, tm)).reshape(N_EXPERTS, tm)

    ei = lax.broadcasted_iota(I32, (N_EXPERTS, tm), 0).astype(F32)
    idx_rows, w_rows = [], []
    chosen = jnp.zeros((N_EXPERTS, tm), F32)
    for _ in range(TOP_K):
        m = jnp.max(masked, axis=0, keepdims=True)
        ik = jnp.min(jnp.where(masked == m, ei, float(N_EXPERTS)), axis=0, keepdims=True)
        hit = ei == ik
        w_rows.append(jnp.sum(jnp.where(hit, scores, 0.0), axis=0, keepdims=True))
        idx_rows.append(ik)
        masked = jnp.where(hit, -jnp.inf, masked)
        chosen = jnp.where(hit, 1.0, chosen)
    idx_ref[...] = jnp.concatenate(idx_rows, axis=0).astype(I32)
    wts = jnp.concatenate(w_rows, axis=0)
    wt_ref[...] = wts / jnp.sum(wts, axis=0, keepdims=True) * ROUTED_SCALE

    before = cnt_sc[...] + jnp.dot(chosen.astype(BF16), tri_ref[...], preferred_element_type=F32)
    rank_rows = [jnp.sum(jnp.where(ei == idx_rows[kk], before, 0.0), axis=0, keepdims=True)
                 for kk in range(TOP_K)]
    rank_ref[...] = jnp.concatenate(rank_rows, axis=0).astype(I32)
    cnt_sc[...] = cnt_sc[...] + jnp.sum(chosen, axis=1, keepdims=True)
    cout_ref[...] = cnt_sc[...]


def _post(x2, ya, yb, ga, gb, gate1, shift2, scale2, prm, count_in, nb, tm):
    t, d = x2.shape
    ns = t // nb // tm
    row = lambda w: pl.BlockSpec((tm, w), lambda b, i: (b * ns + i, 0))
    colb = pl.BlockSpec((TOP_K, tm), lambda b, i: (0, b * ns + i))
    r = jnp.arange(tm)
    tri = (r[:, None] < r[None, :]).astype(BF16)
    in_specs = [row(d), row(FOX_WIDTH), row(HGRN_WIDTH), row(d), row(d),
                _mod_spec(gate1, tm, ns), _mod_spec(shift2, tm, ns), _mod_spec(scale2, tm, ns),
                _const_spec((1, d)), _const_spec((FOX_WIDTH, d)), _const_spec((HGRN_WIDTH, d)),
                _const_spec((d, d)), _const_spec((N_EXPERTS, d)), _const_spec((N_EXPERTS, d)),
                _const_spec((N_EXPERTS, 1)), _const_spec((tm, tm)), _const_spec((N_EXPERTS, 1))]
    out_shape = [jax.ShapeDtypeStruct((t, d), F32), jax.ShapeDtypeStruct((t * ROW_CHUNKS, LANES), F32),
                 jax.ShapeDtypeStruct((TOP_K, t), I32), jax.ShapeDtypeStruct((TOP_K, t), F32),
                 jax.ShapeDtypeStruct((TOP_K, t), I32), jax.ShapeDtypeStruct((N_EXPERTS, 1), F32)]
    out_specs = [row(d), pl.BlockSpec((tm * ROW_CHUNKS, LANES), lambda b, i: (b * ns + i, 0)),
                 colb, colb, colb, pl.BlockSpec((N_EXPERTS, 1), lambda b, i: (0, 0))]
    return pl.pallas_call(
        functools.partial(_post_kernel, tm=tm), grid=(nb, ns), in_specs=in_specs,
        out_specs=out_specs, out_shape=out_shape,
        scratch_shapes=[pltpu.VMEM((N_EXPERTS, 1), F32)],
        compiler_params=_cparams(("arbitrary", "arbitrary")), name="post",
    )(x2, ya, yb, ga, gb, gate1, shift2, scale2, prm["g2"], prm["wpa"], prm["wpb"], prm["wo"],
      prm["wrh"], prm["wrl"], prm["br"], tri, count_in)


def _pos_kernel(idx_ref, rank_ref, pst_ref, pos_ref, *, tm):
    ei = lax.broadcasted_iota(I32, (N_EXPERTS, tm), 0)
    pst = pst_ref[...]
    rows = [jnp.sum(jnp.where(ei == idx_ref[kk:kk + 1, :], pst, 0.0), axis=0, keepdims=True)
            for kk in range(TOP_K)]
    pos_ref[...] = jnp.concatenate(rows, axis=0).astype(I32) + rank_ref[...]


def _pos(idx, rank, pstart_col, tm):
    t = idx.shape[1]
    colb = pl.BlockSpec((TOP_K, tm), lambda i: (0, i))
    return pl.pallas_call(
        functools.partial(_pos_kernel, tm=tm), grid=(t // tm,),
        in_specs=[colb, colb, _const_spec((N_EXPERTS, 1))], out_specs=colb,
        out_shape=jax.ShapeDtypeStruct((TOP_K, t), I32),
        compiler_params=_cparams(("parallel",)), name="moe_pos",
    )(idx, rank, pstart_col)


def _padfill_kernel(cnt_ref, pad_ref, pst_ref, xs_ref, zero_sc, sem):
    zero_sc[...] = jnp.zeros(zero_sc.shape, zero_sc.dtype)
    sizes = [1 << b for b in reversed(range(MOE_BLK.bit_length() - 1))]

    def chunks(e, act):
        n = pad_ref[e] - cnt_ref[e]
        row = pst_ref[e] + cnt_ref[e]
        for size in sizes:
            @pl.when((n & size) != 0)
            def _():
                act(pltpu.make_async_copy(zero_sc.at[pl.ds(0, size)], xs_ref.at[pl.ds(row, size)], sem))

            row = row + (n & size)

    def issue(e, c):
        chunks(e, lambda cp: cp.start())
        return c

    def drain(e, c):
        chunks(e, lambda cp: cp.wait())
        return c

    lax.fori_loop(0, N_EXPERTS, issue, 0)
    lax.fori_loop(0, N_EXPERTS, drain, 0)


def _padfill(counts, padded, pstart, n_rows, d):
    row_tile = (d // LANES, LANES)
    return pl.pallas_call(
        _padfill_kernel,
        grid_spec=pltpu.PrefetchScalarGridSpec(
            num_scalar_prefetch=3, grid=(1,), in_specs=[],
            out_specs=pl.BlockSpec(memory_space=pl.ANY),
            scratch_shapes=[pltpu.VMEM((MOE_BLK // 2,) + row_tile, F32), pltpu.SemaphoreType.DMA(())]),
        out_shape=jax.ShapeDtypeStruct((n_rows,) + row_tile, F32),
        compiler_params=pltpu.CompilerParams(dimension_semantics=("arbitrary",), has_side_effects=True),
        name="moe_padfill",
    )(counts, padded, pstart)


def _dispatch_kernel(pos_ref, h2_hbm, xs_in_ref, xs_ref, ring, load_sem, out_sem, *, tm):
    del xs_in_ref
    i = pl.program_id(0)
    last = pl.num_programs(0) - 1
    cur = i % 3
    par = i % 2

    def load(step, slot):
        return pltpu.make_async_copy(h2_hbm.at[pl.ds(step * tm, tm)], ring.at[slot], load_sem.at[slot])

    def drain(s):
        for kk in range(TOP_K):
            pltpu.make_async_copy(ring.at[0], xs_ref.at[pl.ds(0, tm)], out_sem.at[s]).wait()

    @pl.when(i == 0)
    def _():
        load(0, 0).start()

    @pl.when(i < last)
    def _():
        load(i + 1, (i + 1) % 3).start()

    load(i, cur).wait()
    for t in range(tm):
        for kk in range(TOP_K):
            pltpu.make_async_copy(ring.at[cur, t], xs_ref.at[pos_ref[kk, t]],
                                  out_sem.at[par]).start(priority=kk % 2)

    @pl.when(i > 0)
    def _():
        drain(1 - par)

    @pl.when(i == last)
    def _():
        drain(par)


def _dispatch(pos, h2_flat, xs, tm):
    h2 = h2_flat.reshape(-1, ROW_CHUNKS, LANES)
    t = h2.shape[0]
    return pl.pallas_call(
        functools.partial(_dispatch_kernel, tm=tm),
        grid=(t // tm,),
        in_specs=[pl.BlockSpec((TOP_K, tm), lambda i: (0, i), memory_space=pltpu.SMEM),
                  pl.BlockSpec(memory_space=pl.ANY),
                  pl.BlockSpec(memory_space=pl.ANY)],
        out_specs=pl.BlockSpec(memory_space=pl.ANY),
        out_shape=jax.ShapeDtypeStruct(xs.shape, xs.dtype),
        scratch_shapes=[pltpu.VMEM((3, tm) + h2.shape[1:], F32), pltpu.SemaphoreType.DMA((3,)),
                        pltpu.SemaphoreType.DMA((2,))],
        input_output_aliases={2: 0},
        compiler_params=pltpu.CompilerParams(dimension_semantics=("arbitrary",), has_side_effects=True,
                                             vmem_limit_bytes=VMEM_LIMIT),
        name="moe_dispatch",
    )(pos, h2, xs)


def _moe_kernel(be_ref, nu_ref, first_ref, slot_ref, nxt_ref, x_ref, wg_ref, wu_ref, wd_ref, y_ref,
                wg_buf, wu_buf, wd_buf, wgu_sc, wd_sc, sem):
    i = pl.program_id(0)
    e = be_ref[i]

    def fetch(expert, slot):
        return (pltpu.make_async_copy(wg_ref.at[expert], wg_buf.at[slot], sem.at[slot]),
                pltpu.make_async_copy(wu_ref.at[expert], wu_buf.at[slot], sem.at[slot]),
                pltpu.make_async_copy(wd_ref.at[expert], wd_buf.at[slot], sem.at[slot]))

    @pl.when(first_ref[i] == 1)
    def _switch_expert():
        slot = slot_ref[i]

        @pl.when(i == 0)
        def _():
            for cp in fetch(e, slot):
                cp.start()

        for cp in fetch(e, slot):
            cp.wait()
        nxt = nxt_ref[e]

        @pl.when(nxt >= 0)
        def _():
            for cp in fetch(nxt, 1 - slot):
                cp.start()

        wgu_sc[:, 0:D_EXPERT] = wg_buf[slot].astype(BF16)
        wgu_sc[:, D_EXPERT:2 * D_EXPERT] = wu_buf[slot].astype(BF16)
        wd_sc[...] = wd_buf[slot].astype(BF16)

    @pl.when(i < nu_ref[0])
    def _compute():
        gu = jnp.dot(_load_rows(x_ref).astype(BF16), wgu_sc[...], preferred_element_type=F32)
        gt = gu[:, 0:D_EXPERT]
        h = gt * _sigmoid(gt) * gu[:, D_EXPERT:2 * D_EXPERT]
        _store_rows(y_ref, jnp.dot(h.astype(BF16), wd_sc[...], preferred_element_type=F32))


def _moe(blk_expert, n_used, first, slot, nxt, xs_pool, w_gate, w_up, w_down):
    n_rows = xs_pool.shape[0]
    xs = xs_pool.reshape(n_rows * ROW_CHUNKS, LANES)
    d = w_gate.shape[1]
    nblk = n_rows // MOE_BLK
    xmap = lambda i, be, nu, fi, sl, nx: (jnp.minimum(i, nu[0] - 1), 0)
    hbm = pl.BlockSpec(memory_space=pl.ANY)
    return pl.pallas_call(
        _moe_kernel,
        grid_spec=pltpu.PrefetchScalarGridSpec(
            num_scalar_prefetch=5, grid=(nblk,),
            in_specs=[pl.BlockSpec((MOE_BLK * ROW_CHUNKS, LANES), xmap), hbm, hbm, hbm],
            out_specs=pl.BlockSpec((MOE_BLK * ROW_CHUNKS, LANES), xmap),
            scratch_shapes=[pltpu.VMEM((2, d, D_EXPERT), F32), pltpu.VMEM((2, d, D_EXPERT), F32),
                            pltpu.VMEM((2, D_EXPERT, d), F32),
                            pltpu.VMEM((d, 2 * D_EXPERT), BF16), pltpu.VMEM((D_EXPERT, d), BF16),
                            pltpu.SemaphoreType.DMA((2,))]),
        out_shape=jax.ShapeDtypeStruct(xs.shape, F32),
        compiler_params=_cparams(("arbitrary",)), name="moe_experts",
    )(blk_expert, n_used, first, slot, nxt, xs, w_gate, w_up, w_down)


def _combine_kernel(pos_ref, posn_ref, x1_ref, h2_ref, g2_ref, wt_ref, wsgu_ref, wsd_ref, ys_ref, ysflat_ref,
                    o_ref, buf, sem, *, tm):
    lin = pl.program_id(0) * pl.num_programs(1) + pl.program_id(1)
    total = pl.num_programs(0) * pl.num_programs(1)
    slot = lin % 2

    def issue(p_ref, s):
        for t in range(tm):
            for kk in range(TOP_K):
                pltpu.make_async_copy(ys_ref.at[p_ref[kk, t]], buf.at[s, kk, pl.ds(t * ROW_CHUNKS, ROW_CHUNKS)],
                                      sem.at[s]).start(priority=kk % 2)

    @pl.when(lin == 0)
    def _():
        issue(pos_ref, 0)

    gu = jnp.dot(_load_rows(h2_ref).astype(BF16), wsgu_ref[...], preferred_element_type=F32)
    gt = gu[:, 0:D_SHARED]
    hs = gt * _sigmoid(gt) * gu[:, D_SHARED:2 * D_SHARED]
    shared = jnp.dot(hs.astype(BF16), wsd_ref[...], preferred_element_type=F32)
    wcols = [jnp.broadcast_to(wt_ref[kk:kk + 1, :], (LANES, tm)).T for kk in range(TOP_K)]

    issue(posn_ref, 1 - slot)

    for kk in range(TOP_K):
        pltpu.make_async_copy(ysflat_ref.at[pl.ds(0, tm * ROW_CHUNKS)], buf.at[slot, kk], sem.at[slot]).wait()

    nchunk = shared.shape[-1] // LANES
    acc = [shared[:, c * LANES:(c + 1) * LANES] for c in range(nchunk)]
    for kk in range(TOP_K):
        for c in range(nchunk):
            acc[c] = acc[c] + _row_chunk(buf.at[slot, kk], c)[...] * wcols[kk]
    o_ref[...] = x1_ref[...] + g2_ref[0] * jnp.concatenate(acc, axis=1)

    @pl.when(lin + 1 == total)
    def _():
        for kk in range(TOP_K):
            pltpu.make_async_copy(ysflat_ref.at[pl.ds(0, tm * ROW_CHUNKS)], buf.at[1 - slot, kk],
                                  sem.at[1 - slot]).wait()


def _combine(pos, x1, h2, gate2, wts, wsgu, wsd, ys_flat, nb, tm):
    t, d = x1.shape
    ns = t // nb // tm
    row = pl.BlockSpec((tm, d), lambda b, i: (b * ns + i, 0))
    last = nb * ns - 1
    ys = ys_flat.reshape(-1, ROW_CHUNKS, LANES)
    return pl.pallas_call(
        functools.partial(_combine_kernel, tm=tm),
        grid=(nb, ns),
        in_specs=[pl.BlockSpec((TOP_K, tm), lambda b, i: (0, b * ns + i), memory_space=pltpu.SMEM),
                  pl.BlockSpec((TOP_K, tm), lambda b, i: (0, jnp.minimum(b * ns + i + 1, last)),
                               memory_space=pltpu.SMEM),
                  row, pl.BlockSpec((tm * ROW_CHUNKS, LANES), lambda b, i: (b * ns + i, 0)),
                  _mod_spec(gate2, tm, ns),
                  pl.BlockSpec((TOP_K, tm), lambda b, i: (0, b * ns + i)),
                  _const_spec(wsgu.shape), _const_spec(wsd.shape),
                  pl.BlockSpec(memory_space=pl.ANY), pl.BlockSpec(memory_space=pl.ANY)],
        out_specs=row,
        out_shape=jax.ShapeDtypeStruct((t, d), F32),
        scratch_shapes=[pltpu.VMEM((2, TOP_K, tm * ROW_CHUNKS, LANES), F32), pltpu.SemaphoreType.DMA((2,))],
        compiler_params=_cparams(("arbitrary", "arbitrary")), name="moe_combine",
    )(pos, pos, x1, h2, gate2, wts, wsgu, wsd, ys, ys_flat)


def _prepare_params(g_norm1, w_in, b_fox_f, g_q, g_k, hgrn_lb, g_hgrn_o, w_proj_a, w_proj_b, w_out,
                    g_norm2, w_router, b_router, w_sh_gate, w_sh_up, w_sh_down):
    d = D_MODEL
    fw, hw = FOX_WIDTH, HGRN_WIDTH
    c0 = 3 * fw
    c1 = c0 + FOX_HEADS
    c2 = c1 + 4 * hw
    wff = w_in[:, c0:c1]
    head = jnp.arange(fw) // FOX_HEAD_DIM
    wr_t = w_router.T
    wrh = wr_t.astype(BF16)
    return dict(
        g1=g_norm1.reshape(1, d),
        wqkv=w_in[:, :c0].astype(BF16),
        wff=jnp.pad(wff, ((0, 0), (0, LANES - FOX_HEADS))).astype(BF16),
        wfft=jnp.pad(wff.T, ((0, 16 - FOX_HEADS), (0, 0))).astype(BF16),
        bf=b_fox_f.reshape(1, FOX_HEADS), bft=b_fox_f.reshape(FOX_HEADS, 1),
        gq=jnp.tile(g_q, FOX_HEADS).reshape(1, fw), gk=jnp.tile(g_k, FOX_HEADS).reshape(1, fw),
        bd=(head[:, None] == head[None, :]).astype(BF16) * (1.0 / FOX_HEAD_DIM),
        lbp=hgrn_lb,
        wh=w_in[:, c1:c2].astype(BF16), wg=w_in[:, c2:].astype(BF16),
        gn=g_hgrn_o.reshape(1, HGRN_DK),
        g2=g_norm2.reshape(1, d),
        wpa=w_proj_a.astype(BF16), wpb=w_proj_b.astype(BF16), wo=w_out.astype(BF16),
        wrh=wrh, wrl=(wr_t - wrh.astype(F32)).astype(BF16), br=b_router.reshape(N_EXPERTS, 1),
        wsgu=jnp.concatenate([w_sh_gate, w_sh_up], axis=1).astype(BF16), wsd=w_sh_down.astype(BF16),
    )


def _mixers(x2, nb, seq, tm, shift1, scale1, prm, past, tile, tb):
    (q, k, kb, v, vb, lf, lft, hq, hl, hi, og, ga, gb) = _inproj(x2, nb if shift1.shape[1] == 1 else 1, tm,
                                                                 shift1, scale1, prm)
    t = nb * seq
    hp = FOX_HEADS // 2
    lft = lft.transpose(1, 0, 2).reshape(FOX_HEADS, nb, seq).transpose(1, 0, 2)
    if past is None:
        fcum = _cumsum_lanes(lft.reshape(nb * FOX_HEADS, seq)).reshape(nb, hp, 2, seq)
        ya = _fox_prompt(q, kb, vb, fcum, nb, seq, tile, min(32, tile))
        s0 = None
    else:
        cache_k, cache_v, cache_lf, s0 = past
        plen = cache_k.shape[1]
        ltot = plen + seq
        lpad = -(-ltot // LANES) * LANES
        lf_all = jnp.concatenate([cache_lf.transpose(0, 2, 1), lft,
                                  jnp.zeros((nb, FOX_HEADS, lpad - ltot), F32)], axis=-1)
        fcum = _cumsum_lanes(lf_all.reshape(nb * FOX_HEADS, lpad)).reshape(nb, hp, 2, lpad)
        ya = _fox_sample(q, kb, vb, cache_k.reshape(nb, plen, FOX_WIDTH), cache_v.reshape(nb, plen, FOX_WIDTH),
                         fcum, nb, seq, plen)
    yb, sfin = _hgrn(hq, hl, hi, og, prm["gn"], s0, nb, seq, tb)
    return k, v, lf, sfin, ya, yb, ga, gb


def kernel(x_prompt, x_sample, cache_fox_k, cache_fox_v, cache_fox_logf, state_hgrn, c_prompt, c_sample,
           w_ada, b_ada, g_norm1, w_in, b_fox_f, g_q, g_k, hgrn_lb, g_hgrn_o, w_proj_a, w_proj_b, w_out,
           g_norm2, w_router, b_router, w_exp_gate, w_exp_up, w_exp_down, w_sh_gate, w_sh_up, w_sh_down):
    assert w_ada.shape[0] == 1 and hgrn_lb.shape[0] == 2, "single-layer trunk"
    d = D_MODEL
    bp, sp, _ = x_prompt.shape
    bs, ss, _ = x_sample.shape
    tp, ts = bp * sp, bs * ss
    prm = _prepare_params(g_norm1[0], w_in[0], b_fox_f[0], g_q[0], g_k[0], hgrn_lb, g_hgrn_o[0],
                          w_proj_a[0], w_proj_b[0], w_out[0], g_norm2[0], w_router[0], b_router[0],
                          w_sh_gate[0], w_sh_up[0], w_sh_down[0])

    bc = bp + bs
    bc_pad = -(-bc // 8) * 8
    c_all = jnp.concatenate([c_prompt, c_sample, jnp.zeros((bc_pad - bc, d), F32)], axis=0)
    mod = _ada(c_all, w_ada[0], b_ada[0])
    mod_p = [mod[:bp, j * d:(j + 1) * d].reshape(bp, 1, d) for j in range(6)]
    mod_s = [jnp.repeat(mod[bp:bc, j * d:(j + 1) * d], ss, axis=0).reshape(1, ts, d) for j in range(6)]

    tm_p = min(512, sp)
    tm_s = min(256, ts)
    xp2 = x_prompt.reshape(tp, d)
    xs2 = x_sample.reshape(ts, d)
    tile = min(1024, sp)
    tb = min(256, sp)

    kp, vp, lfp, sfin_p, ya_p, yb_p, ga_p, gb_p = _mixers(
        xp2, bp, sp, tm_p, mod_p[0], mod_p[1], prm, None, tile, tb)
    past = (cache_fox_k[0], cache_fox_v[0], cache_fox_logf[0], state_hgrn[0])
    ks, vs, lfs, sfin_s, ya_s, yb_s, ga_s, gb_s = _mixers(
        xs2, bs, ss, tm_s, mod_s[0], mod_s[1], prm, past, None, min(256, ss))

    zero_cnt = jnp.zeros((N_EXPERTS, 1), F32)
    x1_p, h2_p, idx_p, wt_p, rank_p, cnt_p = _post(xp2, ya_p, yb_p, ga_p, gb_p, mod_p[2], mod_p[3], mod_p[4],
                                                    prm, zero_cnt, bp, tm_p)
    x1_s, h2_s, idx_s, wt_s, rank_s, cnt_all = _post(xs2, ya_s, yb_s, ga_s, gb_s, mod_s[2], mod_s[3], mod_s[4],
                                                     prm, cnt_p, 1, tm_s)

    counts = cnt_all.reshape(N_EXPERTS).astype(I32)
    padded = (counts + MOE_BLK - 1) // MOE_BLK * MOE_BLK
    pend = jnp.cumsum(padded)
    pstart = pend - padded
    nblk = -(-((tp + ts) * TOP_K + N_EXPERTS * (MOE_BLK - 1)) // MOE_BLK)
    n_used = (pend[-1] // MOE_BLK).reshape(1)
    blk_row0 = jnp.arange(nblk, dtype=I32) * MOE_BLK
    blk_expert = jnp.minimum(jnp.sum((pend[None, :] <= blk_row0[:, None]).astype(I32), axis=1), N_EXPERTS - 1)
    blk_used = jnp.arange(nblk) < n_used[0]
    blk_expert = jnp.where(blk_used, blk_expert, blk_expert[jnp.maximum(n_used[0] - 1, 0)])
    blk_first = blk_used & jnp.concatenate([jnp.ones((1,), bool), blk_expert[1:] != blk_expert[:-1]])
    blk_slot = ((jnp.cumsum(blk_first.astype(I32)) - 1) & 1).astype(I32)
    eids = jnp.arange(N_EXPERTS, dtype=I32)
    later = (eids[None, :] > eids[:, None]) & (padded[None, :] > 0)
    nxt_expert = jnp.min(jnp.where(later, eids[None, :], N_EXPERTS), axis=1)
    nxt_expert = jnp.where(nxt_expert < N_EXPERTS, nxt_expert, -1).astype(I32)
    pstart_col = pstart.astype(F32).reshape(N_EXPERTS, 1)
    pos_p = _pos(idx_p, rank_p, pstart_col, tm_p)
    pos_s = _pos(idx_s, rank_s, pstart_col, tm_s)

    xs_pool = _padfill(counts, padded, pstart, nblk * MOE_BLK, d)
    xs_pool = _dispatch(pos_p, h2_p, xs_pool, min(128, tm_p))
    xs_pool = _dispatch(pos_s, h2_s, xs_pool, min(128, tm_s))
    ys_pool = _moe(blk_expert, n_used, blk_first.astype(I32), blk_slot, nxt_expert, xs_pool,
                   w_exp_gate[0], w_exp_up[0], w_exp_down[0])

    tm_c = min(128, sp)
    y_p = _combine(pos_p, x1_p, h2_p, mod_p[5], wt_p, prm["wsgu"], prm["wsd"], ys_pool, bp, tm_c)
    y_s = _combine(pos_s, x1_s, h2_s, mod_s[5], wt_s, prm["wsgu"], prm["wsd"], ys_pool, 1, min(128, ts))

    return (y_p.reshape(bp, sp, d), y_s.reshape(bs, ss, d),
            kp.reshape(1, bp, sp, FOX_HEADS, FOX_HEAD_DIM), vp.reshape(1, bp, sp, FOX_HEADS, FOX_HEAD_DIM),
            lfp.reshape(1, bp, sp, FOX_HEADS), sfin_p[None],
            ks.reshape(1, bs, ss, FOX_HEADS, FOX_HEAD_DIM), vs.reshape(1, bs, ss, FOX_HEADS, FOX_HEAD_DIM),
            lfs.reshape(1, bs, ss, FOX_HEADS), sfin_s[None])
```
